```python
import math
import jax, jax.numpy as jnp
from jax import lax
import numpy as np

D_MODEL = 1024
BATCH = 4
SEQ = 4096
DEPTH = 1
DEC_BATCH = 128
DEC_SEQ = 8
PAST_LEN = 16384
PAGE_SIZE = 128

GDN_HEADS = 8
GDN_DK = 128
GDN_DV = 128
GDN_QK_DIM = GDN_HEADS * GDN_DK
GDN_V_DIM = GDN_HEADS * GDN_DV
CONV_W = 4
GDN_CHUNK = 64
CONV_DIM = 2 * GDN_QK_DIM + GDN_V_DIM
SWA_Q_HEADS = 16
SWA_KV_HEADS = 4
SWA_GROUP = SWA_Q_HEADS // SWA_KV_HEADS
SWA_HEAD_DIM = 64
WINDOW = 128
ROPE_DIM = SWA_HEAD_DIM // 4
ROPE_THETA = 500000.0
D_FF = 2816
EPS = 1e-6
IN_SPLITS = (CONV_DIM, GDN_V_DIM, GDN_HEADS, GDN_HEADS, SWA_Q_HEADS * SWA_HEAD_DIM,
             SWA_KV_HEADS * SWA_HEAD_DIM, SWA_KV_HEADS * SWA_HEAD_DIM, D_MODEL, D_MODEL)
D_IN = sum(IN_SPLITS)

kernel_name = 'hybrid_gdn_swa_macaron_step'


def rmsnorm(x, w):
    x32 = x.astype(jnp.float32)
    y = x32 * lax.rsqrt(jnp.mean(x32 * x32, axis=-1, keepdims=True) + EPS)
    return (y * w.astype(jnp.float32)).astype(x.dtype)


def l2norm(x):
    x32 = x.astype(jnp.float32)
    return x32 * lax.rsqrt(jnp.sum(x32 * x32, axis=-1, keepdims=True) + EPS)


def swiglu(x, w_in, w_out):
    gate, up = jnp.split(x @ w_in, 2, axis=-1)
    return (jax.nn.silu(gate) * up) @ w_out


def rope_partial(x, pos):
    inv = ROPE_THETA ** (-jnp.arange(0, ROPE_DIM, 2, dtype=jnp.float32) / ROPE_DIM)
    ang = pos.astype(jnp.float32)[:, None] * inv[None, :]
    cos = jnp.cos(ang)[None, :, None, :]
    sin = jnp.sin(ang)[None, :, None, :]
    xr = x[..., :ROPE_DIM].astype(jnp.float32)
    x1, x2 = xr[..., :ROPE_DIM // 2], xr[..., ROPE_DIM // 2:]
    rot = jnp.concatenate([x1 * cos - x2 * sin, x2 * cos + x1 * sin], axis=-1)
    return jnp.concatenate([rot.astype(x.dtype), x[..., ROPE_DIM:]], axis=-1)


def sink_attention(q, k, v, mask, sinks):
    s = jnp.einsum('...qhgd,...khd->...hgqk', q, k, preferred_element_type=jnp.float32)
    s = jnp.where(mask, s * (SWA_HEAD_DIM ** -0.5), -jnp.inf)
    sink = sinks.astype(jnp.float32)[:, :, None, None]
    m = jnp.maximum(jnp.max(s, axis=-1, keepdims=True), sink)
    p = jnp.exp(s - m)
    denom = jnp.sum(p, axis=-1, keepdims=True) + jnp.exp(sink - m)
    return jnp.einsum('...hgqk,...khd->...qhgd', (p / denom).astype(v.dtype), v)


def swa_prompt(q, k, v, sinks):
    B, L, _, D = q.shape
    nb = L // WINDOW
    qb = q.reshape(B, nb, WINDOW, SWA_KV_HEADS, SWA_GROUP, D)

    def with_prev(t):
        t = t.reshape(B, nb, WINDOW, SWA_KV_HEADS, D)
        prev = jnp.pad(t[:, :-1], ((0, 0), (1, 0), (0, 0), (0, 0), (0, 0)))
        return jnp.concatenate([prev, t], axis=2)

    qi = jnp.arange(WINDOW)
    kj = jnp.arange(2 * WINDOW) - WINDOW
    rel = qi[:, None] - kj[None, :]
    band = (rel >= 0) & (rel < WINDOW)
    has_prev = (jnp.arange(nb)[:, None] > 0) | (kj[None, :] >= 0)
    mask = band[None] & has_prev[:, None, :]
    o = sink_attention(qb, with_prev(k), with_prev(v), mask[None, :, None, None], sinks)
    return o.reshape(B, L, SWA_Q_HEADS, D)


def swa_sample(q, k, v, k_buf, v_buf, sinks):
    B, T, _, D = q.shape
    n_rows = k_buf.shape[1]
    kc = jnp.concatenate([k_buf.astype(k.dtype), k], axis=1)
    vc = jnp.concatenate([v_buf.astype(v.dtype), v], axis=1)
    qpos = PAST_LEN + jnp.arange(T)
    kpos = PAST_LEN - n_rows + jnp.arange(n_rows + T)
    rel = qpos[:, None] - kpos[None, :]
    mask = (rel >= 0) & (rel < WINDOW)
    o = sink_attention(q.reshape(B, T, SWA_KV_HEADS, SWA_GROUP, D), kc, vc, mask[None, None, None], sinks)
    return o.reshape(B, T, SWA_Q_HEADS, D), kc[:, -n_rows:], vc[:, -n_rows:]


def gated_delta_rule(q, k, v, g, beta, S0):
    f32 = jnp.float32
    B, L, H, DK = q.shape
    DV = v.shape[-1]
    C = min(GDN_CHUNK, L)
    N = -(-L // C)
    pad = N * C - L

    def blocks(t):
        t = jnp.moveaxis(t.astype(f32), 2, 1)
        t = jnp.pad(t, [(0, 0), (0, 0), (0, pad)] + [(0, 0)] * (t.ndim - 3))
        return t.reshape(t.shape[:2] + (N, C) + t.shape[3:])

    qc = blocks(q) * (DK ** -0.5)
    kc = blocks(k)
    vc = blocks(v)
    gc = jnp.cumsum(blocks(g), axis=-1)
    bc = blocks(beta)
    idx = jnp.arange(C)
    incl = idx[:, None] >= idx[None, :]
    strict = idx[:, None] > idx[None, :]
    decay = jnp.exp(jnp.where(incl, gc[..., :, None] - gc[..., None, :], -jnp.inf))
    kb = kc * bc[..., None]
    vb = vc * bc[..., None]
    A = jnp.where(strict, jnp.einsum('bhnid,bhnjd->bhnij', kb, kc) * decay, 0.0)
    eye = jnp.eye(C, dtype=f32)
    T = lax.linalg.triangular_solve(A + eye, jnp.broadcast_to(eye, A.shape), left_side=True, lower=True)
    u_val = jnp.einsum('bhnij,bhnjv->bhniv', T, vb)
    w_key = jnp.einsum('bhnij,bhnjd->bhnid', T, kb * jnp.exp(gc)[..., None])
    qk = jnp.einsum('bhnid,bhnjd->bhnij', qc, kc) * decay
    qg = qc * jnp.exp(gc)[..., None]
    kg = kc * jnp.exp(gc[..., -1:] - gc)[..., None]
    g_last = jnp.exp(gc[..., -1])

    def step(S, xs):
        qk_i, qg_i, kg_i, u_i, w_i, gl_i = xs
        v_new = u_i - jnp.einsum('bhcd,bhdv->bhcv', w_i, S)
        o = jnp.einsum('bhcd,bhdv->bhcv', qg_i, S) + jnp.einsum('bhij,bhjv->bhiv', qk_i, v_new)
        S = S * gl_i[..., None, None] + jnp.einsum('bhcd,bhcv->bhdv', kg_i, v_new)
        return S, o

    xs = tuple(jnp.moveaxis(t, 2, 0) for t in (qk, qg, kg, u_val, w_key, g_last))
    S, o = lax.scan(step, S0.astype(f32), xs)
    o = jnp.moveaxis(o, 0, 2).reshape(B, H, N * C, DV)[:, :, :L]
    return jnp.moveaxis(o, 1, 2), S


def decoder_layer(x, pos, conv_buf, S0, k_buf, v_buf, p):
    B, L, _ = x.shape
    x = x + 0.5 * swiglu(rmsnorm(x, p['norm_ffn1']), p['w_ffn1_in'], p['w_ffn1_out'])
    u = rmsnorm(x, p['norm_mix']) @ p['w_in']
    qkv_raw, z, a, b, q_s, k_s, v_s, gate_a, gate_b = jnp.split(u, np.cumsum(IN_SPLITS)[:-1].tolist(), axis=-1)
    ext = jnp.concatenate([conv_buf.astype(qkv_raw.dtype), qkv_raw], axis=1)
    new_conv = ext[:, -(CONV_W - 1):]
    conv_w = p['conv_w']
    qkv = jax.nn.silu(sum(conv_w[j] * ext[:, j:j + L] for j in range(CONV_W)))
    q_g, k_g, v_g = jnp.split(qkv, [GDN_QK_DIM, 2 * GDN_QK_DIM], axis=-1)
    q_g = l2norm(q_g.reshape(B, L, GDN_HEADS, GDN_DK))
    k_g = l2norm(k_g.reshape(B, L, GDN_HEADS, GDN_DK))
    v_g = v_g.reshape(B, L, GDN_HEADS, GDN_DV)
    g = -jnp.exp(p['gdn_a_log'].astype(jnp.float32)) * jax.nn.softplus(
        a.astype(jnp.float32) + p['gdn_dt_bias'].astype(jnp.float32))
    beta = jax.nn.sigmoid(b.astype(jnp.float32))
    o_g, new_S = gated_delta_rule(q_g, k_g, v_g, g, beta, S0)
    o_g = rmsnorm(o_g.astype(x.dtype), p['gdn_norm']) * jax.nn.silu(z.reshape(B, L, GDN_HEADS, GDN_DV))
    o_g = o_g.reshape(B, L, GDN_V_DIM)
    q_s = rope_partial(q_s.reshape(B, L, SWA_Q_HEADS, SWA_HEAD_DIM), pos)
    k_s = rope_partial(k_s.reshape(B, L, SWA_KV_HEADS, SWA_HEAD_DIM), pos)
    v_s = v_s.reshape(B, L, SWA_KV_HEADS, SWA_HEAD_DIM)
    sinks = p['swa_sinks'].reshape(SWA_KV_HEADS, SWA_GROUP)
    if k_buf is None:
        o_s = swa_prompt(q_s, k_s, v_s, sinks)
        n_keep = min(WINDOW, L)
        new_k, new_v = k_s[:, -n_keep:], v_s[:, -n_keep:]
    else:
        o_s, new_k, new_v = swa_sample(q_s, k_s, v_s, k_buf, v_buf, sinks)
    o_s = o_s.reshape(B, L, SWA_Q_HEADS * SWA_HEAD_DIM)
    mixed = jax.nn.sigmoid(gate_a) * o_g + jax.nn.sigmoid(gate_b) * o_s
    x = x + mixed @ p['w_out']
    x = x + 0.5 * swiglu(rmsnorm(x, p['norm_ffn2']), p['w_ffn2_in'], p['w_ffn2_out'])
    return x, new_conv, new_S, new_k, new_v


def setup_inputs(seed: int = 0) -> dict:
    key = jax.random.key(seed)
    ks = jax.random.split(key, 24)
    f32 = jnp.float32

    def nrm(k, shape, scale):
        return jax.random.normal(k, shape, f32) * scale

    def gain(k, shape):
        return 1.0 + 0.05 * jax.random.normal(k, shape, f32)

    n_rows = min(WINDOW, PAST_LEN)
    dt = jnp.exp(jax.random.uniform(ks[20], (DEPTH, GDN_HEADS), f32, math.log(1e-3), math.log(1e-1)))
    dt_bias = dt + jnp.log(-jnp.expm1(-dt))
    a_log = jnp.log(jax.random.uniform(ks[21], (DEPTH, GDN_HEADS), f32, 1.0, 16.0))
    return {
        'x_prompt': nrm(ks[0], (BATCH, SEQ, D_MODEL), 1.0),
        'x_sample': nrm(ks[1], (DEC_BATCH, DEC_SEQ, D_MODEL), 1.0),
        'state_conv': nrm(ks[2], (DEPTH, DEC_BATCH, CONV_W - 1, CONV_DIM), 1.0),
        'state_gdn': nrm(ks[3], (DEPTH, DEC_BATCH, GDN_HEADS, GDN_DK, GDN_DV), 0.1),
        'cache_swa_k': nrm(ks[4], (DEPTH, DEC_BATCH, n_rows, SWA_KV_HEADS, SWA_HEAD_DIM), 1.0),
        'cache_swa_v': nrm(ks[5], (DEPTH, DEC_BATCH, n_rows, SWA_KV_HEADS, SWA_HEAD_DIM), 1.0),
        'norm_ffn1': gain(ks[6], (DEPTH, D_MODEL)),
        'w_ffn1_in': nrm(ks[7], (DEPTH, D_MODEL, 2 * D_FF), D_MODEL ** -0.5),
        'w_ffn1_out': nrm(ks[8], (DEPTH, D_FF, D_MODEL), D_FF ** -0.5),
        'norm_mix': gain(ks[9], (DEPTH, D_MODEL)),
        'w_in': nrm(ks[10], (DEPTH, D_MODEL, D_IN), D_MODEL ** -0.5),
        'conv_w': nrm(ks[11], (DEPTH, CONV_W, CONV_DIM), CONV_W ** -0.5),
        'gdn_a_log': a_log,
        'gdn_dt_bias': dt_bias,
        'gdn_norm': gain(ks[12], (DEPTH, GDN_DV)),
        'swa_sinks': nrm(ks[13], (DEPTH, SWA_Q_HEADS), 0.5),
        'w_out': nrm(ks[14], (DEPTH, D_MODEL, D_MODEL), D_MODEL ** -0.5),
        'norm_ffn2': gain(ks[15], (DEPTH, D_MODEL)),
        'w_ffn2_in': nrm(ks[16], (DEPTH, D_MODEL, 2 * D_FF), D_MODEL ** -0.5),
        'w_ffn2_out': nrm(ks[17], (DEPTH, D_FF, D_MODEL), D_FF ** -0.5),
        'norm_final': gain(ks[18], (D_MODEL,)),
    }


def reference(x_prompt, x_sample, state_conv, state_gdn, cache_swa_k, cache_swa_v,
              norm_ffn1, w_ffn1_in, w_ffn1_out, norm_mix, w_in, conv_w, gdn_a_log, gdn_dt_bias,
              gdn_norm, swa_sinks, w_out, norm_ffn2, w_ffn2_in, w_ffn2_out, norm_final):
    xp, xs = x_prompt, x_sample
    B, L, _ = xp.shape
    pos_p = jnp.arange(L, dtype=jnp.int32)
    pos_s = PAST_LEN + jnp.arange(xs.shape[1], dtype=jnp.int32)
    conv_p, conv_s, gdn_p, gdn_s, k_p, k_s, v_p, v_s = [], [], [], [], [], [], [], []
    for l in range(DEPTH):
        p = {'norm_ffn1': norm_ffn1[l], 'w_ffn1_in': w_ffn1_in[l], 'w_ffn1_out': w_ffn1_out[l],
             'norm_mix': norm_mix[l], 'w_in': w_in[l], 'conv_w': conv_w[l],
             'gdn_a_log': gdn_a_log[l], 'gdn_dt_bias': gdn_dt_bias[l], 'gdn_norm': gdn_norm[l],
             'swa_sinks': swa_sinks[l], 'w_out': w_out[l], 'norm_ffn2': norm_ffn2[l],
             'w_ffn2_in': w_ffn2_in[l], 'w_ffn2_out': w_ffn2_out[l]}
        zero_conv = jnp.zeros((B, CONV_W - 1, CONV_DIM), xp.dtype)
        zero_S = jnp.zeros((B, GDN_HEADS, GDN_DK, GDN_DV), jnp.float32)
        xp, c1, s1, k1, v1 = decoder_layer(xp, pos_p, zero_conv, zero_S, None, None, p)
        xs, c2, s2, k2, v2 = decoder_layer(xs, pos_s, state_conv[l], state_gdn[l],
                                           cache_swa_k[l], cache_swa_v[l], p)
        conv_p.append(c1); conv_s.append(c2)
        gdn_p.append(s1); gdn_s.append(s2)
        k_p.append(k1); k_s.append(k2)
        v_p.append(v1); v_s.append(v2)
    y_prompt = rmsnorm(xp, norm_final)
    y_sample = rmsnorm(xs, norm_final)
    return (y_prompt, y_sample,
            jnp.stack(conv_p), jnp.stack(conv_s),
            jnp.stack(gdn_p), jnp.stack(gdn_s),
            jnp.stack(k_p), jnp.stack(k_s),
            jnp.stack(v_p), jnp.stack(v_s))
```

```python
import functools
import math

import jax
import jax.numpy as jnp
from jax import lax
from jax.experimental import pallas as pl
from jax.experimental.pallas import tpu as pltpu

D_MODEL = 1024
PAST_LEN = 16384
GDN_HEADS = 8
GDN_DK = 128
GDN_DV = 128
GDN_QK_DIM = GDN_HEADS * GDN_DK
GDN_V_DIM = GDN_HEADS * GDN_DV
CONV_W = 4
GDN_CHUNK = 64
CONV_DIM = 2 * GDN_QK_DIM + GDN_V_DIM
SWA_Q_HEADS = 16
SWA_KV_HEADS = 4
SWA_GROUP = SWA_Q_HEADS // SWA_KV_HEADS
SWA_HEAD_DIM = 64
SWA_Q_DIM = SWA_Q_HEADS * SWA_HEAD_DIM
SWA_KV_DIM = SWA_KV_HEADS * SWA_HEAD_DIM
WINDOW = 128
ROPE_DIM = SWA_HEAD_DIM // 4
ROPE_HALF = ROPE_DIM // 2
ROPE_THETA = 500000.0
D_FF = 2816
EPS = 1e-6
IN_SPLITS = (CONV_DIM, GDN_V_DIM, GDN_HEADS, GDN_HEADS, SWA_Q_DIM, SWA_KV_DIM, SWA_KV_DIM, D_MODEL, D_MODEL)

V7X_LANES = 128
V7X_SUBLANES = 8
V7X_MXU_DIM = 256
V7X_VMEM_BYTES = 64 * 1024 * 1024

FFN_CHUNK = V7X_MXU_DIM
SOLVE_BASE = V7X_SUBLANES

F32 = jnp.float32
BF16 = jnp.bfloat16


def _dot(a, b):
    return jnp.dot(a.astype(BF16), b.astype(BF16), preferred_element_type=F32)


def _dot_nt(a, b):
    return lax.dot_general(a.astype(BF16), b.astype(BF16), (((1,), (1,)), ((), ())),
                           preferred_element_type=F32)


def _dot_tn(a, b):
    return lax.dot_general(a.astype(BF16), b.astype(BF16), (((0,), (0,)), ((), ())),
                           preferred_element_type=F32)


def _rms(x, w):
    return x * lax.rsqrt(jnp.mean(x * x, axis=-1, keepdims=True) + EPS) * w


def _silu(x):
    return x * jax.nn.sigmoid(x)


def _vmem_limit(nbytes):
    return int(min(V7X_VMEM_BYTES - (4 << 20), max(nbytes, 16 << 20)))


def _const_spec(shape):
    nd = len(shape)
    return pl.BlockSpec(shape, lambda *_: (0,) * nd, pipeline_mode=pl.Buffered(1))


def _swiglu_half_step(x, norm_w, win_ref, wout_ref):
    xn = _rms(x, norm_w).astype(BF16)
    acc = None
    for c in range(D_FF // FFN_CHUNK):
        lo = c * FFN_CHUNK
        g = jnp.dot(xn, win_ref[:, lo:lo + FFN_CHUNK], preferred_element_type=F32)
        u = jnp.dot(xn, win_ref[:, D_FF + lo:D_FF + lo + FFN_CHUNK], preferred_element_type=F32)
        act = (_silu(g) * u).astype(BF16)
        part = jnp.dot(act, wout_ref[lo:lo + FFN_CHUNK, :], preferred_element_type=F32)
        acc = part if acc is None else acc + part
    return x + 0.5 * acc


def _ffn_kernel(x_ref, nw_ref, win_ref, wout_ref, o_ref):
    o_ref[...] = _swiglu_half_step(x_ref[...], nw_ref[...], win_ref, wout_ref)


def _ffn(x, norm_w, win, wout, tm):
    t = x.shape[0]
    wbytes = (win.size + wout.size) * 2
    return pl.pallas_call(
        _ffn_kernel,
        grid=(t // tm,),
        in_specs=[pl.BlockSpec((tm, D_MODEL), lambda i: (i, 0)),
                  _const_spec((1, D_MODEL)),
                  _const_spec(win.shape),
                  _const_spec(wout.shape)],
        out_specs=pl.BlockSpec((tm, D_MODEL), lambda i: (i, 0)),
        out_shape=jax.ShapeDtypeStruct((t, D_MODEL), F32),
        compiler_params=pltpu.CompilerParams(
            dimension_semantics=("parallel",),
            vmem_limit_bytes=_vmem_limit(wbytes + 12 * tm * D_MODEL * 4 + (8 << 20))),
        name="ffn1",
    )(x, norm_w, win, wout)


_PROJ_WIDTHS = (CONV_DIM, GDN_V_DIM, V7X_LANES, SWA_Q_DIM, SWA_KV_DIM, SWA_KV_DIM, D_MODEL, D_MODEL)


def _inproj_kernel(x_ref, nw_ref, *refs):
    n = len(_PROJ_WIDTHS)
    w_refs, o_refs = refs[:n], refs[n:]
    xn = _rms(x_ref[...], nw_ref[...]).astype(BF16)
    for w_ref, o_ref in zip(w_refs, o_refs):
        o_ref[...] = jnp.dot(xn, w_ref[...], preferred_element_type=F32)


def _inproj(x, norm_w, weights, tm):
    t = x.shape[0]
    wbytes = sum(w.size for w in weights) * 2
    obytes = sum(_PROJ_WIDTHS) * tm * 4
    return pl.pallas_call(
        _inproj_kernel,
        grid=(t // tm,),
        in_specs=[pl.BlockSpec((tm, D_MODEL), lambda i: (i, 0)), _const_spec((1, D_MODEL))]
                 + [_const_spec(w.shape) for w in weights],
        out_specs=[pl.BlockSpec((tm, n), lambda i: (i, 0)) for n in _PROJ_WIDTHS],
        out_shape=[jax.ShapeDtypeStruct((t, n), F32) for n in _PROJ_WIDTHS],
        compiler_params=pltpu.CompilerParams(
            dimension_semantics=("parallel",),
            vmem_limit_bytes=_vmem_limit(wbytes + 3 * obytes + 4 * tm * D_MODEL * 4 + (4 << 20))),
        name="inproj",
    )(x, norm_w, *weights)


def _rope(x, cos_f, sin_a, sin_b):
    width = x.shape[1]
    reps = width // V7X_LANES
    tile = lambda t: jnp.concatenate([t] * reps, axis=1) if reps > 1 else t
    return (x * tile(cos_f) + pltpu.roll(x, width - ROPE_HALF, 1) * tile(sin_a)
            + pltpu.roll(x, ROPE_HALF, 1) * tile(sin_b))


def _post_kernel(qkv_ref, halo_ref, qs_ref, ks_ref, ab_ref, cw_ref, alog_ref, dtb_ref,
                 cos_ref, sina_ref, sinb_ref,
                 qg_ref, kg_ref, vg_ref, qr_ref, kr_ref, gcb_ref, *, group, chunk, halo_is_prev_rows):
    raw = qkv_ref[0]
    rows = raw.shape[0]
    cw = cw_ref[...]
    acc = raw * cw[CONV_W - 1:CONV_W, :]
    if halo_is_prev_rows:
        halo = jnp.where(pl.program_id(1) > 0, halo_ref[0], 0.0)
        ext = jnp.concatenate([halo, raw], axis=0)
        for s in range(1, CONV_W):
            shifted = pltpu.roll(ext, s, 0)[V7X_SUBLANES:]
            acc = acc + shifted * cw[CONV_W - 1 - s:CONV_W - s, :]
    else:
        halo = halo_ref[0]
        pos = lax.broadcasted_iota(jnp.int32, (rows, 1), 0) % group
        for s in range(1, CONV_W):
            shifted = jnp.where(pos >= s, pltpu.roll(raw, s, 0),
                                pltpu.roll(halo, (rows + s - group) % rows, 0))
            acc = acc + shifted * cw[CONV_W - 1 - s:CONV_W - s, :]
    conv = _silu(acc)
    for h in range(GDN_HEADS):
        lo = h * GDN_DK
        qh = conv[:, lo:lo + GDN_DK]
        qg_ref[0, :, lo:lo + GDN_DK] = qh * lax.rsqrt(jnp.sum(qh * qh, axis=-1, keepdims=True) + EPS)
        kh = conv[:, GDN_QK_DIM + lo:GDN_QK_DIM + lo + GDN_DK]
        kg_ref[0, :, lo:lo + GDN_DK] = kh * lax.rsqrt(jnp.sum(kh * kh, axis=-1, keepdims=True) + EPS)
    vg_ref[0] = conv[:, 2 * GDN_QK_DIM:]

    ab = ab_ref[0]
    sp_in = ab + dtb_ref[...]
    softplus = jnp.maximum(sp_in, 0.0) + jnp.log1p(jnp.exp(-jnp.abs(sp_in)))
    gc = -jnp.exp(alog_ref[...]) * softplus
    cpos = lax.broadcasted_iota(jnp.int32, (rows, 1), 0) % chunk
    s = 1
    while s < chunk:
        gc = gc + jnp.where(cpos >= s, pltpu.roll(gc, s, 0), 0.0)
        s *= 2
    lane = lax.broadcasted_iota(jnp.int32, (1, V7X_LANES), 1)
    gcb_ref[0] = jnp.where(lane < GDN_HEADS, gc, jax.nn.sigmoid(ab))

    cos_f, sin_a, sin_b = cos_ref[...], sina_ref[...], sinb_ref[...]
    qr_ref[0] = _rope(qs_ref[0], cos_f, sin_a, sin_b)
    kr_ref[0] = _rope(ks_ref[0], cos_f, sin_a, sin_b)


def _post(qkv, halo, qs, ks, ab, cw, alog, dtb, tables, *, tl, group, chunk, halo_is_prev_rows):
    nb, rows, _ = qkv.shape
    nt = rows // tl
    row_spec = lambda n: pl.BlockSpec((1, tl, n), lambda b, i: (b, i, 0))
    if halo_is_prev_rows:
        per = tl // V7X_SUBLANES
        halo_spec = pl.BlockSpec((1, V7X_SUBLANES, CONV_DIM),
                                 lambda b, i: (b, jnp.maximum(i * per - 1, 0), 0))
        tab_spec = pl.BlockSpec((tl, V7X_LANES), lambda b, i: (i, 0))
    else:
        halo_spec = row_spec(CONV_DIM)
        tab_spec = pl.BlockSpec((tl, V7X_LANES), lambda b, i: (0, 0))
    kern = functools.partial(_post_kernel, group=group, chunk=chunk, halo_is_prev_rows=halo_is_prev_rows)
    widths = (GDN_QK_DIM, GDN_QK_DIM, GDN_V_DIM, SWA_Q_DIM, SWA_KV_DIM, V7X_LANES)
    return pl.pallas_call(
        kern,
        grid=(nb, nt),
        in_specs=[row_spec(CONV_DIM), halo_spec, row_spec(SWA_Q_DIM), row_spec(SWA_KV_DIM),
                  row_spec(V7X_LANES),
                  pl.BlockSpec((CONV_W, CONV_DIM), lambda b, i: (0, 0)),
                  pl.BlockSpec((1, V7X_LANES), lambda b, i: (0, 0)),
                  pl.BlockSpec((1, V7X_LANES), lambda b, i: (0, 0)),
                  tab_spec, tab_spec, tab_spec],
        out_specs=[row_spec(n) for n in widths],
        out_shape=[jax.ShapeDtypeStruct((nb, rows, n), F32) for n in widths],
        compiler_params=pltpu.CompilerParams(
            dimension_semantics=("parallel", "parallel"),
            vmem_limit_bytes=_vmem_limit(48 << 20)),
        name="post",
    )(qkv, halo, qs, ks, ab, cw, alog, dtb, *tables)


def _unit_lower_inverse(a, c):
    ii = lax.broadcasted_iota(jnp.int32, (c, c), 0)
    jj = lax.broadcasted_iota(jnp.int32, (c, c), 1)
    eye = (ii == jj).astype(F32)
    b = min(SOLVE_BASE, c)
    same = (ii // b) == (jj // b)
    ab = jnp.where(same, a, 0.0)
    d = eye - ab
    p = ab
    span = 2
    while span < b:
        p = _dot(p, p)
        d = d + _dot(d, p)
        span *= 2
    while b < c:
        same2 = (ii // (2 * b)) == (jj // (2 * b))
        off = jnp.where(same2 & jnp.logical_not(same), a, 0.0)
        d = d - _dot(d, _dot(off, d))
        same = same2
        b *= 2
    return d


def _gdn_pre_kernel(q_ref, k_ref, v_ref, gcb_ref, grow_ref,
                    u_ref, w_ref, qg_ref, kg_ref, qk_ref, gl_ref, *, chunk):
    c = chunk
    gcb = gcb_ref[0]
    grow = grow_ref[0, 0]
    ii = lax.broadcasted_iota(jnp.int32, (c, c), 0)
    jj = lax.broadcasted_iota(jnp.int32, (c, c), 1)
    for h in range(GDN_HEADS):
        lo = h * GDN_DK
        q = q_ref[0, :, lo:lo + GDN_DK] * (GDN_DK ** -0.5)
        k = k_ref[0, :, lo:lo + GDN_DK]
        v = v_ref[0, :, lo:lo + GDN_DV]
        gc = gcb[:, h:h + 1]
        beta = gcb[:, GDN_HEADS + h:GDN_HEADS + h + 1]
        gr = grow[h:h + 1, :]
        decay = jnp.exp(jnp.where(ii >= jj, gc - gr, -jnp.inf))
        kb = k * beta
        vb = v * beta
        eg = jnp.exp(gc)
        a = jnp.where(ii > jj, _dot_nt(kb, k) * decay, 0.0)
        t = _unit_lower_inverse(a, c)
        u_ref[0, :, lo:lo + GDN_DV] = _dot(t, vb)
        w_ref[0, :, lo:lo + GDN_DK] = _dot(t, kb * eg)
        qk_ref[0, :, h * c:(h + 1) * c] = _dot_nt(q, k) * decay
        qg_ref[0, :, lo:lo + GDN_DK] = q * eg
        g_end = gc[c - 1:c, :]
        kg_ref[0, :, lo:lo + GDN_DK] = k * jnp.exp(g_end - gc)
        gl_ref[0, 0, h:h + 1, :] = jnp.broadcast_to(jnp.exp(g_end), (1, V7X_LANES))


def _gdn_pre(q, k, v, gcb, grow, chunk):
    b, l, _ = q.shape
    n = l // chunk
    tok = lambda w: pl.BlockSpec((1, chunk, w), lambda i, j: (i, j, 0))
    kern = functools.partial(_gdn_pre_kernel, chunk=chunk)
    return pl.pallas_call(
        kern,
        grid=(b, n),
        in_specs=[tok(GDN_QK_DIM), tok(GDN_QK_DIM), tok(GDN_V_DIM), tok(V7X_LANES),
                  pl.BlockSpec((1, 1, GDN_HEADS, chunk), lambda i, j: (i, j, 0, 0))],
        out_specs=[tok(GDN_V_DIM), tok(GDN_QK_DIM), tok(GDN_QK_DIM), tok(GDN_QK_DIM),
                   tok(GDN_HEADS * chunk),
                   pl.BlockSpec((1, 1, GDN_HEADS, V7X_LANES), lambda i, j: (i, j, 0, 0))],
        out_shape=[jax.ShapeDtypeStruct((b, l, GDN_V_DIM), F32),
                   jax.ShapeDtypeStruct((b, l, GDN_QK_DIM), F32),
                   jax.ShapeDtypeStruct((b, l, GDN_QK_DIM), F32),
                   jax.ShapeDtypeStruct((b, l, GDN_QK_DIM), F32),
                   jax.ShapeDtypeStruct((b, l, GDN_HEADS * chunk), F32),
                   jax.ShapeDtypeStruct((b, n, GDN_HEADS, V7X_LANES), F32)],
        compiler_params=pltpu.CompilerParams(dimension_semantics=("parallel", "parallel")),
        name="gdn_pre",
    )(q, k, v, gcb, grow)


def _gdn_scan_kernel(*refs, chunk, has_s0):
    if has_s0:
        u_ref, w_ref, qg_ref, kg_ref, qk_ref, gl_ref, s0_ref, o_ref, sout_ref, s_ref = refs
    else:
        u_ref, w_ref, qg_ref, kg_ref, qk_ref, gl_ref, o_ref, sout_ref, s_ref = refs
    c = chunk
    j = pl.program_id(1)

    @pl.when(j == 0)
    def _():
        if has_s0:
            s_ref[...] = s0_ref[0]
        else:
            s_ref[...] = jnp.zeros_like(s_ref)

    for h in range(GDN_HEADS):
        lo = h * GDN_DK
        s_h = s_ref[h]
        s_b = s_h.astype(BF16)
        v_new = u_ref[0, :, lo:lo + GDN_DV] - _dot(w_ref[0, :, lo:lo + GDN_DK], s_b)
        o = _dot(qg_ref[0, :, lo:lo + GDN_DK], s_b) + _dot(qk_ref[0, :, h * c:(h + 1) * c], v_new)
        o_ref[0, :, lo:lo + GDN_DV] = o
        s_ref[h] = s_h * gl_ref[0, 0, h:h + 1, :] + _dot_tn(kg_ref[0, :, lo:lo + GDN_DK], v_new)

    @pl.when(j == pl.num_programs(1) - 1)
    def _():
        sout_ref[0] = s_ref[...]


def _gdn_scan(u, w, qg, kg, qk, gl, s0, chunk):
    b, l, _ = u.shape
    n = l // chunk
    tok = lambda wd: pl.BlockSpec((1, chunk, wd), lambda i, j: (i, j, 0))
    state_spec = pl.BlockSpec((1, GDN_HEADS, GDN_DK, GDN_DV), lambda i, j: (i, 0, 0, 0))
    in_specs = [tok(GDN_V_DIM), tok(GDN_QK_DIM), tok(GDN_QK_DIM), tok(GDN_QK_DIM),
                tok(GDN_HEADS * chunk),
                pl.BlockSpec((1, 1, GDN_HEADS, V7X_LANES), lambda i, j: (i, j, 0, 0))]
    args = [u, w, qg, kg, qk, gl]
    if s0 is not None:
        in_specs.append(state_spec)
        args.append(s0)
    kern = functools.partial(_gdn_scan_kernel, chunk=chunk, has_s0=s0 is not None)
    return pl.pallas_call(
        kern,
        grid=(b, n),
        in_specs=in_specs,
        out_specs=[tok(GDN_V_DIM), state_spec],
        out_shape=[jax.ShapeDtypeStruct((b, l, GDN_V_DIM), F32),
                   jax.ShapeDtypeStruct((b, GDN_HEADS, GDN_DK, GDN_DV), F32)],
        scratch_shapes=[pltpu.VMEM((GDN_HEADS, GDN_DK, GDN_DV), F32)],
        compiler_params=pltpu.CompilerParams(dimension_semantics=("parallel", "arbitrary")),
        name="gdn_scan",
    )(*args)


def _sink_softmax_pv(s_list, v_list, sink):
    m = sink
    for s in s_list:
        m = jnp.maximum(m, jnp.max(s, axis=-1, keepdims=True))
    denom = jnp.exp(sink - m)
    o = None
    for s, v in zip(s_list, v_list):
        p = jnp.exp(s - m)
        denom = denom + jnp.sum(p, axis=-1, keepdims=True)
        pv = _dot(p, v)
        o = pv if o is None else o + pv
    return o / denom


def _swa_prompt_kernel(q_ref, kc_ref, kp_ref, vc_ref, vp_ref, sink_ref, o_ref):
    w = WINDOW
    rows = SWA_GROUP * w
    has_prev = pl.program_id(1) > 0
    qi = lax.broadcasted_iota(jnp.int32, (rows, w), 0) % w
    kj = lax.broadcasted_iota(jnp.int32, (rows, w), 1)
    mask_cur = kj <= qi
    mask_prev = (kj > qi) & has_prev
    scale = SWA_HEAD_DIM ** -0.5
    for h in range(SWA_KV_HEADS):
        klo = h * SWA_HEAD_DIM
        q_h = jnp.concatenate(
            [q_ref[0, :, (h * SWA_GROUP + g) * SWA_HEAD_DIM:(h * SWA_GROUP + g + 1) * SWA_HEAD_DIM]
             for g in range(SWA_GROUP)], axis=0)
        sink = sink_ref[h]
        k_c = kc_ref[0, :, klo:klo + SWA_HEAD_DIM]
        k_p = kp_ref[0, :, klo:klo + SWA_HEAD_DIM]
        s_c = jnp.where(mask_cur, _dot_nt(q_h, k_c) * scale, -jnp.inf)
        s_p = jnp.where(mask_prev, _dot_nt(q_h, k_p) * scale, -jnp.inf)
        o = _sink_softmax_pv([s_p, s_c],
                             [vp_ref[0, :, klo:klo + SWA_HEAD_DIM], vc_ref[0, :, klo:klo + SWA_HEAD_DIM]],
                             sink)
        for g in range(SWA_GROUP):
            lo = (h * SWA_GROUP + g) * SWA_HEAD_DIM
            o_ref[0, :, lo:lo + SWA_HEAD_DIM] = o[g * w:(g + 1) * w]


def _swa_prompt(q, k, v, sink_rows):
    b, l, _ = q.shape
    nb = l // WINDOW
    cur = lambda wd: pl.BlockSpec((1, WINDOW, wd), lambda i, j: (i, j, 0))
    prev = lambda wd: pl.BlockSpec((1, WINDOW, wd), lambda i, j: (i, jnp.maximum(j - 1, 0), 0))
    return pl.pallas_call(
        _swa_prompt_kernel,
        grid=(b, nb),
        in_specs=[cur(SWA_Q_DIM), cur(SWA_KV_DIM), prev(SWA_KV_DIM), cur(SWA_KV_DIM), prev(SWA_KV_DIM),
                  pl.BlockSpec(sink_rows.shape, lambda i, j: (0, 0, 0))],
        out_specs=cur(SWA_Q_DIM),
        out_shape=jax.ShapeDtypeStruct((b, l, SWA_Q_DIM), F32),
        compiler_params=pltpu.CompilerParams(dimension_semantics=("parallel", "parallel")),
        name="swa_prompt",
    )(q, k, k, v, v, sink_rows)


def _swa_sample_kernel(q_ref, kn_ref, vn_ref, kb_ref, vb_ref, sink_ref, o_ref, ko_ref, vo_ref, *, steps):
    t = steps
    nrows = kb_ref.shape[1]
    rows = SWA_GROUP * t
    bt = q_ref.shape[0]
    qt_b = lax.broadcasted_iota(jnp.int32, (rows, nrows), 0) % t
    ki_b = lax.broadcasted_iota(jnp.int32, (rows, nrows), 1)
    mask_buf = ((nrows - ki_b + qt_b) < WINDOW)[None]
    qt_n = lax.broadcasted_iota(jnp.int32, (rows, t), 0) % t
    ki_n = lax.broadcasted_iota(jnp.int32, (rows, t), 1)
    mask_new = (ki_n <= qt_n)[None]
    scale = SWA_HEAD_DIM ** -0.5
    bdot = lambda a, b_, dims: lax.dot_general(a.astype(BF16), b_.astype(BF16), dims,
                                               preferred_element_type=F32)
    nt_dims = (((2,), (2,)), ((0,), (0,)))
    nn_dims = (((2,), (1,)), ((0,), (0,)))
    for h in range(SWA_KV_HEADS):
        klo = h * SWA_HEAD_DIM
        q_h = jnp.concatenate(
            [q_ref[:, :, (h * SWA_GROUP + g) * SWA_HEAD_DIM:(h * SWA_GROUP + g + 1) * SWA_HEAD_DIM]
             for g in range(SWA_GROUP)], axis=1)
        sink = sink_ref[h][None]
        k_b = kb_ref[:, :, klo:klo + SWA_HEAD_DIM]
        k_n = kn_ref[:, :, klo:klo + SWA_HEAD_DIM]
        v_b = vb_ref[:, :, klo:klo + SWA_HEAD_DIM]
        v_n = vn_ref[:, :, klo:klo + SWA_HEAD_DIM]
        s_b = jnp.where(mask_buf, bdot(q_h, k_b, nt_dims) * scale, -jnp.inf)
        s_n = jnp.where(mask_new, bdot(q_h, k_n, nt_dims) * scale, -jnp.inf)
        m = jnp.maximum(jnp.maximum(jnp.max(s_b, axis=-1, keepdims=True),
                                    jnp.max(s_n, axis=-1, keepdims=True)), sink)
        p_b = jnp.exp(s_b - m)
        p_n = jnp.exp(s_n - m)
        denom = (jnp.sum(p_b, axis=-1, keepdims=True) + jnp.sum(p_n, axis=-1, keepdims=True)
                 + jnp.exp(sink - m))
        o = (bdot(p_b, v_b, nn_dims) + bdot(p_n, v_n, nn_dims)) / denom
        for g in range(SWA_GROUP):
            lo = (h * SWA_GROUP + g) * SWA_HEAD_DIM
            o_ref[:, :, lo:lo + SWA_HEAD_DIM] = o[:, g * t:(g + 1) * t]
    ko_ref[:, :nrows - t] = kb_ref[:, t:]
    ko_ref[:, nrows - t:] = kn_ref[...]
    vo_ref[:, :nrows - t] = vb_ref[:, t:]
    vo_ref[:, nrows - t:] = vn_ref[...]


def _swa_sample(q, k, v, k_buf, v_buf, sink_rows, bt):
    b, t, _ = q.shape
    nrows = k_buf.shape[1]
    new = lambda wd: pl.BlockSpec((bt, t, wd), lambda i: (i, 0, 0))
    buf = pl.BlockSpec((bt, nrows, SWA_KV_DIM), lambda i: (i, 0, 0))
    kern = functools.partial(_swa_sample_kernel, steps=t)
    return pl.pallas_call(
        kern,
        grid=(b // bt,),
        in_specs=[new(SWA_Q_DIM), new(SWA_KV_DIM), new(SWA_KV_DIM), buf, buf,
                  pl.BlockSpec(sink_rows.shape, lambda i: (0, 0, 0))],
        out_specs=[new(SWA_Q_DIM), buf, buf],
        out_shape=[jax.ShapeDtypeStruct((b, t, SWA_Q_DIM), F32),
                   jax.ShapeDtypeStruct((b, nrows, SWA_KV_DIM), F32),
                   jax.ShapeDtypeStruct((b, nrows, SWA_KV_DIM), F32)],
        compiler_params=pltpu.CompilerParams(dimension_semantics=("parallel",)),
        name="swa_sample",
    )(q, k, v, k_buf, v_buf, sink_rows)


def _out_kernel(x_ref, og_ref, z_ref, os_ref, ga_ref, gb_ref, gn_ref, wo_ref,
                nw2_ref, win_ref, wout_ref, nf_ref, y_ref):
    gn = gn_ref[...]
    parts = []
    for h in range(GDN_HEADS):
        lo = h * GDN_DV
        oh = og_ref[:, lo:lo + GDN_DV]
        zh = z_ref[:, lo:lo + GDN_DV]
        parts.append(_rms(oh, gn) * _silu(zh))
    o_g = jnp.concatenate(parts, axis=1)
    mixed = jax.nn.sigmoid(ga_ref[...]) * o_g + jax.nn.sigmoid(gb_ref[...]) * os_ref[...]
    x2 = x_ref[...] + jnp.dot(mixed.astype(BF16), wo_ref[...], preferred_element_type=F32)
    x3 = _swiglu_half_step(x2, nw2_ref[...], win_ref, wout_ref)
    y_ref[...] = _rms(x3, nf_ref[...])


def _out(x1, og, z, os_, ga, gb, gnorm, wo, nw2, win, wout, nf, tm):
    t = x1.shape[0]
    row = pl.BlockSpec((tm, D_MODEL), lambda i: (i, 0))
    wbytes = (wo.size + win.size + wout.size) * 2
    return pl.pallas_call(
        _out_kernel,
        grid=(t // tm,),
        in_specs=[row] * 6 + [_const_spec(gnorm.shape), _const_spec(wo.shape), _const_spec(nw2.shape),
                              _const_spec(win.shape), _const_spec(wout.shape), _const_spec(nf.shape)],
        out_specs=row,
        out_shape=jax.ShapeDtypeStruct((t, D_MODEL), F32),
        compiler_params=pltpu.CompilerParams(
            dimension_semantics=("parallel",),
            vmem_limit_bytes=_vmem_limit(wbytes + 26 * tm * D_MODEL * 4 + (8 << 20))),
        name="out",
    )(x1, og, z, os_, ga, gb, gnorm, wo, nw2, win, wout, nf)


def _rope_tables(pos):
    inv = ROPE_THETA ** (-jnp.arange(0, ROPE_DIM, 2, dtype=F32) / ROPE_DIM)
    ang = pos.astype(F32)[:, None] * inv[None, :]
    cos, sin = jnp.cos(ang), jnp.sin(ang)
    n = pos.shape[0]
    pad = SWA_HEAD_DIM - ROPE_DIM
    one_head = lambda first, second, fill: jnp.concatenate(
        [first, second, jnp.full((n, pad), fill, F32)], axis=1)
    zeros = jnp.zeros_like(sin)
    cos_f = one_head(cos, cos, 1.0)
    sin_a = one_head(-sin, zeros, 0.0)
    sin_b = one_head(zeros, sin, 0.0)
    reps = V7X_LANES // SWA_HEAD_DIM
    return tuple(jnp.tile(t, (1, reps)) for t in (cos_f, sin_a, sin_b))


def _pad_lanes(v, width):
    return jnp.pad(v.astype(F32), (0, width - v.shape[0]))[None, :]


def _token_tile(t, want):
    tm = min(want, t)
    while t % tm:
        tm //= 2
    return tm


def _branch(x, pos, conv_buf, s0, k_buf, v_buf, p, chunk):
    b, l, _ = x.shape
    t = b * l
    x2d = x.reshape(t, D_MODEL)
    x1 = _ffn(x2d, p['norm_ffn1'], p['w_ffn1_in'], p['w_ffn1_out'], _token_tile(t, 512))
    qkv, z, ab, qs, ks, vs, ga, gb = _inproj(x1, p['norm_mix'], p['w_in_parts'], _token_tile(t, 512))

    tables = _rope_tables(pos)
    if conv_buf is None:
        tl = _token_tile(l, 256)
        post_args = dict(tl=tl, group=tl, chunk=chunk, halo_is_prev_rows=True)
        shape3 = lambda a: a.reshape(b, l, a.shape[-1])
        halo = shape3(qkv)
    else:
        tl = _token_tile(t, 256)
        post_args = dict(tl=tl, group=l, chunk=chunk, halo_is_prev_rows=False)
        shape3 = lambda a: a.reshape(1, t, a.shape[-1])
        halo = jnp.pad(conv_buf, ((0, 0), (l - (CONV_W - 1), 0), (0, 0))).reshape(1, t, CONV_DIM)
        tables = tuple(jnp.tile(tb, (tl // l, 1)) for tb in tables)
    qg, kg, vg, qr, kr, gcb = _post(shape3(qkv), halo, shape3(qs), shape3(ks), shape3(ab),
                                    p['conv_w'], p['a_log_pad'], p['dt_bias_pad'], tables, **post_args)
    seq = lambda a: a.reshape(b, l, a.shape[-1])
    qg, kg, vg, qr, kr, gcb = (seq(a) for a in (qg, kg, vg, qr, kr, gcb))
    new_conv = qkv.reshape(b, l, CONV_DIM)[:, l - (CONV_W - 1):]

    n = l // chunk
    grow = gcb[:, :, :GDN_HEADS].reshape(b, n, chunk, GDN_HEADS).transpose(0, 1, 3, 2)
    u, w, qgd, kgd, qk, gl = _gdn_pre(qg, kg, vg, gcb, grow, chunk)
    og, new_s = _gdn_scan(u, w, qgd, kgd, qk, gl, s0, chunk)

    vs3 = vs.reshape(b, l, SWA_KV_DIM)
    if k_buf is None:
        rows = SWA_GROUP * WINDOW
        sink_rows = jnp.repeat(p['sinks'], WINDOW, axis=1).reshape(SWA_KV_HEADS, rows, 1)
        os_ = _swa_prompt(qr, kr, vs3, sink_rows)
        new_k, new_v = kr[:, l - WINDOW:], vs3[:, l - WINDOW:]
    else:
        rows = SWA_GROUP * l
        sink_rows = jnp.repeat(p['sinks'], l, axis=1).reshape(SWA_KV_HEADS, rows, 1)
        nrows = k_buf.shape[1]
        os_, new_k, new_v = _swa_sample(qr, kr, vs3, k_buf.reshape(b, nrows, SWA_KV_DIM),
                                        v_buf.reshape(b, nrows, SWA_KV_DIM), sink_rows, _token_tile(b, 16))
    y = _out(x1, og.reshape(t, GDN_V_DIM), z, os_.reshape(t, SWA_Q_DIM), ga, gb, p['gdn_norm'], p['w_out'],
             p['norm_ffn2'], p['w_ffn2_in'], p['w_ffn2_out'], p['norm_final'], _token_tile(t, 256))
    kv_shape = (b, new_k.shape[1], SWA_KV_HEADS, SWA_HEAD_DIM)
    return (y.reshape(b, l, D_MODEL), new_conv, new_s, new_k.reshape(kv_shape), new_v.reshape(kv_shape))


def _layer_params(l, norm_ffn1, w_ffn1_in, w_ffn1_out, norm_mix, w_in, conv_w, gdn_a_log, gdn_dt_bias,
                  gdn_norm, swa_sinks, w_out, norm_ffn2, w_ffn2_in, w_ffn2_out, norm_final):
    bounds = [0]
    for s in IN_SPLITS:
        bounds.append(bounds[-1] + s)
    col = lambda i: w_in[l][:, bounds[i]:bounds[i + 1]]
    w_ab = jnp.pad(jnp.concatenate([col(2), col(3)], axis=1), ((0, 0), (0, V7X_LANES - 2 * GDN_HEADS)))
    parts = [col(0), col(1), w_ab, col(4), col(5), col(6), col(7), col(8)]
    row = lambda v: v.astype(F32)[None, :]
    return {
        'norm_ffn1': row(norm_ffn1[l]), 'w_ffn1_in': w_ffn1_in[l].astype(BF16),
        'w_ffn1_out': w_ffn1_out[l].astype(BF16),
        'norm_mix': row(norm_mix[l]), 'w_in_parts': [w.astype(BF16) for w in parts],
        'conv_w': conv_w[l].astype(F32),
        'a_log_pad': _pad_lanes(gdn_a_log[l], V7X_LANES), 'dt_bias_pad': _pad_lanes(gdn_dt_bias[l], V7X_LANES),
        'gdn_norm': row(gdn_norm[l]),
        'sinks': swa_sinks[l].astype(F32).reshape(SWA_KV_HEADS, SWA_GROUP),
        'w_out': w_out[l].astype(BF16),
        'norm_ffn2': row(norm_ffn2[l]), 'w_ffn2_in': w_ffn2_in[l].astype(BF16),
        'w_ffn2_out': w_ffn2_out[l].astype(BF16),
        'norm_final': row(norm_final),
    }


def kernel(x_prompt, x_sample, state_conv, state_gdn, cache_swa_k, cache_swa_v, norm_ffn1, w_ffn1_in, w_ffn1_out, norm_mix, w_in, conv_w, gdn_a_log, gdn_dt_bias, gdn_norm, swa_sinks, w_out, norm_ffn2, w_ffn2_in, w_ffn2_out, norm_final):
    depth = w_in.shape[0]
    assert depth == 1, "the final norm is fused into the layer's last kernel"
    l_p = x_prompt.shape[1]
    l_s = x_sample.shape[1]
    pos_p = jnp.arange(l_p, dtype=jnp.int32)
    pos_s = PAST_LEN + jnp.arange(l_s, dtype=jnp.int32)
    p = _layer_params(0, norm_ffn1, w_ffn1_in, w_ffn1_out, norm_mix, w_in, conv_w, gdn_a_log, gdn_dt_bias,
                      gdn_norm, swa_sinks, w_out, norm_ffn2, w_ffn2_in, w_ffn2_out, norm_final)
    yp, c1, s1, k1, v1 = _branch(x_prompt, pos_p, None, None, None, None, p, min(GDN_CHUNK, l_p))
    ys, c2, s2, k2, v2 = _branch(x_sample, pos_s, state_conv[0], state_gdn[0], cache_swa_k[0],
                                 cache_swa_v[0], p, min(GDN_CHUNK, l_s))
    return (yp, ys, c1[None], c2[None], s1[None], s2[None], k1[None], k2[None], v1[None], v2[None])
```

```python
import functools
import math

import jax
import jax.numpy as jnp
from jax import lax
from jax.experimental import pallas as pl
from jax.experimental.pallas import tpu as pltpu

D_MODEL = 1024
PAST_LEN = 16384
GDN_HEADS = 8
GDN_DK = 128
GDN_DV = 128
GDN_QK_DIM = GDN_HEADS * GDN_DK
GDN_V_DIM = GDN_HEADS * GDN_DV
CONV_W = 4
GDN_CHUNK = 64
CONV_DIM = 2 * GDN_QK_DIM + GDN_V_DIM
SWA_Q_HEADS = 16
SWA_KV_HEADS = 4
SWA_GROUP = SWA_Q_HEADS // SWA_KV_HEADS
SWA_HEAD_DIM = 64
SWA_Q_DIM = SWA_Q_HEADS * SWA_HEAD_DIM
SWA_KV_DIM = SWA_KV_HEADS * SWA_HEAD_DIM
WINDOW = 128
ROPE_DIM = SWA_HEAD_DIM // 4
ROPE_HALF = ROPE_DIM // 2
ROPE_THETA = 500000.0
D_FF = 2816
EPS = 1e-6
IN_SPLITS = (CONV_DIM, GDN_V_DIM, GDN_HEADS, GDN_HEADS, SWA_Q_DIM, SWA_KV_DIM, SWA_KV_DIM, D_MODEL, D_MODEL)

V7X_LANES = 128
V7X_SUBLANES = 8
V7X_MXU_DIM = 256
V7X_VMEM_BYTES = 64 * 1024 * 1024

FFN_CHUNK = V7X_MXU_DIM
SOLVE_BASE = V7X_SUBLANES

F32 = jnp.float32
BF16 = jnp.bfloat16


def _dot(a, b):
    return jnp.dot(a.astype(BF16), b.astype(BF16), preferred_element_type=F32)


def _dot_nt(a, b):
    return lax.dot_general(a.astype(BF16), b.astype(BF16), (((1,), (1,)), ((), ())),
                           preferred_element_type=F32)


def _dot_tn(a, b):
    return lax.dot_general(a.astype(BF16), b.astype(BF16), (((0,), (0,)), ((), ())),
                           preferred_element_type=F32)


def _rms(x, w):
    return x * lax.rsqrt(jnp.mean(x * x, axis=-1, keepdims=True) + EPS) * w


def _silu(x):
    return x * jax.nn.sigmoid(x)


def _vmem_limit(nbytes):
    return int(min(V7X_VMEM_BYTES - (4 << 20), max(nbytes, 16 << 20)))


def _const_spec(shape):
    nd = len(shape)
    return pl.BlockSpec(shape, lambda *_: (0,) * nd, pipeline_mode=pl.Buffered(1))


def _swiglu_half_step(x, norm_w, win_ref, wout_ref):
    xn = _rms(x, norm_w).astype(BF16)
    acc = None
    for c in range(D_FF // FFN_CHUNK):
        lo = c * FFN_CHUNK
        g = jnp.dot(xn, win_ref[:, lo:lo + FFN_CHUNK], preferred_element_type=F32)
        u = jnp.dot(xn, win_ref[:, D_FF + lo:D_FF + lo + FFN_CHUNK], preferred_element_type=F32)
        act = (_silu(g) * u).astype(BF16)
        part = jnp.dot(act, wout_ref[lo:lo + FFN_CHUNK, :], preferred_element_type=F32)
        acc = part if acc is None else acc + part
    return x + 0.5 * acc


def _ffn_kernel(x_ref, nw_ref, win_ref, wout_ref, o_ref):
    o_ref[...] = _swiglu_half_step(x_ref[...], nw_ref[...], win_ref, wout_ref)


def _ffn(x, norm_w, win, wout, tm):
    t = x.shape[0]
    wbytes = (win.size + wout.size) * 2
    return pl.pallas_call(
        _ffn_kernel,
        grid=(t // tm,),
        in_specs=[pl.BlockSpec((tm, D_MODEL), lambda i: (i, 0)),
                  _const_spec((1, D_MODEL)),
                  _const_spec(win.shape),
                  _const_spec(wout.shape)],
        out_specs=pl.BlockSpec((tm, D_MODEL), lambda i: (i, 0)),
        out_shape=jax.ShapeDtypeStruct((t, D_MODEL), F32),
        compiler_params=pltpu.CompilerParams(
            dimension_semantics=("parallel",),
            vmem_limit_bytes=_vmem_limit(wbytes + 12 * tm * D_MODEL * 4 + (8 << 20))),
        name="ffn1",
    )(x, norm_w, win, wout)


_PROJ_WIDTHS = (CONV_DIM, GDN_V_DIM, V7X_LANES, SWA_Q_DIM, SWA_KV_DIM, SWA_KV_DIM, D_MODEL, D_MODEL)


def _inproj_kernel(x_ref, nw_ref, *refs):
    n = len(_PROJ_WIDTHS)
    w_refs, o_refs = refs[:n], refs[n:]
    xn = _rms(x_ref[...], nw_ref[...]).astype(BF16)
    for w_ref, o_ref in zip(w_refs, o_refs):
        o_ref[...] = jnp.dot(xn, w_ref[...], preferred_element_type=F32)


def _inproj(x, norm_w, weights, tm):
    t = x.shape[0]
    wbytes = sum(w.size for w in weights) * 2
    obytes = sum(_PROJ_WIDTHS) * tm * 4
    return pl.pallas_call(
        _inproj_kernel,
        grid=(t // tm,),
        in_specs=[pl.BlockSpec((tm, D_MODEL), lambda i: (i, 0)), _const_spec((1, D_MODEL))]
                 + [_const_spec(w.shape) for w in weights],
        out_specs=[pl.BlockSpec((tm, n), lambda i: (i, 0)) for n in _PROJ_WIDTHS],
        out_shape=[jax.ShapeDtypeStruct((t, n), F32) for n in _PROJ_WIDTHS],
        compiler_params=pltpu.CompilerParams(
            dimension_semantics=("parallel",),
            vmem_limit_bytes=_vmem_limit(wbytes + 3 * obytes + 4 * tm * D_MODEL * 4 + (4 << 20))),
        name="inproj",
    )(x, norm_w, *weights)


def _rope(x, cos_f, sin_a, sin_b):
    width = x.shape[1]
    reps = width // V7X_LANES
    tile = lambda t: jnp.concatenate([t] * reps, axis=1) if reps > 1 else t
    return (x * tile(cos_f) + pltpu.roll(x, width - ROPE_HALF, 1) * tile(sin_a)
            + pltpu.roll(x, ROPE_HALF, 1) * tile(sin_b))


def _post_kernel(qkv_ref, halo_ref, qs_ref, ks_ref, ab_ref, cw_ref, alog_ref, dtb_ref,
                 cos_ref, sina_ref, sinb_ref,
                 qg_ref, kg_ref, vg_ref, qr_ref, kr_ref, gcb_ref, *, group, chunk, halo_is_prev_rows):
    raw = qkv_ref[0]
    rows = raw.shape[0]
    cw = cw_ref[...]
    acc = raw * cw[CONV_W - 1:CONV_W, :]
    if halo_is_prev_rows:
        halo = jnp.where(pl.program_id(1) > 0, halo_ref[0], 0.0)
        ext = jnp.concatenate([halo, raw], axis=0)
        for s in range(1, CONV_W):
            shifted = pltpu.roll(ext, s, 0)[V7X_SUBLANES:]
            acc = acc + shifted * cw[CONV_W - 1 - s:CONV_W - s, :]
    else:
        halo = halo_ref[0]
        pos = lax.broadcasted_iota(jnp.int32, (rows, 1), 0) % group
        for s in range(1, CONV_W):
            shifted = jnp.where(pos >= s, pltpu.roll(raw, s, 0),
                                pltpu.roll(halo, (rows + s - group) % rows, 0))
            acc = acc + shifted * cw[CONV_W - 1 - s:CONV_W - s, :]
    conv = _silu(acc)
    for h in range(GDN_HEADS):
        lo = h * GDN_DK
        qh = conv[:, lo:lo + GDN_DK]
        qg_ref[0, :, lo:lo + GDN_DK] = qh * lax.rsqrt(jnp.sum(qh * qh, axis=-1, keepdims=True) + EPS)
        kh = conv[:, GDN_QK_DIM + lo:GDN_QK_DIM + lo + GDN_DK]
        kg_ref[0, :, lo:lo + GDN_DK] = kh * lax.rsqrt(jnp.sum(kh * kh, axis=-1, keepdims=True) + EPS)
    vg_ref[0] = conv[:, 2 * GDN_QK_DIM:]

    ab = ab_ref[0]
    sp_in = ab + dtb_ref[...]
    softplus = jnp.maximum(sp_in, 0.0) + jnp.log1p(jnp.exp(-jnp.abs(sp_in)))
    gc = -jnp.exp(alog_ref[...]) * softplus
    cpos = lax.broadcasted_iota(jnp.int32, (rows, 1), 0) % chunk
    s = 1
    while s < chunk:
        gc = gc + jnp.where(cpos >= s, pltpu.roll(gc, s, 0), 0.0)
        s *= 2
    lane = lax.broadcasted_iota(jnp.int32, (1, V7X_LANES), 1)
    gcb_ref[0] = jnp.where(lane < GDN_HEADS, gc, jax.nn.sigmoid(ab))

    cos_f, sin_a, sin_b = cos_ref[...], sina_ref[...], sinb_ref[...]
    qr_ref[0] = _rope(qs_ref[0], cos_f, sin_a, sin_b)
    kr_ref[0] = _rope(ks_ref[0], cos_f, sin_a, sin_b)


def _post(qkv, halo, qs, ks, ab, cw, alog, dtb, tables, *, tl, group, chunk, halo_is_prev_rows):
    nb, rows, _ = qkv.shape
    nt = rows // tl
    row_spec = lambda n: pl.BlockSpec((1, tl, n), lambda b, i: (b, i, 0))
    if halo_is_prev_rows:
        per = tl // V7X_SUBLANES
        halo_spec = pl.BlockSpec((1, V7X_SUBLANES, CONV_DIM),
                                 lambda b, i: (b, jnp.maximum(i * per - 1, 0), 0))
        tab_spec = pl.BlockSpec((tl, V7X_LANES), lambda b, i: (i, 0))
    else:
        halo_spec = row_spec(CONV_DIM)
        tab_spec = pl.BlockSpec((tl, V7X_LANES), lambda b, i: (0, 0))
    kern = functools.partial(_post_kernel, group=group, chunk=chunk, halo_is_prev_rows=halo_is_prev_rows)
    widths = (GDN_QK_DIM, GDN_QK_DIM, GDN_V_DIM, SWA_Q_DIM, SWA_KV_DIM, V7X_LANES)
    return pl.pallas_call(
        kern,
        grid=(nb, nt),
        in_specs=[row_spec(CONV_DIM), halo_spec, row_spec(SWA_Q_DIM), row_spec(SWA_KV_DIM),
                  row_spec(V7X_LANES),
                  pl.BlockSpec((CONV_W, CONV_DIM), lambda b, i: (0, 0)),
                  pl.BlockSpec((1, V7X_LANES), lambda b, i: (0, 0)),
                  pl.BlockSpec((1, V7X_LANES), lambda b, i: (0, 0)),
                  tab_spec, tab_spec, tab_spec],
        out_specs=[row_spec(n) for n in widths],
        out_shape=[jax.ShapeDtypeStruct((nb, rows, n), F32) for n in widths],
        compiler_params=pltpu.CompilerParams(
            dimension_semantics=("parallel", "parallel"),
            vmem_limit_bytes=_vmem_limit(48 << 20)),
        name="post",
    )(qkv, halo, qs, ks, ab, cw, alog, dtb, *tables)


def _unit_lower_inverse(a, c):
    ii = lax.broadcasted_iota(jnp.int32, (c, c), 0)
    jj = lax.broadcasted_iota(jnp.int32, (c, c), 1)
    eye = (ii == jj).astype(F32)
    b = min(SOLVE_BASE, c)
    same = (ii // b) == (jj // b)
    ab = jnp.where(same, a, 0.0)
    d = eye - ab
    p = ab
    span = 2
    while span < b:
        p = _dot(p, p)
        d = d + _dot(d, p)
        span *= 2
    while b < c:
        same2 = (ii // (2 * b)) == (jj // (2 * b))
        off = jnp.where(same2 & jnp.logical_not(same), a, 0.0)
        d = d - _dot(d, _dot(off, d))
        same = same2
        b *= 2
    return d


def _gdn_pre_kernel(q_ref, k_ref, v_ref, gcb_ref, grow_ref,
                    u_ref, w_ref, qg_ref, kg_ref, qk_ref, gl_ref, *, chunk):
    c = chunk
    gcb = gcb_ref[0]
    grow = grow_ref[0, 0]
    ii = lax.broadcasted_iota(jnp.int32, (c, c), 0)
    jj = lax.broadcasted_iota(jnp.int32, (c, c), 1)
    for h in range(GDN_HEADS):
        lo = h * GDN_DK
        q = q_ref[0, :, lo:lo + GDN_DK] * (GDN_DK ** -0.5)
        k = k_ref[0, :, lo:lo + GDN_DK]
        v = v_ref[0, :, lo:lo + GDN_DV]
        gc = gcb[:, h:h + 1]
        beta = gcb[:, GDN_HEADS + h:GDN_HEADS + h + 1]
        gr = grow[h:h + 1, :]
        decay = jnp.exp(jnp.where(ii >= jj, gc - gr, -jnp.inf))
        kb = k * beta
        vb = v * beta
        eg = jnp.exp(gc)
        a = jnp.where(ii > jj, _dot_nt(kb, k) * decay, 0.0)
        t = _unit_lower_inverse(a, c)
        u_ref[0, :, lo:lo + GDN_DV] = _dot(t, vb)
        w_ref[0, :, lo:lo + GDN_DK] = _dot(t, kb * eg)
        qk_ref[0, :, h * c:(h + 1) * c] = _dot_nt(q, k) * decay
        qg_ref[0, :, lo:lo + GDN_DK] = q * eg
        g_end = gc[c - 1:c, :]
        kg_ref[0, :, lo:lo + GDN_DK] = k * jnp.exp(g_end - gc)
        gl_ref[0, 0, h:h + 1, :] = jnp.broadcast_to(jnp.exp(g_end), (1, V7X_LANES))


def _gdn_pre(q, k, v, gcb, grow, chunk):
    b, l, _ = q.shape
    n = l // chunk
    tok = lambda w: pl.BlockSpec((1, chunk, w), lambda i, j: (i, j, 0))
    kern = functools.partial(_gdn_pre_kernel, chunk=chunk)
    return pl.pallas_call(
        kern,
        grid=(b, n),
        in_specs=[tok(GDN_QK_DIM), tok(GDN_QK_DIM), tok(GDN_V_DIM), tok(V7X_LANES),
                  pl.BlockSpec((1, 1, GDN_HEADS, chunk), lambda i, j: (i, j, 0, 0))],
        out_specs=[tok(GDN_V_DIM), tok(GDN_QK_DIM), tok(GDN_QK_DIM), tok(GDN_QK_DIM),
                   tok(GDN_HEADS * chunk),
                   pl.BlockSpec((1, 1, GDN_HEADS, V7X_LANES), lambda i, j: (i, j, 0, 0))],
        out_shape=[jax.ShapeDtypeStruct((b, l, GDN_V_DIM), F32),
                   jax.ShapeDtypeStruct((b, l, GDN_QK_DIM), F32),
                   jax.ShapeDtypeStruct((b, l, GDN_QK_DIM), F32),
                   jax.ShapeDtypeStruct((b, l, GDN_QK_DIM), F32),
                   jax.ShapeDtypeStruct((b, l, GDN_HEADS * chunk), F32),
                   jax.ShapeDtypeStruct((b, n, GDN_HEADS, V7X_LANES), F32)],
        compiler_params=pltpu.CompilerParams(dimension_semantics=("parallel", "parallel")),
        name="gdn_pre",
    )(q, k, v, gcb, grow)


def _gdn_scan_kernel(*refs, chunk, has_s0):
    if has_s0:
        u_ref, w_ref, qg_ref, kg_ref, qk_ref, gl_ref, s0_ref, o_ref, sout_ref, s_ref = refs
    else:
        u_ref, w_ref, qg_ref, kg_ref, qk_ref, gl_ref, o_ref, sout_ref, s_ref = refs
    c = chunk
    j = pl.program_id(1)

    @pl.when(j == 0)
    def _():
        if has_s0:
            s_ref[...] = s0_ref[0]
        else:
            s_ref[...] = jnp.zeros_like(s_ref)

    for h in range(GDN_HEADS):
        lo = h * GDN_DK
        s_h = s_ref[h]
        s_b = s_h.astype(BF16)
        v_new = u_ref[0, :, lo:lo + GDN_DV] - _dot(w_ref[0, :, lo:lo + GDN_DK], s_b)
        o = _dot(qg_ref[0, :, lo:lo + GDN_DK], s_b) + _dot(qk_ref[0, :, h * c:(h + 1) * c], v_new)
        o_ref[0, :, lo:lo + GDN_DV] = o
        s_ref[h] = s_h * gl_ref[0, 0, h:h + 1, :] + _dot_tn(kg_ref[0, :, lo:lo + GDN_DK], v_new)

    @pl.when(j == pl.num_programs(1) - 1)
    def _():
        sout_ref[0] = s_ref[...]


def _gdn_scan(u, w, qg, kg, qk, gl, s0, chunk):
    b, l, _ = u.shape
    n = l // chunk
    tok = lambda wd: pl.BlockSpec((1, chunk, wd), lambda i, j: (i, j, 0))
    state_spec = pl.BlockSpec((1, GDN_HEADS, GDN_DK, GDN_DV), lambda i, j: (i, 0, 0, 0))
    in_specs = [tok(GDN_V_DIM), tok(GDN_QK_DIM), tok(GDN_QK_DIM), tok(GDN_QK_DIM),
                tok(GDN_HEADS * chunk),
                pl.BlockSpec((1, 1, GDN_HEADS, V7X_LANES), lambda i, j: (i, j, 0, 0))]
    args = [u, w, qg, kg, qk, gl]
    if s0 is not None:
        in_specs.append(state_spec)
        args.append(s0)
    kern = functools.partial(_gdn_scan_kernel, chunk=chunk, has_s0=s0 is not None)
    return pl.pallas_call(
        kern,
        grid=(b, n),
        in_specs=in_specs,
        out_specs=[tok(GDN_V_DIM), state_spec],
        out_shape=[jax.ShapeDtypeStruct((b, l, GDN_V_DIM), F32),
                   jax.ShapeDtypeStruct((b, GDN_HEADS, GDN_DK, GDN_DV), F32)],
        scratch_shapes=[pltpu.VMEM((GDN_HEADS, GDN_DK, GDN_DV), F32)],
        compiler_params=pltpu.CompilerParams(dimension_semantics=("parallel", "arbitrary")),
        name="gdn_scan",
    )(*args)


PK_C = GDN_CHUNK
PK_CAT = GDN_HEADS * PK_C
PK_PAIR = 2 * GDN_DK
PK_NPAIR = GDN_HEADS // 2
PK_GROUP = V7X_MXU_DIM // PK_C


def _pk_masks():
    c = PK_C
    i = lax.broadcasted_iota(jnp.int32, (c, PK_CAT), 0)
    j = lax.broadcasted_iota(jnp.int32, (c, PK_CAT), 1) % c
    r = lax.broadcasted_iota(jnp.int32, (V7X_MXU_DIM, V7X_MXU_DIM), 0) // c
    l = lax.broadcasted_iota(jnp.int32, (V7X_MXU_DIM, V7X_MXU_DIM), 1) // c
    return i, j, (r == l).astype(BF16)


def _pk_prod(x, y, bd_mask):
    outs = []
    for g in range(GDN_HEADS // PK_GROUP):
        sl = slice(g * V7X_MXU_DIM, (g + 1) * V7X_MXU_DIM)
        yb = y[:, sl].astype(BF16)
        bd = jnp.concatenate([yb] * PK_GROUP, axis=0) * bd_mask
        outs.append(jnp.dot(x[:, sl].astype(BF16), bd, preferred_element_type=F32))
    return jnp.concatenate(outs, axis=1)


def _pk_solve(a_list, i, j, bd_mask):
    prod_all = lambda xs, ys: [_pk_prod(x, y, bd_mask) for x, y in zip(xs, ys)]
    eye = (i == j).astype(F32)
    b = SOLVE_BASE
    same = (i // b) == (j // b)
    p = [jnp.where(same, a, 0.0) for a in a_list]
    d = [eye - x for x in p]
    span = 2
    while span < b:
        p = prod_all(p, p)
        d = [x + y for x, y in zip(d, prod_all(d, p))]
        span *= 2
    while b < PK_C:
        same2 = (i // (2 * b)) == (j // (2 * b))
        sel = same2 & jnp.logical_not(same)
        off = [jnp.where(sel, a, 0.0) for a in a_list]
        d = [x - y for x, y in zip(d, prod_all(d, prod_all(off, d)))]
        same = same2
        b *= 2
    return d


def _pk_pre_kernel(q_ref, k_ref, v_ref, gcb_ref, grow_ref,
                   u_ref, w_ref, qg_ref, kg_ref, qk_ref, gl_ref, *, nc):
    c = PK_C
    i, j, bd_mask = _pk_masks()
    lane_lo = lax.broadcasted_iota(jnp.int32, (c, V7X_LANES), 1) < c
    zeros_k = jnp.zeros((c, GDN_DK), BF16)
    a_list, qk_list, vb_list, kbg_list = [], [], [], []
    for ch in range(nc):
        rows = slice(ch * c, (ch + 1) * c)
        gcb = gcb_ref[0, rows, :]
        g_end = gcb[c - 1:c, :]
        eg_s = jnp.exp(gcb)
        ek_s = jnp.exp(g_end - gcb)
        col = lambda arr, n: jnp.broadcast_to(arr[:, n:n + 1], (c, V7X_LANES))
        full = lambda arr, base: jnp.concatenate([col(arr, base + h) for h in range(GDN_HEADS)], axis=1)
        gc_cols = [col(gcb, h) for h in range(GDN_HEADS)]
        beta_f = full(gcb, GDN_HEADS)
        eg_f = full(eg_s, 0)
        ek_f = full(ek_s, 0)
        gc_cat = jnp.concatenate([jnp.where(lane_lo, gc_cols[2 * m], gc_cols[2 * m + 1])
                                  for m in range(PK_NPAIR)], axis=1)
        decay = jnp.exp(jnp.where(i >= j, gc_cat - grow_ref[0, ch], -jnp.inf))
        q = q_ref[0, rows, :] * (GDN_DK ** -0.5)
        k = k_ref[0, rows, :]
        kb = k * beta_f
        vb_list.append((v_ref[0, rows, :] * beta_f).astype(BF16))
        kbg_list.append((kb * eg_f).astype(BF16))
        qg_ref[0, rows, :] = (q * eg_f).astype(BF16)
        kg_ref[0, rows, :] = (k * ek_f).astype(BF16)
        gl_ref[0, ch] = eg_f[c - 1:c, :]
        kk = []
        for m in range(PK_NPAIR):
            sl = slice(m * PK_PAIR, (m + 1) * PK_PAIR)
            lhs = jnp.concatenate([kb[:, sl], q[:, sl]], axis=0).astype(BF16)
            k2 = k[:, sl].astype(BF16)
            rhs_nt = jnp.concatenate(
                [jnp.concatenate([k2[:, :GDN_DK], zeros_k], axis=1),
                 jnp.concatenate([zeros_k, k2[:, GDN_DK:]], axis=1)], axis=0)
            kk.append(lax.dot_general(lhs, rhs_nt, (((1,), (1,)), ((), ())), preferred_element_type=F32))
        a_raw = jnp.concatenate([x[:c] for x in kk], axis=1)
        qk_raw = jnp.concatenate([x[c:] for x in kk], axis=1)
        a_list.append(jnp.where(i > j, a_raw * decay, 0.0))
        qk_ref[0, rows, :] = (qk_raw * decay).astype(BF16)
    t_list = _pk_solve(a_list, i, j, bd_mask)
    for ch in range(nc):
        rows = slice(ch * c, (ch + 1) * c)
        t = t_list[ch]
        for m in range(PK_NPAIR):
            tp = t[:, m * V7X_LANES:(m + 1) * V7X_LANES]
            lhs = jnp.concatenate([jnp.where(lane_lo, tp, 0.0), jnp.where(lane_lo, 0.0, tp)],
                                  axis=0).astype(BF16)
            h0, h1 = 2 * m, 2 * m + 1
            hs = lambda h: slice(h * GDN_DK, (h + 1) * GDN_DK)
            rhs = jnp.concatenate(
                [jnp.concatenate([vb_list[ch][:, hs(h0)], kbg_list[ch][:, hs(h0)]], axis=1),
                 jnp.concatenate([vb_list[ch][:, hs(h1)], kbg_list[ch][:, hs(h1)]], axis=1)], axis=0)
            uw = jnp.dot(lhs, rhs, preferred_element_type=F32)
            u_ref[0, rows, hs(h0)] = uw[:c, :GDN_DV]
            w_ref[0, rows, hs(h0)] = uw[:c, GDN_DV:].astype(BF16)
            u_ref[0, rows, hs(h1)] = uw[c:, :GDN_DV]
            w_ref[0, rows, hs(h1)] = uw[c:, GDN_DV:].astype(BF16)


def _pk_pre(q, k, v, gcb, grow_flat, nc):
    b, l, _ = q.shape
    tl = nc * PK_C
    tok = lambda wd: pl.BlockSpec((1, tl, wd), lambda bi, ti: (bi, ti, 0))
    per_chunk = lambda wd: pl.BlockSpec((1, nc, 1, wd), lambda bi, ti: (bi, ti, 0, 0))
    kern = functools.partial(_pk_pre_kernel, nc=nc)
    n = l // PK_C
    return pl.pallas_call(
        kern,
        grid=(b, l // tl),
        in_specs=[tok(GDN_QK_DIM), tok(GDN_QK_DIM), tok(GDN_V_DIM), tok(V7X_LANES), per_chunk(PK_CAT)],
        out_specs=[tok(GDN_V_DIM), tok(GDN_QK_DIM), tok(GDN_QK_DIM), tok(GDN_QK_DIM), tok(PK_CAT),
                   per_chunk(GDN_V_DIM)],
        out_shape=[jax.ShapeDtypeStruct((b, l, GDN_V_DIM), F32),
                   jax.ShapeDtypeStruct((b, l, GDN_QK_DIM), BF16),
                   jax.ShapeDtypeStruct((b, l, GDN_QK_DIM), BF16),
                   jax.ShapeDtypeStruct((b, l, GDN_QK_DIM), BF16),
                   jax.ShapeDtypeStruct((b, l, PK_CAT), BF16),
                   jax.ShapeDtypeStruct((b, n, 1, GDN_V_DIM), F32)],
        compiler_params=pltpu.CompilerParams(dimension_semantics=("parallel", "parallel"),
                                             vmem_limit_bytes=_vmem_limit(48 << 20)),
        name="gdn_pre_packed",
    )(q, k, v, gcb, grow_flat)


def _pk_scan_kernel(u_ref, w_ref, qg_ref, kg_ref, qk_ref, gl_ref, o_ref, sout_ref, s_ref):
    c = PK_C
    nb = u_ref.shape[0]
    step = pl.program_id(0)

    @pl.when(step == 0)
    def _():
        s_ref[...] = jnp.zeros_like(s_ref)

    zeros_s = jnp.zeros((GDN_DK, GDN_DV), BF16)
    zeros_v = jnp.zeros((c, GDN_DV), BF16)
    chains = [(b, m) for b in range(nb) for m in range(PK_NPAIR)]
    pair = lambda m: slice(m * PK_PAIR, (m + 1) * PK_PAIR)
    s_old, wq = {}, {}
    for b, m in chains:
        s_c = s_ref[b, m]
        s_old[b, m] = s_c
        sb = s_c.astype(BF16)
        bd_s = jnp.concatenate([jnp.concatenate([sb[:, :GDN_DV], zeros_s], axis=1),
                                jnp.concatenate([zeros_s, sb[:, GDN_DV:]], axis=1)], axis=0)
        lhs = jnp.concatenate([w_ref[b, :, pair(m)], qg_ref[b, :, pair(m)]], axis=0)
        wq[b, m] = jnp.dot(lhs, bd_s, preferred_element_type=F32)
    bd_v = {}
    for b, m in chains:
        v_new = (u_ref[b, :, pair(m)] - wq[b, m][:c]).astype(BF16)
        bd_v[b, m] = jnp.concatenate([jnp.concatenate([v_new[:, :GDN_DV], zeros_v], axis=1),
                                      jnp.concatenate([zeros_v, v_new[:, GDN_DV:]], axis=1)], axis=0)
    for b, m in chains:
        qk_p = qk_ref[b, :, m * V7X_LANES:(m + 1) * V7X_LANES]
        o_ref[b, :, pair(m)] = wq[b, m][c:] + jnp.dot(qk_p, bd_v[b, m], preferred_element_type=F32)
        kg_p = kg_ref[b, :, pair(m)]
        kg_st = jnp.concatenate([kg_p[:, :GDN_DK], kg_p[:, GDN_DK:]], axis=0)
        upd = lax.dot_general(kg_st, bd_v[b, m], (((0,), (0,)), ((), ())), preferred_element_type=F32)
        s_ref[b, m] = s_old[b, m] * gl_ref[b, 0, :, pair(m)] + upd

    @pl.when(step == pl.num_programs(0) - 1)
    def _():
        for b, m in chains:
            s_c = s_ref[b, m]
            sout_ref[b, 2 * m] = s_c[:, :GDN_DV]
            sout_ref[b, 2 * m + 1] = s_c[:, GDN_DV:]


def _pk_scan(u, w, qg, kg, qk, gl):
    b, l, _ = u.shape
    n = l // PK_C
    tok = lambda wd: pl.BlockSpec((b, PK_C, wd), lambda s: (0, s, 0))
    return pl.pallas_call(
        _pk_scan_kernel,
        grid=(n,),
        in_specs=[tok(GDN_V_DIM), tok(GDN_QK_DIM), tok(GDN_QK_DIM), tok(GDN_QK_DIM), tok(PK_CAT),
                  pl.BlockSpec((b, 1, 1, GDN_V_DIM), lambda s: (0, s, 0, 0))],
        out_specs=[tok(GDN_V_DIM),
                   pl.BlockSpec((b, GDN_HEADS, GDN_DK, GDN_DV), lambda s: (0, 0, 0, 0))],
        out_shape=[jax.ShapeDtypeStruct((b, l, GDN_V_DIM), F32),
                   jax.ShapeDtypeStruct((b, GDN_HEADS, GDN_DK, GDN_DV), F32)],
        scratch_shapes=[pltpu.VMEM((b, PK_NPAIR, GDN_DK, 2 * GDN_DV), F32)],
        compiler_params=pltpu.CompilerParams(dimension_semantics=("arbitrary",)),
        name="gdn_scan_packed",
    )(u, w, qg, kg, qk, gl)


def _sink_softmax_pv(s_list, v_list, sink):
    m = sink
    for s in s_list:
        m = jnp.maximum(m, jnp.max(s, axis=-1, keepdims=True))
    denom = jnp.exp(sink - m)
    o = None
    for s, v in zip(s_list, v_list):
        p = jnp.exp(s - m)
        denom = denom + jnp.sum(p, axis=-1, keepdims=True)
        pv = _dot(p, v)
        o = pv if o is None else o + pv
    return o / denom


def _swa_prompt_kernel(q_ref, kc_ref, kp_ref, vc_ref, vp_ref, sink_ref, o_ref):
    w = WINDOW
    rows = SWA_GROUP * w
    has_prev = pl.program_id(1) > 0
    qi = lax.broadcasted_iota(jnp.int32, (rows, w), 0) % w
    kj = lax.broadcasted_iota(jnp.int32, (rows, w), 1)
    mask_cur = kj <= qi
    mask_prev = (kj > qi) & has_prev
    scale = SWA_HEAD_DIM ** -0.5
    for h in range(SWA_KV_HEADS):
        klo = h * SWA_HEAD_DIM
        q_h = jnp.concatenate(
            [q_ref[0, :, (h * SWA_GROUP + g) * SWA_HEAD_DIM:(h * SWA_GROUP + g + 1) * SWA_HEAD_DIM]
             for g in range(SWA_GROUP)], axis=0)
        sink = sink_ref[h]
        k_c = kc_ref[0, :, klo:klo + SWA_HEAD_DIM]
        k_p = kp_ref[0, :, klo:klo + SWA_HEAD_DIM]
        s_c = jnp.where(mask_cur, _dot_nt(q_h, k_c) * scale, -jnp.inf)
        s_p = jnp.where(mask_prev, _dot_nt(q_h, k_p) * scale, -jnp.inf)
        o = _sink_softmax_pv([s_p, s_c],
                             [vp_ref[0, :, klo:klo + SWA_HEAD_DIM], vc_ref[0, :, klo:klo + SWA_HEAD_DIM]],
                             sink)
        for g in range(SWA_GROUP):
            lo = (h * SWA_GROUP + g) * SWA_HEAD_DIM
            o_ref[0, :, lo:lo + SWA_HEAD_DIM] = o[g * w:(g + 1) * w]


def _swa_prompt(q, k, v, sink_rows):
    b, l, _ = q.shape
    nb = l // WINDOW
    cur = lambda wd: pl.BlockSpec((1, WINDOW, wd), lambda i, j: (i, j, 0))
    prev = lambda wd: pl.BlockSpec((1, WINDOW, wd), lambda i, j: (i, jnp.maximum(j - 1, 0), 0))
    return pl.pallas_call(
        _swa_prompt_kernel,
        grid=(b, nb),
        in_specs=[cur(SWA_Q_DIM), cur(SWA_KV_DIM), prev(SWA_KV_DIM), cur(SWA_KV_DIM), prev(SWA_KV_DIM),
                  pl.BlockSpec(sink_rows.shape, lambda i, j: (0, 0, 0))],
        out_specs=cur(SWA_Q_DIM),
        out_shape=jax.ShapeDtypeStruct((b, l, SWA_Q_DIM), F32),
        compiler_params=pltpu.CompilerParams(dimension_semantics=("parallel", "parallel")),
        name="swa_prompt",
    )(q, k, k, v, v, sink_rows)


def _swa_sample_kernel(q_ref, kn_ref, vn_ref, kb_ref, vb_ref, sink_ref, o_ref, ko_ref, vo_ref, *, steps):
    t = steps
    nrows = kb_ref.shape[1]
    rows = SWA_GROUP * t
    bt = q_ref.shape[0]
    qt_b = lax.broadcasted_iota(jnp.int32, (rows, nrows), 0) % t
    ki_b = lax.broadcasted_iota(jnp.int32, (rows, nrows), 1)
    mask_buf = ((nrows - ki_b + qt_b) < WINDOW)[None]
    qt_n = lax.broadcasted_iota(jnp.int32, (rows, t), 0) % t
    ki_n = lax.broadcasted_iota(jnp.int32, (rows, t), 1)
    mask_new = (ki_n <= qt_n)[None]
    scale = SWA_HEAD_DIM ** -0.5
    bdot = lambda a, b_, dims: lax.dot_general(a.astype(BF16), b_.astype(BF16), dims,
                                               preferred_element_type=F32)
    nt_dims = (((2,), (2,)), ((0,), (0,)))
    nn_dims = (((2,), (1,)), ((0,), (0,)))
    for h in range(SWA_KV_HEADS):
        klo = h * SWA_HEAD_DIM
        q_h = jnp.concatenate(
            [q_ref[:, :, (h * SWA_GROUP + g) * SWA_HEAD_DIM:(h * SWA_GROUP + g + 1) * SWA_HEAD_DIM]
             for g in range(SWA_GROUP)], axis=1)
        sink = sink_ref[h][None]
        k_b = kb_ref[:, :, klo:klo + SWA_HEAD_DIM]
        k_n = kn_ref[:, :, klo:klo + SWA_HEAD_DIM]
        v_b = vb_ref[:, :, klo:klo + SWA_HEAD_DIM]
        v_n = vn_ref[:, :, klo:klo + SWA_HEAD_DIM]
        s_b = jnp.where(mask_buf, bdot(q_h, k_b, nt_dims) * scale, -jnp.inf)
        s_n = jnp.where(mask_new, bdot(q_h, k_n, nt_dims) * scale, -jnp.inf)
        m = jnp.maximum(jnp.maximum(jnp.max(s_b, axis=-1, keepdims=True),
                                    jnp.max(s_n, axis=-1, keepdims=True)), sink)
        p_b = jnp.exp(s_b - m)
        p_n = jnp.exp(s_n - m)
        denom = (jnp.sum(p_b, axis=-1, keepdims=True) + jnp.sum(p_n, axis=-1, keepdims=True)
                 + jnp.exp(sink - m))
        o = (bdot(p_b, v_b, nn_dims) + bdot(p_n, v_n, nn_dims)) / denom
        for g in range(SWA_GROUP):
            lo = (h * SWA_GROUP + g) * SWA_HEAD_DIM
            o_ref[:, :, lo:lo + SWA_HEAD_DIM] = o[:, g * t:(g + 1) * t]
    ko_ref[:, :nrows - t] = kb_ref[:, t:]
    ko_ref[:, nrows - t:] = kn_ref[...]
    vo_ref[:, :nrows - t] = vb_ref[:, t:]
    vo_ref[:, nrows - t:] = vn_ref[...]


def _swa_sample(q, k, v, k_buf, v_buf, sink_rows, bt):
    b, t, _ = q.shape
    nrows = k_buf.shape[1]
    new = lambda wd: pl.BlockSpec((bt, t, wd), lambda i: (i, 0, 0))
    buf = pl.BlockSpec((bt, nrows, SWA_KV_DIM), lambda i: (i, 0, 0))
    kern = functools.partial(_swa_sample_kernel, steps=t)
    return pl.pallas_call(
        kern,
        grid=(b // bt,),
        in_specs=[new(SWA_Q_DIM), new(SWA_KV_DIM), new(SWA_KV_DIM), buf, buf,
                  pl.BlockSpec(sink_rows.shape, lambda i: (0, 0, 0))],
        out_specs=[new(SWA_Q_DIM), buf, buf],
        out_shape=[jax.ShapeDtypeStruct((b, t, SWA_Q_DIM), F32),
                   jax.ShapeDtypeStruct((b, nrows, SWA_KV_DIM), F32),
                   jax.ShapeDtypeStruct((b, nrows, SWA_KV_DIM), F32)],
        compiler_params=pltpu.CompilerParams(dimension_semantics=("parallel",)),
        name="swa_sample",
    )(q, k, v, k_buf, v_buf, sink_rows)


def _out_kernel(x_ref, og_ref, z_ref, os_ref, ga_ref, gb_ref, gn_ref, wo_ref,
                nw2_ref, win_ref, wout_ref, nf_ref, y_ref):
    gn = gn_ref[...]
    parts = []
    for h in range(GDN_HEADS):
        lo = h * GDN_DV
        oh = og_ref[:, lo:lo + GDN_DV]
        zh = z_ref[:, lo:lo + GDN_DV]
        parts.append(_rms(oh, gn) * _silu(zh))
    o_g = jnp.concatenate(parts, axis=1)
    mixed = jax.nn.sigmoid(ga_ref[...]) * o_g + jax.nn.sigmoid(gb_ref[...]) * os_ref[...]
    x2 = x_ref[...] + jnp.dot(mixed.astype(BF16), wo_ref[...], preferred_element_type=F32)
    x3 = _swiglu_half_step(x2, nw2_ref[...], win_ref, wout_ref)
    y_ref[...] = _rms(x3, nf_ref[...])


def _out(x1, og, z, os_, ga, gb, gnorm, wo, nw2, win, wout, nf, tm):
    t = x1.shape[0]
    row = pl.BlockSpec((tm, D_MODEL), lambda i: (i, 0))
    wbytes = (wo.size + win.size + wout.size) * 2
    return pl.pallas_call(
        _out_kernel,
        grid=(t // tm,),
        in_specs=[row] * 6 + [_const_spec(gnorm.shape), _const_spec(wo.shape), _const_spec(nw2.shape),
                              _const_spec(win.shape), _const_spec(wout.shape), _const_spec(nf.shape)],
        out_specs=row,
        out_shape=jax.ShapeDtypeStruct((t, D_MODEL), F32),
        compiler_params=pltpu.CompilerParams(
            dimension_semantics=("parallel",),
            vmem_limit_bytes=_vmem_limit(wbytes + 26 * tm * D_MODEL * 4 + (8 << 20))),
        name="out",
    )(x1, og, z, os_, ga, gb, gnorm, wo, nw2, win, wout, nf)


def _rope_tables(pos):
    inv = ROPE_THETA ** (-jnp.arange(0, ROPE_DIM, 2, dtype=F32) / ROPE_DIM)
    ang = pos.astype(F32)[:, None] * inv[None, :]
    cos, sin = jnp.cos(ang), jnp.sin(ang)
    n = pos.shape[0]
    pad = SWA_HEAD_DIM - ROPE_DIM
    one_head = lambda first, second, fill: jnp.concatenate(
        [first, second, jnp.full((n, pad), fill, F32)], axis=1)
    zeros = jnp.zeros_like(sin)
    cos_f = one_head(cos, cos, 1.0)
    sin_a = one_head(-sin, zeros, 0.0)
    sin_b = one_head(zeros, sin, 0.0)
    reps = V7X_LANES // SWA_HEAD_DIM
    return tuple(jnp.tile(t, (1, reps)) for t in (cos_f, sin_a, sin_b))


def _pad_lanes(v, width):
    return jnp.pad(v.astype(F32), (0, width - v.shape[0]))[None, :]


def _token_tile(t, want):
    tm = min(want, t)
    while t % tm:
        tm //= 2
    return tm


def _branch(x, pos, conv_buf, s0, k_buf, v_buf, p, chunk):
    b, l, _ = x.shape
    t = b * l
    x2d = x.reshape(t, D_MODEL)
    x1 = _ffn(x2d, p['norm_ffn1'], p['w_ffn1_in'], p['w_ffn1_out'], _token_tile(t, 512))
    qkv, z, ab, qs, ks, vs, ga, gb = _inproj(x1, p['norm_mix'], p['w_in_parts'], _token_tile(t, 512))

    tables = _rope_tables(pos)
    if conv_buf is None:
        tl = _token_tile(l, 256)
        post_args = dict(tl=tl, group=tl, chunk=chunk, halo_is_prev_rows=True)
        shape3 = lambda a: a.reshape(b, l, a.shape[-1])
        halo = shape3(qkv)
    else:
        tl = _token_tile(t, 256)
        post_args = dict(tl=tl, group=l, chunk=chunk, halo_is_prev_rows=False)
        shape3 = lambda a: a.reshape(1, t, a.shape[-1])
        halo = jnp.pad(conv_buf, ((0, 0), (l - (CONV_W - 1), 0), (0, 0))).reshape(1, t, CONV_DIM)
        tables = tuple(jnp.tile(tb, (tl // l, 1)) for tb in tables)
    qg, kg, vg, qr, kr, gcb = _post(shape3(qkv), halo, shape3(qs), shape3(ks), shape3(ab),
                                    p['conv_w'], p['a_log_pad'], p['dt_bias_pad'], tables, **post_args)
    seq = lambda a: a.reshape(b, l, a.shape[-1])
    qg, kg, vg, qr, kr, gcb = (seq(a) for a in (qg, kg, vg, qr, kr, gcb))
    new_conv = qkv.reshape(b, l, CONV_DIM)[:, l - (CONV_W - 1):]

    n = l // chunk
    grow = gcb[:, :, :GDN_HEADS].reshape(b, n, chunk, GDN_HEADS).transpose(0, 1, 3, 2)
    if chunk == PK_C and s0 is None:
        nc = _token_tile(n, 4)
        u, w, qgd, kgd, qk, gl = _pk_pre(qg, kg, vg, gcb, grow.reshape(b, n, 1, PK_CAT), nc)
        og, new_s = _pk_scan(u, w, qgd, kgd, qk, gl)
    else:
        u, w, qgd, kgd, qk, gl = _gdn_pre(qg, kg, vg, gcb, grow, chunk)
        og, new_s = _gdn_scan(u, w, qgd, kgd, qk, gl, s0, chunk)

    vs3 = vs.reshape(b, l, SWA_KV_DIM)
    if k_buf is None:
        rows = SWA_GROUP * WINDOW
        sink_rows = jnp.repeat(p['sinks'], WINDOW, axis=1).reshape(SWA_KV_HEADS, rows, 1)
        os_ = _swa_prompt(qr, kr, vs3, sink_rows)
        new_k, new_v = kr[:, l - WINDOW:], vs3[:, l - WINDOW:]
    else:
        rows = SWA_GROUP * l
        sink_rows = jnp.repeat(p['sinks'], l, axis=1).reshape(SWA_KV_HEADS, rows, 1)
        nrows = k_buf.shape[1]
        os_, new_k, new_v = _swa_sample(qr, kr, vs3, k_buf.reshape(b, nrows, SWA_KV_DIM),
                                        v_buf.reshape(b, nrows, SWA_KV_DIM), sink_rows, _token_tile(b, 16))
    y = _out(x1, og.reshape(t, GDN_V_DIM), z, os_.reshape(t, SWA_Q_DIM), ga, gb, p['gdn_norm'], p['w_out'],
             p['norm_ffn2'], p['w_ffn2_in'], p['w_ffn2_out'], p['norm_final'], _token_tile(t, 256))
    kv_shape = (b, new_k.shape[1], SWA_KV_HEADS, SWA_HEAD_DIM)
    return (y.reshape(b, l, D_MODEL), new_conv, new_s, new_k.reshape(kv_shape), new_v.reshape(kv_shape))


def _layer_params(l, norm_ffn1, w_ffn1_in, w_ffn1_out, norm_mix, w_in, conv_w, gdn_a_log, gdn_dt_bias,
                  gdn_norm, swa_sinks, w_out, norm_ffn2, w_ffn2_in, w_ffn2_out, norm_final):
    bounds = [0]
    for s in IN_SPLITS:
        bounds.append(bounds[-1] + s)
    col = lambda i: w_in[l][:, bounds[i]:bounds[i + 1]]
    w_ab = jnp.pad(jnp.concatenate([col(2), col(3)], axis=1), ((0, 0), (0, V7X_LANES - 2 * GDN_HEADS)))
    parts = [col(0), col(1), w_ab, col(4), col(5), col(6), col(7), col(8)]
    row = lambda v: v.astype(F32)[None, :]
    return {
        'norm_ffn1': row(norm_ffn1[l]), 'w_ffn1_in': w_ffn1_in[l].astype(BF16),
        'w_ffn1_out': w_ffn1_out[l].astype(BF16),
        'norm_mix': row(norm_mix[l]), 'w_in_parts': [w.astype(BF16) for w in parts],
        'conv_w': conv_w[l].astype(F32),
        'a_log_pad': _pad_lanes(gdn_a_log[l], V7X_LANES), 'dt_bias_pad': _pad_lanes(gdn_dt_bias[l], V7X_LANES),
        'gdn_norm': row(gdn_norm[l]),
        'sinks': swa_sinks[l].astype(F32).reshape(SWA_KV_HEADS, SWA_GROUP),
        'w_out': w_out[l].astype(BF16),
        'norm_ffn2': row(norm_ffn2[l]), 'w_ffn2_in': w_ffn2_in[l].astype(BF16),
        'w_ffn2_out': w_ffn2_out[l].astype(BF16),
        'norm_final': row(norm_final),
    }


def kernel(x_prompt, x_sample, state_conv, state_gdn, cache_swa_k, cache_swa_v, norm_ffn1, w_ffn1_in, w_ffn1_out, norm_mix, w_in, conv_w, gdn_a_log, gdn_dt_bias, gdn_norm, swa_sinks, w_out, norm_ffn2, w_ffn2_in, w_ffn2_out, norm_final):
    depth = w_in.shape[0]
    assert depth == 1, "the final norm is fused into the layer's last kernel"
    l_p = x_prompt.shape[1]
    l_s = x_sample.shape[1]
    pos_p = jnp.arange(l_p, dtype=jnp.int32)
    pos_s = PAST_LEN + jnp.arange(l_s, dtype=jnp.int32)
    p = _layer_params(0, norm_ffn1, w_ffn1_in, w_ffn1_out, norm_mix, w_in, conv_w, gdn_a_log, gdn_dt_bias,
                      gdn_norm, swa_sinks, w_out, norm_ffn2, w_ffn2_in, w_ffn2_out, norm_final)
    yp, c1, s1, k1, v1 = _branch(x_prompt, pos_p, None, None, None, None, p, min(GDN_CHUNK, l_p))
    ys, c2, s2, k2, v2 = _branch(x_sample, pos_s, state_conv[0], state_gdn[0], cache_swa_k[0],
                                 cache_swa_v[0], p, min(GDN_CHUNK, l_s))
    return (yp, ys, c1[None], c2[None], s1[None], s2[None], k1[None], k2[None], v1[None], v2[None])
```

```python
import functools
import math

import jax
import jax.numpy as jnp
from jax import lax
from jax.experimental import pallas as pl
from jax.experimental.pallas import tpu as pltpu

D_MODEL = 1024
PAST_LEN = 16384
GDN_HEADS = 8
GDN_DK = 128
GDN_DV = 128
GDN_QK_DIM = GDN_HEADS * GDN_DK
GDN_V_DIM = GDN_HEADS * GDN_DV
CONV_W = 4
GDN_CHUNK = 64
CONV_DIM = 2 * GDN_QK_DIM + GDN_V_DIM
SWA_Q_HEADS = 16
SWA_KV_HEADS = 4
SWA_GROUP = SWA_Q_HEADS // SWA_KV_HEADS
SWA_HEAD_DIM = 64
SWA_Q_DIM = SWA_Q_HEADS * SWA_HEAD_DIM
SWA_KV_DIM = SWA_KV_HEADS * SWA_HEAD_DIM
WINDOW = 128
ROPE_DIM = SWA_HEAD_DIM // 4
ROPE_HALF = ROPE_DIM // 2
ROPE_THETA = 500000.0
D_FF = 2816
EPS = 1e-6
IN_SPLITS = (CONV_DIM, GDN_V_DIM, GDN_HEADS, GDN_HEADS, SWA_Q_DIM, SWA_KV_DIM, SWA_KV_DIM, D_MODEL, D_MODEL)

V7X_LANES = 128
V7X_SUBLANES = 8
V7X_MXU_DIM = 256
V7X_VMEM_BYTES = 64 * 1024 * 1024

FFN_CHUNK = V7X_MXU_DIM
SOLVE_BASE = V7X_SUBLANES

F32 = jnp.float32
BF16 = jnp.bfloat16


def _dot(a, b):
    return jnp.dot(a.astype(BF16), b.astype(BF16), preferred_element_type=F32)


def _dot_nt(a, b):
    return lax.dot_general(a.astype(BF16), b.astype(BF16), (((1,), (1,)), ((), ())),
                           preferred_element_type=F32)


def _dot_tn(a, b):
    return lax.dot_general(a.astype(BF16), b.astype(BF16), (((0,), (0,)), ((), ())),
                           preferred_element_type=F32)


def _rms(x, w):
    return x * lax.rsqrt(jnp.mean(x * x, axis=-1, keepdims=True) + EPS) * w


def _silu(x):
    return x * jax.nn.sigmoid(x)


def _vmem_limit(nbytes):
    return int(min(V7X_VMEM_BYTES - (4 << 20), max(nbytes, 16 << 20)))


def _const_spec(shape):
    nd = len(shape)
    return pl.BlockSpec(shape, lambda *_: (0,) * nd, pipeline_mode=pl.Buffered(1))


def _swiglu_half_step(x, norm_w, win_ref, wout_ref):
    xn = _rms(x, norm_w).astype(BF16)
    acc = None
    for c in range(D_FF // FFN_CHUNK):
        lo = c * FFN_CHUNK
        g = jnp.dot(xn, win_ref[:, lo:lo + FFN_CHUNK], preferred_element_type=F32)
        u = jnp.dot(xn, win_ref[:, D_FF + lo:D_FF + lo + FFN_CHUNK], preferred_element_type=F32)
        act = (_silu(g) * u).astype(BF16)
        part = jnp.dot(act, wout_ref[lo:lo + FFN_CHUNK, :], preferred_element_type=F32)
        acc = part if acc is None else acc + part
    return x + 0.5 * acc


def _ffn_kernel(x_ref, nw_ref, win_ref, wout_ref, o_ref):
    o_ref[...] = _swiglu_half_step(x_ref[...], nw_ref[...], win_ref, wout_ref)


def _ffn(x, norm_w, win, wout, tm):
    t = x.shape[0]
    wbytes = (win.size + wout.size) * 2
    return pl.pallas_call(
        _ffn_kernel,
        grid=(t // tm,),
        in_specs=[pl.BlockSpec((tm, D_MODEL), lambda i: (i, 0)),
                  _const_spec((1, D_MODEL)),
                  _const_spec(win.shape),
                  _const_spec(wout.shape)],
        out_specs=pl.BlockSpec((tm, D_MODEL), lambda i: (i, 0)),
        out_shape=jax.ShapeDtypeStruct((t, D_MODEL), F32),
        compiler_params=pltpu.CompilerParams(
            dimension_semantics=("parallel",),
            vmem_limit_bytes=_vmem_limit(wbytes + 12 * tm * D_MODEL * 4 + (8 << 20))),
        name="ffn1",
    )(x, norm_w, win, wout)


_PROJ_WIDTHS = (CONV_DIM, GDN_V_DIM, V7X_LANES, SWA_Q_DIM, SWA_KV_DIM, SWA_KV_DIM, D_MODEL, D_MODEL)


def _inproj_kernel(x_ref, nw_ref, *refs):
    n = len(_PROJ_WIDTHS)
    w_refs, o_refs = refs[:n], refs[n:]
    xn = _rms(x_ref[...], nw_ref[...]).astype(BF16)
    for w_ref, o_ref in zip(w_refs, o_refs):
        o_ref[...] = jnp.dot(xn, w_ref[...], preferred_element_type=F32)


def _inproj(x, norm_w, weights, tm):
    t = x.shape[0]
    wbytes = sum(w.size for w in weights) * 2
    obytes = sum(_PROJ_WIDTHS) * tm * 4
    return pl.pallas_call(
        _inproj_kernel,
        grid=(t // tm,),
        in_specs=[pl.BlockSpec((tm, D_MODEL), lambda i: (i, 0)), _const_spec((1, D_MODEL))]
                 + [_const_spec(w.shape) for w in weights],
        out_specs=[pl.BlockSpec((tm, n), lambda i: (i, 0)) for n in _PROJ_WIDTHS],
        out_shape=[jax.ShapeDtypeStruct((t, n), F32) for n in _PROJ_WIDTHS],
        compiler_params=pltpu.CompilerParams(
            dimension_semantics=("parallel",),
            vmem_limit_bytes=_vmem_limit(wbytes + 3 * obytes + 4 * tm * D_MODEL * 4 + (4 << 20))),
        name="inproj",
    )(x, norm_w, *weights)


def _rope(x, cos_f, sin_a, sin_b):
    width = x.shape[1]
    reps = width // V7X_LANES
    tile = lambda t: jnp.concatenate([t] * reps, axis=1) if reps > 1 else t
    return (x * tile(cos_f) + pltpu.roll(x, width - ROPE_HALF, 1) * tile(sin_a)
            + pltpu.roll(x, ROPE_HALF, 1) * tile(sin_b))


def _post_kernel(qkv_ref, halo_ref, qs_ref, ks_ref, ab_ref, cw_ref, alog_ref, dtb_ref,
                 cos_ref, sina_ref, sinb_ref,
                 qg_ref, kg_ref, vg_ref, qr_ref, kr_ref, gcb_ref, *, group, chunk, halo_is_prev_rows):
    raw = qkv_ref[0]
    rows = raw.shape[0]
    cw = cw_ref[...]
    acc = raw * cw[CONV_W - 1:CONV_W, :]
    if halo_is_prev_rows:
        halo = jnp.where(pl.program_id(1) > 0, halo_ref[0], 0.0)
        ext = jnp.concatenate([halo, raw], axis=0)
        for s in range(1, CONV_W):
            shifted = pltpu.roll(ext, s, 0)[V7X_SUBLANES:]
            acc = acc + shifted * cw[CONV_W - 1 - s:CONV_W - s, :]
    else:
        halo = halo_ref[0]
        pos = lax.broadcasted_iota(jnp.int32, (rows, 1), 0) % group
        for s in range(1, CONV_W):
            shifted = jnp.where(pos >= s, pltpu.roll(raw, s, 0),
                                pltpu.roll(halo, (rows + s - group) % rows, 0))
            acc = acc + shifted * cw[CONV_W - 1 - s:CONV_W - s, :]
    conv = _silu(acc)
    for h in range(GDN_HEADS):
        lo = h * GDN_DK
        qh = conv[:, lo:lo + GDN_DK]
        qg_ref[0, :, lo:lo + GDN_DK] = qh * lax.rsqrt(jnp.sum(qh * qh, axis=-1, keepdims=True) + EPS)
        kh = conv[:, GDN_QK_DIM + lo:GDN_QK_DIM + lo + GDN_DK]
        kg_ref[0, :, lo:lo + GDN_DK] = kh * lax.rsqrt(jnp.sum(kh * kh, axis=-1, keepdims=True) + EPS)
    vg_ref[0] = conv[:, 2 * GDN_QK_DIM:]

    ab = ab_ref[0]
    sp_in = ab + dtb_ref[...]
    softplus = jnp.maximum(sp_in, 0.0) + jnp.log1p(jnp.exp(-jnp.abs(sp_in)))
    gc = -jnp.exp(alog_ref[...]) * softplus
    cpos = lax.broadcasted_iota(jnp.int32, (rows, 1), 0) % chunk
    s = 1
    while s < chunk:
        gc = gc + jnp.where(cpos >= s, pltpu.roll(gc, s, 0), 0.0)
        s *= 2
    lane = lax.broadcasted_iota(jnp.int32, (1, V7X_LANES), 1)
    gcb_ref[0] = jnp.where(lane < GDN_HEADS, gc, jax.nn.sigmoid(ab))

    cos_f, sin_a, sin_b = cos_ref[...], sina_ref[...], sinb_ref[...]
    qr_ref[0] = _rope(qs_ref[0], cos_f, sin_a, sin_b)
    kr_ref[0] = _rope(ks_ref[0], cos_f, sin_a, sin_b)


def _post(qkv, halo, qs, ks, ab, cw, alog, dtb, tables, *, tl, group, chunk, halo_is_prev_rows):
    nb, rows, _ = qkv.shape
    nt = rows // tl
    row_spec = lambda n: pl.BlockSpec((1, tl, n), lambda b, i: (b, i, 0))
    if halo_is_prev_rows:
        per = tl // V7X_SUBLANES
        halo_spec = pl.BlockSpec((1, V7X_SUBLANES, CONV_DIM),
                                 lambda b, i: (b, jnp.maximum(i * per - 1, 0), 0))
        tab_spec = pl.BlockSpec((tl, V7X_LANES), lambda b, i: (i, 0))
    else:
        halo_spec = row_spec(CONV_DIM)
        tab_spec = pl.BlockSpec((tl, V7X_LANES), lambda b, i: (0, 0))
    kern = functools.partial(_post_kernel, group=group, chunk=chunk, halo_is_prev_rows=halo_is_prev_rows)
    widths = (GDN_QK_DIM, GDN_QK_DIM, GDN_V_DIM, SWA_Q_DIM, SWA_KV_DIM, V7X_LANES)
    return pl.pallas_call(
        kern,
        grid=(nb, nt),
        in_specs=[row_spec(CONV_DIM), halo_spec, row_spec(SWA_Q_DIM), row_spec(SWA_KV_DIM),
                  row_spec(V7X_LANES),
                  pl.BlockSpec((CONV_W, CONV_DIM), lambda b, i: (0, 0)),
                  pl.BlockSpec((1, V7X_LANES), lambda b, i: (0, 0)),
                  pl.BlockSpec((1, V7X_LANES), lambda b, i: (0, 0)),
                  tab_spec, tab_spec, tab_spec],
        out_specs=[row_spec(n) for n in widths],
        out_shape=[jax.ShapeDtypeStruct((nb, rows, n), F32) for n in widths],
        compiler_params=pltpu.CompilerParams(
            dimension_semantics=("parallel", "parallel"),
            vmem_limit_bytes=_vmem_limit(48 << 20)),
        name="post",
    )(qkv, halo, qs, ks, ab, cw, alog, dtb, *tables)


def _unit_lower_inverse(a, c):
    ii = lax.broadcasted_iota(jnp.int32, (c, c), 0)
    jj = lax.broadcasted_iota(jnp.int32, (c, c), 1)
    eye = (ii == jj).astype(F32)
    b = min(SOLVE_BASE, c)
    same = (ii // b) == (jj // b)
    ab = jnp.where(same, a, 0.0)
    d = eye - ab
    p = ab
    span = 2
    while span < b:
        p = _dot(p, p)
        d = d + _dot(d, p)
        span *= 2
    while b < c:
        same2 = (ii // (2 * b)) == (jj // (2 * b))
        off = jnp.where(same2 & jnp.logical_not(same), a, 0.0)
        d = d - _dot(d, _dot(off, d))
        same = same2
        b *= 2
    return d


def _gdn_pre_kernel(q_ref, k_ref, v_ref, gcb_ref, grow_ref,
                    u_ref, w_ref, qg_ref, kg_ref, qk_ref, gl_ref, *, chunk):
    c = chunk
    gcb = gcb_ref[0]
    grow = grow_ref[0, 0]
    ii = lax.broadcasted_iota(jnp.int32, (c, c), 0)
    jj = lax.broadcasted_iota(jnp.int32, (c, c), 1)
    for h in range(GDN_HEADS):
        lo = h * GDN_DK
        q = q_ref[0, :, lo:lo + GDN_DK] * (GDN_DK ** -0.5)
        k = k_ref[0, :, lo:lo + GDN_DK]
        v = v_ref[0, :, lo:lo + GDN_DV]
        gc = gcb[:, h:h + 1]
        beta = gcb[:, GDN_HEADS + h:GDN_HEADS + h + 1]
        gr = grow[h:h + 1, :]
        decay = jnp.exp(jnp.where(ii >= jj, gc - gr, -jnp.inf))
        kb = k * beta
        vb = v * beta
        eg = jnp.exp(gc)
        a = jnp.where(ii > jj, _dot_nt(kb, k) * decay, 0.0)
        t = _unit_lower_inverse(a, c)
        u_ref[0, :, lo:lo + GDN_DV] = _dot(t, vb)
        w_ref[0, :, lo:lo + GDN_DK] = _dot(t, kb * eg)
        qk_ref[0, :, h * c:(h + 1) * c] = _dot_nt(q, k) * decay
        qg_ref[0, :, lo:lo + GDN_DK] = q * eg
        g_end = gc[c - 1:c, :]
        kg_ref[0, :, lo:lo + GDN_DK] = k * jnp.exp(g_end - gc)
        gl_ref[0, 0, h:h + 1, :] = jnp.broadcast_to(jnp.exp(g_end), (1, V7X_LANES))


def _gdn_pre(q, k, v, gcb, grow, chunk):
    b, l, _ = q.shape
    n = l // chunk
    tok = lambda w: pl.BlockSpec((1, chunk, w), lambda i, j: (i, j, 0))
    kern = functools.partial(_gdn_pre_kernel, chunk=chunk)
    return pl.pallas_call(
        kern,
        grid=(b, n),
        in_specs=[tok(GDN_QK_DIM), tok(GDN_QK_DIM), tok(GDN_V_DIM), tok(V7X_LANES),
                  pl.BlockSpec((1, 1, GDN_HEADS, chunk), lambda i, j: (i, j, 0, 0))],
        out_specs=[tok(GDN_V_DIM), tok(GDN_QK_DIM), tok(GDN_QK_DIM), tok(GDN_QK_DIM),
                   tok(GDN_HEADS * chunk),
                   pl.BlockSpec((1, 1, GDN_HEADS, V7X_LANES), lambda i, j: (i, j, 0, 0))],
        out_shape=[jax.ShapeDtypeStruct((b, l, GDN_V_DIM), F32),
                   jax.ShapeDtypeStruct((b, l, GDN_QK_DIM), F32),
                   jax.ShapeDtypeStruct((b, l, GDN_QK_DIM), F32),
                   jax.ShapeDtypeStruct((b, l, GDN_QK_DIM), F32),
                   jax.ShapeDtypeStruct((b, l, GDN_HEADS * chunk), F32),
                   jax.ShapeDtypeStruct((b, n, GDN_HEADS, V7X_LANES), F32)],
        compiler_params=pltpu.CompilerParams(dimension_semantics=("parallel", "parallel")),
        name="gdn_pre",
    )(q, k, v, gcb, grow)


def _gdn_scan_kernel(*refs, chunk, has_s0):
    if has_s0:
        u_ref, w_ref, qg_ref, kg_ref, qk_ref, gl_ref, s0_ref, o_ref, sout_ref, s_ref = refs
    else:
        u_ref, w_ref, qg_ref, kg_ref, qk_ref, gl_ref, o_ref, sout_ref, s_ref = refs
    c = chunk
    j = pl.program_id(1)

    @pl.when(j == 0)
    def _():
        if has_s0:
            s_ref[...] = s0_ref[0]
        else:
            s_ref[...] = jnp.zeros_like(s_ref)

    for h in range(GDN_HEADS):
        lo = h * GDN_DK
        s_h = s_ref[h]
        s_b = s_h.astype(BF16)
        v_new = u_ref[0, :, lo:lo + GDN_DV] - _dot(w_ref[0, :, lo:lo + GDN_DK], s_b)
        o = _dot(qg_ref[0, :, lo:lo + GDN_DK], s_b) + _dot(qk_ref[0, :, h * c:(h + 1) * c], v_new)
        o_ref[0, :, lo:lo + GDN_DV] = o
        s_ref[h] = s_h * gl_ref[0, 0, h:h + 1, :] + _dot_tn(kg_ref[0, :, lo:lo + GDN_DK], v_new)

    @pl.when(j == pl.num_programs(1) - 1)
    def _():
        sout_ref[0] = s_ref[...]


def _gdn_scan(u, w, qg, kg, qk, gl, s0, chunk):
    b, l, _ = u.shape
    n = l // chunk
    tok = lambda wd: pl.BlockSpec((1, chunk, wd), lambda i, j: (i, j, 0))
    state_spec = pl.BlockSpec((1, GDN_HEADS, GDN_DK, GDN_DV), lambda i, j: (i, 0, 0, 0))
    in_specs = [tok(GDN_V_DIM), tok(GDN_QK_DIM), tok(GDN_QK_DIM), tok(GDN_QK_DIM),
                tok(GDN_HEADS * chunk),
                pl.BlockSpec((1, 1, GDN_HEADS, V7X_LANES), lambda i, j: (i, j, 0, 0))]
    args = [u, w, qg, kg, qk, gl]
    if s0 is not None:
        in_specs.append(state_spec)
        args.append(s0)
    kern = functools.partial(_gdn_scan_kernel, chunk=chunk, has_s0=s0 is not None)
    return pl.pallas_call(
        kern,
        grid=(b, n),
        in_specs=in_specs,
        out_specs=[tok(GDN_V_DIM), state_spec],
        out_shape=[jax.ShapeDtypeStruct((b, l, GDN_V_DIM), F32),
                   jax.ShapeDtypeStruct((b, GDN_HEADS, GDN_DK, GDN_DV), F32)],
        scratch_shapes=[pltpu.VMEM((GDN_HEADS, GDN_DK, GDN_DV), F32)],
        compiler_params=pltpu.CompilerParams(dimension_semantics=("parallel", "arbitrary")),
        name="gdn_scan",
    )(*args)


PK_C = GDN_CHUNK
PK_CAT = GDN_HEADS * PK_C
PK_PAIR = 2 * GDN_DK
PK_NPAIR = GDN_HEADS // 2
PK_GROUP = V7X_MXU_DIM // PK_C


def _pk_masks():
    c = PK_C
    i = lax.broadcasted_iota(jnp.int32, (c, PK_CAT), 0)
    j = lax.broadcasted_iota(jnp.int32, (c, PK_CAT), 1) % c
    r = lax.broadcasted_iota(jnp.int32, (V7X_MXU_DIM, V7X_MXU_DIM), 0) // c
    l = lax.broadcasted_iota(jnp.int32, (V7X_MXU_DIM, V7X_MXU_DIM), 1) // c
    return i, j, (r == l).astype(BF16)


def _pk_prod(x, y, bd_mask):
    outs = []
    for g in range(GDN_HEADS // PK_GROUP):
        sl = slice(g * V7X_MXU_DIM, (g + 1) * V7X_MXU_DIM)
        yb = y[:, sl].astype(BF16)
        bd = jnp.concatenate([yb] * PK_GROUP, axis=0) * bd_mask
        outs.append(jnp.dot(x[:, sl].astype(BF16), bd, preferred_element_type=F32))
    return jnp.concatenate(outs, axis=1)


def _pk_solve(a_list, i, j, bd_mask):
    prod_all = lambda xs, ys: [_pk_prod(x, y, bd_mask) for x, y in zip(xs, ys)]
    eye = (i == j).astype(F32)
    b = SOLVE_BASE
    same = (i // b) == (j // b)
    p = [jnp.where(same, a, 0.0) for a in a_list]
    d = [eye - x for x in p]
    span = 2
    while span < b:
        p = prod_all(p, p)
        d = [x + y for x, y in zip(d, prod_all(d, p))]
        span *= 2
    while b < PK_C:
        same2 = (i // (2 * b)) == (j // (2 * b))
        sel = same2 & jnp.logical_not(same)
        off = [jnp.where(sel, a, 0.0) for a in a_list]
        d = [x - y for x, y in zip(d, prod_all(d, prod_all(off, d)))]
        same = same2
        b *= 2
    return d


def _pk_pre_kernel(q_ref, k_ref, v_ref, gcb_ref, grow_ref,
                   u_ref, w_ref, qg_ref, kg_ref, qk_ref, gl_ref, *, nc):
    c = PK_C
    i, j, bd_mask = _pk_masks()
    lane_lo = lax.broadcasted_iota(jnp.int32, (c, V7X_LANES), 1) < c
    zeros_k = jnp.zeros((c, GDN_DK), BF16)
    a_list, qk_list, vb_list, kbg_list = [], [], [], []
    for ch in range(nc):
        rows = slice(ch * c, (ch + 1) * c)
        gcb = gcb_ref[0, rows, :]
        g_end = gcb[c - 1:c, :]
        eg_s = jnp.exp(gcb)
        ek_s = jnp.exp(g_end - gcb)
        col = lambda arr, n: jnp.broadcast_to(arr[:, n:n + 1], (c, V7X_LANES))
        full = lambda arr, base: jnp.concatenate([col(arr, base + h) for h in range(GDN_HEADS)], axis=1)
        gc_cols = [col(gcb, h) for h in range(GDN_HEADS)]
        beta_f = full(gcb, GDN_HEADS)
        eg_f = full(eg_s, 0)
        ek_f = full(ek_s, 0)
        gc_cat = jnp.concatenate([jnp.where(lane_lo, gc_cols[2 * m], gc_cols[2 * m + 1])
                                  for m in range(PK_NPAIR)], axis=1)
        decay = jnp.exp(jnp.where(i >= j, gc_cat - grow_ref[0, ch], -jnp.inf))
        q = q_ref[0, rows, :] * (GDN_DK ** -0.5)
        k = k_ref[0, rows, :]
        kb = k * beta_f
        vb_list.append((v_ref[0, rows, :] * beta_f).astype(BF16))
        kbg_list.append((kb * eg_f).astype(BF16))
        qg_ref[0, rows, :] = (q * eg_f).astype(BF16)
        kg_ref[0, rows, :] = (k * ek_f).astype(BF16)
        gl_ref[0, ch] = eg_f[c - 1:c, :]
        kk = []
        for m in range(PK_NPAIR):
            sl = slice(m * PK_PAIR, (m + 1) * PK_PAIR)
            lhs = jnp.concatenate([kb[:, sl], q[:, sl]], axis=0).astype(BF16)
            k2 = k[:, sl].astype(BF16)
            rhs_nt = jnp.concatenate(
                [jnp.concatenate([k2[:, :GDN_DK], zeros_k], axis=1),
                 jnp.concatenate([zeros_k, k2[:, GDN_DK:]], axis=1)], axis=0)
            kk.append(lax.dot_general(lhs, rhs_nt, (((1,), (1,)), ((), ())), preferred_element_type=F32))
        a_raw = jnp.concatenate([x[:c] for x in kk], axis=1)
        qk_raw = jnp.concatenate([x[c:] for x in kk], axis=1)
        a_list.append(jnp.where(i > j, a_raw * decay, 0.0))
        qk_ref[0, rows, :] = (qk_raw * decay).astype(BF16)
    t_list = _pk_solve(a_list, i, j, bd_mask)
    for ch in range(nc):
        rows = slice(ch * c, (ch + 1) * c)
        t = t_list[ch]
        for m in range(PK_NPAIR):
            tp = t[:, m * V7X_LANES:(m + 1) * V7X_LANES]
            lhs = jnp.concatenate([jnp.where(lane_lo, tp, 0.0), jnp.where(lane_lo, 0.0, tp)],
                                  axis=0).astype(BF16)
            h0, h1 = 2 * m, 2 * m + 1
            hs = lambda h: slice(h * GDN_DK, (h + 1) * GDN_DK)
            rhs = jnp.concatenate(
                [jnp.concatenate([vb_list[ch][:, hs(h0)], kbg_list[ch][:, hs(h0)]], axis=1),
                 jnp.concatenate([vb_list[ch][:, hs(h1)], kbg_list[ch][:, hs(h1)]], axis=1)], axis=0)
            uw = jnp.dot(lhs, rhs, preferred_element_type=F32)
            u_ref[0, rows, hs(h0)] = uw[:c, :GDN_DV]
            w_ref[0, rows, hs(h0)] = uw[:c, GDN_DV:].astype(BF16)
            u_ref[0, rows, hs(h1)] = uw[c:, :GDN_DV]
            w_ref[0, rows, hs(h1)] = uw[c:, GDN_DV:].astype(BF16)


def _pk_pre(q, k, v, gcb, grow_flat, nc):
    b, l, _ = q.shape
    tl = nc * PK_C
    tok = lambda wd: pl.BlockSpec((1, tl, wd), lambda bi, ti: (bi, ti, 0))
    per_chunk = lambda wd: pl.BlockSpec((1, nc, 1, wd), lambda bi, ti: (bi, ti, 0, 0))
    kern = functools.partial(_pk_pre_kernel, nc=nc)
    n = l // PK_C
    return pl.pallas_call(
        kern,
        grid=(b, l // tl),
        in_specs=[tok(GDN_QK_DIM), tok(GDN_QK_DIM), tok(GDN_V_DIM), tok(V7X_LANES), per_chunk(PK_CAT)],
        out_specs=[tok(GDN_V_DIM), tok(GDN_QK_DIM), tok(GDN_QK_DIM), tok(GDN_QK_DIM), tok(PK_CAT),
                   per_chunk(GDN_V_DIM)],
        out_shape=[jax.ShapeDtypeStruct((b, l, GDN_V_DIM), F32),
                   jax.ShapeDtypeStruct((b, l, GDN_QK_DIM), BF16),
                   jax.ShapeDtypeStruct((b, l, GDN_QK_DIM), BF16),
                   jax.ShapeDtypeStruct((b, l, GDN_QK_DIM), BF16),
                   jax.ShapeDtypeStruct((b, l, PK_CAT), BF16),
                   jax.ShapeDtypeStruct((b, n, 1, GDN_V_DIM), F32)],
        compiler_params=pltpu.CompilerParams(dimension_semantics=("parallel", "parallel"),
                                             vmem_limit_bytes=_vmem_limit(48 << 20)),
        name="gdn_pre_packed",
    )(q, k, v, gcb, grow_flat)


def _pk_scan_kernel(u_ref, w_ref, qg_ref, kg_ref, qk_ref, gl_ref, o_ref, sout_ref, s_ref):
    c = PK_C
    nb = u_ref.shape[0]
    step = pl.program_id(0)

    @pl.when(step == 0)
    def _():
        s_ref[...] = jnp.zeros_like(s_ref)

    zeros_s = jnp.zeros((GDN_DK, GDN_DV), BF16)
    zeros_v = jnp.zeros((c, GDN_DV), BF16)
    chains = [(b, m) for b in range(nb) for m in range(PK_NPAIR)]
    pair = lambda m: slice(m * PK_PAIR, (m + 1) * PK_PAIR)
    s_old, wq = {}, {}
    for b, m in chains:
        s_c = s_ref[b, m]
        s_old[b, m] = s_c
        sb = s_c.astype(BF16)
        bd_s = jnp.concatenate([jnp.concatenate([sb[:, :GDN_DV], zeros_s], axis=1),
                                jnp.concatenate([zeros_s, sb[:, GDN_DV:]], axis=1)], axis=0)
        lhs = jnp.concatenate([w_ref[b, :, pair(m)], qg_ref[b, :, pair(m)]], axis=0)
        wq[b, m] = jnp.dot(lhs, bd_s, preferred_element_type=F32)
    bd_v = {}
    for b, m in chains:
        v_new = (u_ref[b, :, pair(m)] - wq[b, m][:c]).astype(BF16)
        bd_v[b, m] = jnp.concatenate([jnp.concatenate([v_new[:, :GDN_DV], zeros_v], axis=1),
                                      jnp.concatenate([zeros_v, v_new[:, GDN_DV:]], axis=1)], axis=0)
    for b, m in chains:
        qk_p = qk_ref[b, :, m * V7X_LANES:(m + 1) * V7X_LANES]
        o_ref[b, :, pair(m)] = wq[b, m][c:] + jnp.dot(qk_p, bd_v[b, m], preferred_element_type=F32)
        kg_p = kg_ref[b, :, pair(m)]
        kg_st = jnp.concatenate([kg_p[:, :GDN_DK], kg_p[:, GDN_DK:]], axis=0)
        upd = lax.dot_general(kg_st, bd_v[b, m], (((0,), (0,)), ((), ())), preferred_element_type=F32)
        s_ref[b, m] = s_old[b, m] * gl_ref[b, 0, :, pair(m)] + upd

    @pl.when(step == pl.num_programs(0) - 1)
    def _():
        for b, m in chains:
            s_c = s_ref[b, m]
            sout_ref[b, 2 * m] = s_c[:, :GDN_DV]
            sout_ref[b, 2 * m + 1] = s_c[:, GDN_DV:]


def _pk_scan(u, w, qg, kg, qk, gl):
    b, l, _ = u.shape
    n = l // PK_C
    tok = lambda wd: pl.BlockSpec((b, PK_C, wd), lambda s: (0, s, 0))
    return pl.pallas_call(
        _pk_scan_kernel,
        grid=(n,),
        in_specs=[tok(GDN_V_DIM), tok(GDN_QK_DIM), tok(GDN_QK_DIM), tok(GDN_QK_DIM), tok(PK_CAT),
                  pl.BlockSpec((b, 1, 1, GDN_V_DIM), lambda s: (0, s, 0, 0))],
        out_specs=[tok(GDN_V_DIM),
                   pl.BlockSpec((b, GDN_HEADS, GDN_DK, GDN_DV), lambda s: (0, 0, 0, 0))],
        out_shape=[jax.ShapeDtypeStruct((b, l, GDN_V_DIM), F32),
                   jax.ShapeDtypeStruct((b, GDN_HEADS, GDN_DK, GDN_DV), F32)],
        scratch_shapes=[pltpu.VMEM((b, PK_NPAIR, GDN_DK, 2 * GDN_DV), F32)],
        compiler_params=pltpu.CompilerParams(dimension_semantics=("arbitrary",)),
        name="gdn_scan_packed",
    )(u, w, qg, kg, qk, gl)


def _head_block_rows(x, t):
    width = x.shape[1] // GDN_HEADS
    z = jnp.zeros((t, width), x.dtype)
    rows = []
    for h in range(GDN_HEADS):
        rows.append(jnp.concatenate([z] * h + [x[:, h * width:(h + 1) * width]]
                                    + [z] * (GDN_HEADS - 1 - h), axis=1))
    return jnp.concatenate(rows, axis=0)


def _gdn_sample_kernel(q_ref, k_ref, v_ref, gcb_ref, grep_ref, grow_ref, s0_ref, o_ref, sout_ref):
    bt, t, _ = q_ref.shape
    cat = GDN_HEADS * t
    i = lax.broadcasted_iota(jnp.int32, (t, cat), 0)
    j = lax.broadcasted_iota(jnp.int32, (t, cat), 1) % t
    r = lax.broadcasted_iota(jnp.int32, (cat, cat), 0) // t
    l = lax.broadcasted_iota(jnp.int32, (cat, cat), 1) // t
    same_head = r == l
    eye = (i == j).astype(F32)
    seqs = range(bt)
    pair = lambda m: slice(m * PK_PAIR, (m + 1) * PK_PAIR)

    vb, kbg, qg, kg, gl, a, qk = {}, {}, {}, {}, {}, {}, {}
    for b in seqs:
        gcb = gcb_ref[b]
        g_end = gcb[t - 1:t, :]
        eg_s = jnp.exp(gcb)
        ek_s = jnp.exp(g_end - gcb)
        col = lambda arr, n: jnp.broadcast_to(arr[:, n:n + 1], (t, V7X_LANES))
        full = lambda arr, base: jnp.concatenate([col(arr, base + h) for h in range(GDN_HEADS)], axis=1)
        beta_f, eg_f, ek_f = full(gcb, GDN_HEADS), full(eg_s, 0), full(ek_s, 0)
        decay = jnp.exp(jnp.where(i >= j, grep_ref[b] - grow_ref[b], -jnp.inf))
        q = q_ref[b] * (GDN_DK ** -0.5)
        k = k_ref[b]
        kb = k * beta_f
        vb[b] = v_ref[b] * beta_f
        kbg[b] = kb * eg_f
        qg[b] = q * eg_f
        kg[b] = k * ek_f
        gl[b] = eg_f[t - 1:t, :]
        kk = _dot_nt(jnp.concatenate([kb, q], axis=0), _head_block_rows(k, t))
        a[b] = jnp.where(i > j, kk[:t] * decay, 0.0)
        qk[b] = kk[t:] * decay

    def prod_all(xs, ys):
        return {b: _dot(xs[b], jnp.where(same_head, jnp.concatenate([ys[b]] * GDN_HEADS, axis=0), 0.0))
                for b in seqs}

    d = {b: eye - a[b] for b in seqs}
    p = a
    span = 2
    while span < t:
        p = prod_all(p, p)
        dp = prod_all(d, p)
        d = {b: d[b] + dp[b] for b in seqs}
        span *= 2

    u, w = {}, {}
    for b in seqs:
        rhs = jnp.concatenate([_head_block_rows(vb[b], t), _head_block_rows(kbg[b], t)], axis=1)
        uw = _dot(d[b], rhs)
        u[b], w[b] = uw[:, :GDN_V_DIM], uw[:, GDN_V_DIM:]

    zeros_s = jnp.zeros((GDN_DK, GDN_DV), BF16)
    wq = {}
    for b in seqs:
        for m in range(PK_NPAIR):
            s0 = s0_ref[b, 2 * m].astype(BF16)
            s1 = s0_ref[b, 2 * m + 1].astype(BF16)
            bd_s = jnp.concatenate([jnp.concatenate([s0, zeros_s], axis=1),
                                    jnp.concatenate([zeros_s, s1], axis=1)], axis=0)
            lhs = jnp.concatenate([w[b][:, pair(m)], qg[b][:, pair(m)]], axis=0).astype(BF16)
            wq[b, m] = jnp.dot(lhs, bd_s, preferred_element_type=F32)
    v_new = {b: u[b] - jnp.concatenate([wq[b, m][:t] for m in range(PK_NPAIR)], axis=1) for b in seqs}
    zeros_v = jnp.zeros((t, GDN_DV), F32)
    for b in seqs:
        o_ref[b] = (jnp.concatenate([wq[b, m][t:] for m in range(PK_NPAIR)], axis=1)
                    + _dot(qk[b], _head_block_rows(v_new[b], t)))
        for m in range(PK_NPAIR):
            vp = v_new[b][:, pair(m)]
            bd_v = jnp.concatenate([jnp.concatenate([vp[:, :GDN_DV], zeros_v], axis=1),
                                    jnp.concatenate([zeros_v, vp[:, GDN_DV:]], axis=1)], axis=0)
            kp = kg[b][:, pair(m)]
            kg_st = jnp.concatenate([kp[:, :GDN_DK], kp[:, GDN_DK:]], axis=0)
            upd = _dot_tn(kg_st, bd_v)
            glp = gl[b][:, pair(m)]
            sout_ref[b, 2 * m] = s0_ref[b, 2 * m] * glp[:, :GDN_DV] + upd[:, :GDN_DV]
            sout_ref[b, 2 * m + 1] = s0_ref[b, 2 * m + 1] * glp[:, GDN_DV:] + upd[:, GDN_DV:]


def _gdn_sample(q, k, v, gcb, grep, grow_flat, s0, bt):
    b, t, _ = q.shape
    assert t == V7X_SUBLANES, "one sublane tile of steps per sequence"
    cat = GDN_HEADS * t
    tok = lambda wd: pl.BlockSpec((bt, t, wd), lambda i: (i, 0, 0))
    state = pl.BlockSpec((bt, GDN_HEADS, GDN_DK, GDN_DV), lambda i: (i, 0, 0, 0))
    return pl.pallas_call(
        _gdn_sample_kernel,
        grid=(b // bt,),
        in_specs=[tok(GDN_QK_DIM), tok(GDN_QK_DIM), tok(GDN_V_DIM), tok(V7X_LANES), tok(cat),
                  pl.BlockSpec((bt, 1, cat), lambda i: (i, 0, 0)), state],
        out_specs=[tok(GDN_V_DIM), state],
        out_shape=[jax.ShapeDtypeStruct((b, t, GDN_V_DIM), F32),
                   jax.ShapeDtypeStruct((b, GDN_HEADS, GDN_DK, GDN_DV), F32)],
        compiler_params=pltpu.CompilerParams(dimension_semantics=("parallel",),
                                             vmem_limit_bytes=_vmem_limit(40 << 20)),
        name="gdn_sample",
    )(q, k, v, gcb, grep, grow_flat, s0)


def _swa_prompt_kernel(q_ref, kc_ref, kp_ref, vc_ref, vp_ref, sink_ref, o_ref):
    w = WINDOW
    hd = SWA_HEAD_DIM
    nkeys = 2 * w
    ncols = 2 * w
    has_prev = pl.program_id(1) > 0
    kj = lax.broadcasted_iota(jnp.int32, (nkeys, ncols), 0)
    qi = lax.broadcasted_iota(jnp.int32, (nkeys, ncols), 1) % w
    mask = ((kj >= w) & ((kj - w) <= qi)) | ((kj < w) & (kj > qi) & has_prev)
    lo_keys = lax.broadcasted_iota(jnp.int32, (nkeys, V7X_LANES), 1) < hd
    lo_q = lax.broadcasted_iota(jnp.int32, (w, V7X_LANES), 1) < hd
    scale = hd ** -0.5
    pairs = [(h, gp) for h in range(SWA_KV_HEADS) for gp in range(SWA_GROUP // 2)]

    kk, vbd = {}, {}
    for hp in range(SWA_KV_HEADS // 2):
        sl = slice(hp * V7X_LANES, (hp + 1) * V7X_LANES)
        kt = jnp.concatenate([kp_ref[0, :, sl], kc_ref[0, :, sl]], axis=0)
        vt = jnp.concatenate([vp_ref[0, :, sl], vc_ref[0, :, sl]], axis=0)
        kr = pltpu.roll(kt, hd, 1)
        vr = pltpu.roll(vt, hd, 1)
        for par, h in enumerate((2 * hp, 2 * hp + 1)):
            kk[h] = (jnp.where(lo_keys, kt, kr) if par == 0 else jnp.where(lo_keys, kr, kt)).astype(BF16)
            vv = jnp.where(lo_keys, vt, vr) if par == 0 else jnp.where(lo_keys, vr, vt)
            vbd[h] = jnp.concatenate([jnp.where(lo_keys, vv, 0.0), jnp.where(lo_keys, 0.0, vv)],
                                     axis=0).astype(BF16)
    s = {}
    for h, gp in pairs:
        lo = (h * SWA_GROUP + 2 * gp) * hd
        qp = q_ref[0, :, lo:lo + V7X_LANES]
        rq = jnp.concatenate([jnp.where(lo_q, qp, 0.0), jnp.where(lo_q, 0.0, qp)], axis=0).astype(BF16)
        sc = lax.dot_general(kk[h], rq, (((1,), (1,)), ((), ())), preferred_element_type=F32)
        s[h, gp] = jnp.where(mask, sc * scale, -jnp.inf)
    pn = {}
    for h, gp in pairs:
        sink = sink_ref[h * (SWA_GROUP // 2) + gp]
        m = jnp.maximum(jnp.max(s[h, gp], axis=0, keepdims=True), sink)
        p = jnp.exp(s[h, gp] - m)
        denom = jnp.sum(p, axis=0, keepdims=True) + jnp.exp(sink - m)
        p = (p * (1.0 / denom)).astype(BF16)
        pn[h, gp] = jnp.concatenate([p[:, :w], p[:, w:]], axis=0)
    for h, gp in pairs:
        lo = (h * SWA_GROUP + 2 * gp) * hd
        o_ref[0, :, lo:lo + V7X_LANES] = lax.dot_general(
            pn[h, gp], vbd[h], (((0,), (0,)), ((), ())), preferred_element_type=F32)


def _swa_prompt(q, k, v, sink_rows):
    b, l, _ = q.shape
    nb = l // WINDOW
    cur = lambda wd: pl.BlockSpec((1, WINDOW, wd), lambda i, j: (i, j, 0))
    prev = lambda wd: pl.BlockSpec((1, WINDOW, wd), lambda i, j: (i, jnp.maximum(j - 1, 0), 0))
    return pl.pallas_call(
        _swa_prompt_kernel,
        grid=(b, nb),
        in_specs=[cur(SWA_Q_DIM), cur(SWA_KV_DIM), prev(SWA_KV_DIM), cur(SWA_KV_DIM), prev(SWA_KV_DIM),
                  pl.BlockSpec(sink_rows.shape, lambda i, j: (0, 0, 0))],
        out_specs=cur(SWA_Q_DIM),
        out_shape=jax.ShapeDtypeStruct((b, l, SWA_Q_DIM), F32),
        compiler_params=pltpu.CompilerParams(dimension_semantics=("parallel", "parallel")),
        name="swa_prompt",
    )(q, k, k, v, v, sink_rows)


def _swa_sample_kernel(q_ref, kn_ref, vn_ref, kb_ref, vb_ref, sink_ref, o_ref, ko_ref, vo_ref, *, steps):
    t = steps
    nrows = kb_ref.shape[1]
    rows = SWA_GROUP * t
    bt = q_ref.shape[0]
    qt_b = lax.broadcasted_iota(jnp.int32, (rows, nrows), 0) % t
    ki_b = lax.broadcasted_iota(jnp.int32, (rows, nrows), 1)
    mask_buf = ((nrows - ki_b + qt_b) < WINDOW)[None]
    qt_n = lax.broadcasted_iota(jnp.int32, (rows, t), 0) % t
    ki_n = lax.broadcasted_iota(jnp.int32, (rows, t), 1)
    mask_new = (ki_n <= qt_n)[None]
    scale = SWA_HEAD_DIM ** -0.5
    bdot = lambda a, b_, dims: lax.dot_general(a.astype(BF16), b_.astype(BF16), dims,
                                               preferred_element_type=F32)
    nt_dims = (((2,), (2,)), ((0,), (0,)))
    nn_dims = (((2,), (1,)), ((0,), (0,)))
    for h in range(SWA_KV_HEADS):
        klo = h * SWA_HEAD_DIM
        q_h = jnp.concatenate(
            [q_ref[:, :, (h * SWA_GROUP + g) * SWA_HEAD_DIM:(h * SWA_GROUP + g + 1) * SWA_HEAD_DIM]
             for g in range(SWA_GROUP)], axis=1)
        sink = sink_ref[h][None]
        k_b = kb_ref[:, :, klo:klo + SWA_HEAD_DIM]
        k_n = kn_ref[:, :, klo:klo + SWA_HEAD_DIM]
        v_b = vb_ref[:, :, klo:klo + SWA_HEAD_DIM]
        v_n = vn_ref[:, :, klo:klo + SWA_HEAD_DIM]
        s_b = jnp.where(mask_buf, bdot(q_h, k_b, nt_dims) * scale, -jnp.inf)
        s_n = jnp.where(mask_new, bdot(q_h, k_n, nt_dims) * scale, -jnp.inf)
        m = jnp.maximum(jnp.maximum(jnp.max(s_b, axis=-1, keepdims=True),
                                    jnp.max(s_n, axis=-1, keepdims=True)), sink)
        p_b = jnp.exp(s_b - m)
        p_n = jnp.exp(s_n - m)
        denom = (jnp.sum(p_b, axis=-1, keepdims=True) + jnp.sum(p_n, axis=-1, keepdims=True)
                 + jnp.exp(sink - m))
        o = (bdot(p_b, v_b, nn_dims) + bdot(p_n, v_n, nn_dims)) / denom
        for g in range(SWA_GROUP):
            lo = (h * SWA_GROUP + g) * SWA_HEAD_DIM
            o_ref[:, :, lo:lo + SWA_HEAD_DIM] = o[:, g * t:(g + 1) * t]
    ko_ref[:, :nrows - t] = kb_ref[:, t:]
    ko_ref[:, nrows - t:] = kn_ref[...]
    vo_ref[:, :nrows - t] = vb_ref[:, t:]
    vo_ref[:, nrows - t:] = vn_ref[...]


def _swa_sample(q, k, v, k_buf, v_buf, sink_rows, bt):
    b, t, _ = q.shape
    nrows = k_buf.shape[1]
    new = lambda wd: pl.BlockSpec((bt, t, wd), lambda i: (i, 0, 0))
    buf = pl.BlockSpec((bt, nrows, SWA_KV_DIM), lambda i: (i, 0, 0))
    kern = functools.partial(_swa_sample_kernel, steps=t)
    return pl.pallas_call(
        kern,
        grid=(b // bt,),
        in_specs=[new(SWA_Q_DIM), new(SWA_KV_DIM), new(SWA_KV_DIM), buf, buf,
                  pl.BlockSpec(sink_rows.shape, lambda i: (0, 0, 0))],
        out_specs=[new(SWA_Q_DIM), buf, buf],
        out_shape=[jax.ShapeDtypeStruct((b, t, SWA_Q_DIM), F32),
                   jax.ShapeDtypeStruct((b, nrows, SWA_KV_DIM), F32),
                   jax.ShapeDtypeStruct((b, nrows, SWA_KV_DIM), F32)],
        compiler_params=pltpu.CompilerParams(dimension_semantics=("parallel",)),
        name="swa_sample",
    )(q, k, v, k_buf, v_buf, sink_rows)


def _out_kernel(x_ref, og_ref, z_ref, os_ref, ga_ref, gb_ref, gn_ref, wo_ref,
                nw2_ref, win_ref, wout_ref, nf_ref, y_ref):
    gn = gn_ref[...]
    parts = []
    for h in range(GDN_HEADS):
        lo = h * GDN_DV
        oh = og_ref[:, lo:lo + GDN_DV]
        zh = z_ref[:, lo:lo + GDN_DV]
        parts.append(_rms(oh, gn) * _silu(zh))
    o_g = jnp.concatenate(parts, axis=1)
    mixed = jax.nn.sigmoid(ga_ref[...]) * o_g + jax.nn.sigmoid(gb_ref[...]) * os_ref[...]
    x2 = x_ref[...] + jnp.dot(mixed.astype(BF16), wo_ref[...], preferred_element_type=F32)
    x3 = _swiglu_half_step(x2, nw2_ref[...], win_ref, wout_ref)
    y_ref[...] = _rms(x3, nf_ref[...])


def _out(x1, og, z, os_, ga, gb, gnorm, wo, nw2, win, wout, nf, tm):
    t = x1.shape[0]
    row = pl.BlockSpec((tm, D_MODEL), lambda i: (i, 0))
    wbytes = (wo.size + win.size + wout.size) * 2
    return pl.pallas_call(
        _out_kernel,
        grid=(t // tm,),
        in_specs=[row] * 6 + [_const_spec(gnorm.shape), _const_spec(wo.shape), _const_spec(nw2.shape),
                              _const_spec(win.shape), _const_spec(wout.shape), _const_spec(nf.shape)],
        out_specs=row,
        out_shape=jax.ShapeDtypeStruct((t, D_MODEL), F32),
        compiler_params=pltpu.CompilerParams(
            dimension_semantics=("parallel",),
            vmem_limit_bytes=_vmem_limit(wbytes + 26 * tm * D_MODEL * 4 + (8 << 20))),
        name="out",
    )(x1, og, z, os_, ga, gb, gnorm, wo, nw2, win, wout, nf)


def _rope_tables(pos):
    inv = ROPE_THETA ** (-jnp.arange(0, ROPE_DIM, 2, dtype=F32) / ROPE_DIM)
    ang = pos.astype(F32)[:, None] * inv[None, :]
    cos, sin = jnp.cos(ang), jnp.sin(ang)
    n = pos.shape[0]
    pad = SWA_HEAD_DIM - ROPE_DIM
    one_head = lambda first, second, fill: jnp.concatenate(
        [first, second, jnp.full((n, pad), fill, F32)], axis=1)
    zeros = jnp.zeros_like(sin)
    cos_f = one_head(cos, cos, 1.0)
    sin_a = one_head(-sin, zeros, 0.0)
    sin_b = one_head(zeros, sin, 0.0)
    reps = V7X_LANES // SWA_HEAD_DIM
    return tuple(jnp.tile(t, (1, reps)) for t in (cos_f, sin_a, sin_b))


def _pad_lanes(v, width):
    return jnp.pad(v.astype(F32), (0, width - v.shape[0]))[None, :]


def _token_tile(t, want):
    tm = min(want, t)
    while t % tm:
        tm //= 2
    return tm


def _branch(x, pos, conv_buf, s0, k_buf, v_buf, p, chunk):
    b, l, _ = x.shape
    t = b * l
    x2d = x.reshape(t, D_MODEL)
    x1 = _ffn(x2d, p['norm_ffn1'], p['w_ffn1_in'], p['w_ffn1_out'], _token_tile(t, 512))
    qkv, z, ab, qs, ks, vs, ga, gb = _inproj(x1, p['norm_mix'], p['w_in_parts'], _token_tile(t, 512))

    tables = _rope_tables(pos)
    if conv_buf is None:
        tl = _token_tile(l, 256)
        post_args = dict(tl=tl, group=tl, chunk=chunk, halo_is_prev_rows=True)
        shape3 = lambda a: a.reshape(b, l, a.shape[-1])
        halo = shape3(qkv)
    else:
        tl = _token_tile(t, 256)
        post_args = dict(tl=tl, group=l, chunk=chunk, halo_is_prev_rows=False)
        shape3 = lambda a: a.reshape(1, t, a.shape[-1])
        halo = jnp.pad(conv_buf, ((0, 0), (l - (CONV_W - 1), 0), (0, 0))).reshape(1, t, CONV_DIM)
        tables = tuple(jnp.tile(tb, (tl // l, 1)) for tb in tables)
    qg, kg, vg, qr, kr, gcb = _post(shape3(qkv), halo, shape3(qs), shape3(ks), shape3(ab),
                                    p['conv_w'], p['a_log_pad'], p['dt_bias_pad'], tables, **post_args)
    seq = lambda a: a.reshape(b, l, a.shape[-1])
    qg, kg, vg, qr, kr, gcb = (seq(a) for a in (qg, kg, vg, qr, kr, gcb))
    new_conv = qkv.reshape(b, l, CONV_DIM)[:, l - (CONV_W - 1):]

    n = l // chunk
    grow = gcb[:, :, :GDN_HEADS].reshape(b, n, chunk, GDN_HEADS).transpose(0, 1, 3, 2)
    if chunk == PK_C and s0 is None:
        nc = _token_tile(n, 4)
        u, w, qgd, kgd, qk, gl = _pk_pre(qg, kg, vg, gcb, grow.reshape(b, n, 1, PK_CAT), nc)
        og, new_s = _pk_scan(u, w, qgd, kgd, qk, gl)
    elif n == 1 and s0 is not None:
        gc = gcb[:, :, :GDN_HEADS]
        og, new_s = _gdn_sample(qg, kg, vg, gcb, jnp.repeat(gc, l, axis=-1),
                                grow.reshape(b, 1, GDN_HEADS * l), s0, _token_tile(b, 8))
    else:
        u, w, qgd, kgd, qk, gl = _gdn_pre(qg, kg, vg, gcb, grow, chunk)
        og, new_s = _gdn_scan(u, w, qgd, kgd, qk, gl, s0, chunk)

    vs3 = vs.reshape(b, l, SWA_KV_DIM)
    if k_buf is None:
        sink_cols = jnp.repeat(p['sinks'].reshape(SWA_Q_HEADS // 2, 2), WINDOW, axis=1)
        os_ = _swa_prompt(qr, kr, vs3, sink_cols.reshape(SWA_Q_HEADS // 2, 1, 2 * WINDOW))
        new_k, new_v = kr[:, l - WINDOW:], vs3[:, l - WINDOW:]
    else:
        rows = SWA_GROUP * l
        sink_rows = jnp.repeat(p['sinks'], l, axis=1).reshape(SWA_KV_HEADS, rows, 1)
        nrows = k_buf.shape[1]
        os_, new_k, new_v = _swa_sample(qr, kr, vs3, k_buf.reshape(b, nrows, SWA_KV_DIM),
                                        v_buf.reshape(b, nrows, SWA_KV_DIM), sink_rows, _token_tile(b, 16))
    y = _out(x1, og.reshape(t, GDN_V_DIM), z, os_.reshape(t, SWA_Q_DIM), ga, gb, p['gdn_norm'], p['w_out'],
             p['norm_ffn2'], p['w_ffn2_in'], p['w_ffn2_out'], p['norm_final'], _token_tile(t, 256))
    kv_shape = (b, new_k.shape[1], SWA_KV_HEADS, SWA_HEAD_DIM)
    return (y.reshape(b, l, D_MODEL), new_conv, new_s, new_k.reshape(kv_shape), new_v.reshape(kv_shape))


def _layer_params(l, norm_ffn1, w_ffn1_in, w_ffn1_out, norm_mix, w_in, conv_w, gdn_a_log, gdn_dt_bias,
                  gdn_norm, swa_sinks, w_out, norm_ffn2, w_ffn2_in, w_ffn2_out, norm_final):
    bounds = [0]
    for s in IN_SPLITS:
        bounds.append(bounds[-1] + s)
    col = lambda i: w_in[l][:, bounds[i]:bounds[i + 1]]
    w_ab = jnp.pad(jnp.concatenate([col(2), col(3)], axis=1), ((0, 0), (0, V7X_LANES - 2 * GDN_HEADS)))
    parts = [col(0), col(1), w_ab, col(4), col(5), col(6), col(7), col(8)]
    row = lambda v: v.astype(F32)[None, :]
    return {
        'norm_ffn1': row(norm_ffn1[l]), 'w_ffn1_in': w_ffn1_in[l].astype(BF16),
        'w_ffn1_out': w_ffn1_out[l].astype(BF16),
        'norm_mix': row(norm_mix[l]), 'w_in_parts': [w.astype(BF16) for w in parts],
        'conv_w': conv_w[l].astype(F32),
        'a_log_pad': _pad_lanes(gdn_a_log[l], V7X_LANES), 'dt_bias_pad': _pad_lanes(gdn_dt_bias[l], V7X_LANES),
        'gdn_norm': row(gdn_norm[l]),
        'sinks': swa_sinks[l].astype(F32).reshape(SWA_KV_HEADS, SWA_GROUP),
        'w_out': w_out[l].astype(BF16),
        'norm_ffn2': row(norm_ffn2[l]), 'w_ffn2_in': w_ffn2_in[l].astype(BF16),
        'w_ffn2_out': w_ffn2_out[l].astype(BF16),
        'norm_final': row(norm_final),
    }


def kernel(x_prompt, x_sample, state_conv, state_gdn, cache_swa_k, cache_swa_v, norm_ffn1, w_ffn1_in, w_ffn1_out, norm_mix, w_in, conv_w, gdn_a_log, gdn_dt_bias, gdn_norm, swa_sinks, w_out, norm_ffn2, w_ffn2_in, w_ffn2_out, norm_final):
    depth = w_in.shape[0]
    assert depth == 1, "the final norm is fused into the layer's last kernel"
    l_p = x_prompt.shape[1]
    l_s = x_sample.shape[1]
    pos_p = jnp.arange(l_p, dtype=jnp.int32)
    pos_s = PAST_LEN + jnp.arange(l_s, dtype=jnp.int32)
    p = _layer_params(0, norm_ffn1, w_ffn1_in, w_ffn1_out, norm_mix, w_in, conv_w, gdn_a_log, gdn_dt_bias,
                      gdn_norm, swa_sinks, w_out, norm_ffn2, w_ffn2_in, w_ffn2_out, norm_final)
    yp, c1, s1, k1, v1 = _branch(x_prompt, pos_p, None, None, None, None, p, min(GDN_CHUNK, l_p))
    ys, c2, s2, k2, v2 = _branch(x_sample, pos_s, state_conv[0], state_gdn[0], cache_swa_k[0],
                                 cache_swa_v[0], p, min(GDN_CHUNK, l_s))
    return (yp, ys, c1[None], c2[None], s1[None], s2[None], k1[None], k2[None], v1[None], v2[None])
```

```python
import functools

import jax
import jax.numpy as jnp
from jax import lax
from jax.experimental import pallas as pl
from jax.experimental.pallas import tpu as pltpu

D_MODEL = 1024
PAST_LEN = 16384
GDN_HEADS = 8
GDN_DK = 128
GDN_DV = 128
GDN_QK_DIM = GDN_HEADS * GDN_DK
GDN_V_DIM = GDN_HEADS * GDN_DV
CONV_W = 4
GDN_CHUNK = 64
CONV_DIM = 2 * GDN_QK_DIM + GDN_V_DIM
SWA_Q_HEADS = 16
SWA_KV_HEADS = 4
SWA_GROUP = SWA_Q_HEADS // SWA_KV_HEADS
SWA_HEAD_DIM = 64
SWA_Q_DIM = SWA_Q_HEADS * SWA_HEAD_DIM
SWA_KV_DIM = SWA_KV_HEADS * SWA_HEAD_DIM
WINDOW = 128
ROPE_DIM = SWA_HEAD_DIM // 4
ROPE_HALF = ROPE_DIM // 2
ROPE_THETA = 500000.0
D_FF = 2816
EPS = 1e-6
IN_SPLITS = (CONV_DIM, GDN_V_DIM, GDN_HEADS, GDN_HEADS, SWA_Q_DIM, SWA_KV_DIM, SWA_KV_DIM, D_MODEL, D_MODEL)

V7X_LANES = 128
V7X_SUBLANES = 8
V7X_MXU_DIM = 256
V7X_VMEM_BYTES = 64 * 1024 * 1024

TOKEN_TILE = 2 * V7X_MXU_DIM
FFN_CHUNK = V7X_MXU_DIM
SOLVE_BASE = V7X_SUBLANES

F32 = jnp.float32
BF16 = jnp.bfloat16


def _dot(a, b):
    return jnp.dot(a.astype(BF16), b.astype(BF16), preferred_element_type=F32)


def _dot_nt(a, b):
    return lax.dot_general(a.astype(BF16), b.astype(BF16), (((1,), (1,)), ((), ())),
                           preferred_element_type=F32)


def _dot_tn(a, b):
    return lax.dot_general(a.astype(BF16), b.astype(BF16), (((0,), (0,)), ((), ())),
                           preferred_element_type=F32)


def _rms(x, w):
    return x * lax.rsqrt(jnp.mean(x * x, axis=-1, keepdims=True) + EPS) * w


def _silu(x):
    return x * jax.nn.sigmoid(x)


def _vmem_limit(nbytes):
    return int(min(V7X_VMEM_BYTES - (4 << 20), max(nbytes, 16 << 20)))


def _const_spec(shape):
    nd = len(shape)
    return pl.BlockSpec(shape, lambda *_: (0,) * nd, pipeline_mode=pl.Buffered(1))


def _token_tile(t, want):
    tm = min(want, t)
    while t % tm:
        tm //= 2
    return tm


def _swiglu_half_step(x, norm_w, win_ref, wout_ref):
    xn = _rms(x, norm_w).astype(BF16)
    acc = None
    for c in range(D_FF // FFN_CHUNK):
        lo = c * FFN_CHUNK
        g = jnp.dot(xn, win_ref[:, lo:lo + FFN_CHUNK], preferred_element_type=F32)
        u = jnp.dot(xn, win_ref[:, D_FF + lo:D_FF + lo + FFN_CHUNK], preferred_element_type=F32)
        act = (_silu(g) * u).astype(BF16)
        part = jnp.dot(act, wout_ref[lo:lo + FFN_CHUNK, :], preferred_element_type=F32)
        acc = part if acc is None else acc + part
    return x + 0.5 * acc


def _ffn_kernel(x_ref, nw_ref, win_ref, wout_ref, o_ref):
    o_ref[...] = _swiglu_half_step(x_ref[...], nw_ref[...], win_ref, wout_ref)


def _ffn(x, norm_w, win, wout, tm):
    t = x.shape[0]
    wbytes = (win.size + wout.size) * 2
    return pl.pallas_call(
        _ffn_kernel,
        grid=(t // tm,),
        in_specs=[pl.BlockSpec((tm, D_MODEL), lambda i: (i, 0)),
                  _const_spec((1, D_MODEL)),
                  _const_spec(win.shape),
                  _const_spec(wout.shape)],
        out_specs=pl.BlockSpec((tm, D_MODEL), lambda i: (i, 0)),
        out_shape=jax.ShapeDtypeStruct((t, D_MODEL), F32),
        compiler_params=pltpu.CompilerParams(
            dimension_semantics=("parallel",),
            vmem_limit_bytes=_vmem_limit(wbytes + 12 * tm * D_MODEL * 4 + (8 << 20))),
        name="ffn1",
    )(x, norm_w, win, wout)


def _mix_a_kernel(*refs, tiles_per_seq, group, chunk):
    carried = tiles_per_seq > 0
    if carried:
        (x_ref, nw_ref, wqkv_ref, wab_ref, wz_ref, wga_ref, cw_ref, alog_ref, dtb_ref,
         qg_ref, kg_ref, vg_ref, gcb_ref, zga_ref, tail_ref, carry_ref) = refs
    else:
        (x_ref, nw_ref, wqkv_ref, wab_ref, wz_ref, wga_ref, cw_ref, alog_ref, dtb_ref, halo_ref,
         qg_ref, kg_ref, vg_ref, gcb_ref, zga_ref, tail_ref) = refs
    rows = x_ref.shape[0]
    xn = _rms(x_ref[...], nw_ref[...]).astype(BF16)
    if carried:
        seq_start = (pl.program_id(0) % tiles_per_seq) == 0

        @pl.when(pl.program_id(0) == 0)
        def _():
            carry_ref[...] = jnp.zeros_like(carry_ref)
    else:
        pos = lax.broadcasted_iota(jnp.int32, (rows, 1), 0) % group

    for part, out_ref in enumerate((qg_ref, kg_ref, vg_ref)):
        cols = slice(part * GDN_QK_DIM, (part + 1) * GDN_QK_DIM)
        raw = jnp.dot(xn, wqkv_ref[:, cols], preferred_element_type=F32)
        cw = cw_ref[:, cols]
        acc = raw * cw[CONV_W - 1:CONV_W, :]
        if carried:
            halo = jnp.where(seq_start, 0.0, carry_ref[:, cols])
            ext = jnp.concatenate([halo, raw], axis=0)
            for s in range(1, CONV_W):
                acc = acc + pltpu.roll(ext, s, 0)[V7X_SUBLANES:] * cw[CONV_W - 1 - s:CONV_W - s, :]
            carry_ref[:, cols] = raw[rows - V7X_SUBLANES:]
            tail_ref[0, :, cols] = raw[rows - V7X_SUBLANES:]
        else:
            halo = halo_ref[:, cols]
            for s in range(1, CONV_W):
                shifted = jnp.where(pos >= s, pltpu.roll(raw, s, 0),
                                    pltpu.roll(halo, (rows + s - group) % rows, 0))
                acc = acc + shifted * cw[CONV_W - 1 - s:CONV_W - s, :]
            tail_ref[:, cols] = raw
        conv = _silu(acc)
        if part == 2:
            out_ref[...] = conv
        else:
            for h in range(GDN_HEADS):
                lo = h * GDN_DK
                ch = conv[:, lo:lo + GDN_DK]
                out_ref[:, lo:lo + GDN_DK] = ch * lax.rsqrt(jnp.sum(ch * ch, axis=-1, keepdims=True) + EPS)

    ab = jnp.dot(xn, wab_ref[...], preferred_element_type=F32)
    sp_in = ab + dtb_ref[...]
    softplus = jnp.maximum(sp_in, 0.0) + jnp.log1p(jnp.exp(-jnp.abs(sp_in)))
    gc = -jnp.exp(alog_ref[...]) * softplus
    cpos = lax.broadcasted_iota(jnp.int32, (rows, 1), 0) % chunk
    s = 1
    while s < chunk:
        gc = gc + jnp.where(cpos >= s, pltpu.roll(gc, s, 0), 0.0)
        s *= 2
    lane = lax.broadcasted_iota(jnp.int32, (1, V7X_LANES), 1)
    gcb_ref[...] = jnp.where(lane < GDN_HEADS, gc, jax.nn.sigmoid(ab))

    z = jnp.dot(xn, wz_ref[...], preferred_element_type=F32)
    ga = jnp.dot(xn, wga_ref[...], preferred_element_type=F32)
    zga_ref[...] = _silu(z) * jax.nn.sigmoid(ga)


def _mix_a(x, norm_w, wqkv, wab, wz, wga, cw, alog, dtb, halo, *, tm, seq_len, group, chunk):
    t = x.shape[0]
    carried = halo is None
    tiles_per_seq = seq_len // tm if carried else 0
    row = lambda n: pl.BlockSpec((tm, n), lambda i: (i, 0))
    weights = [wqkv, wab, wz, wga]
    in_specs = ([row(D_MODEL), _const_spec((1, D_MODEL))] + [_const_spec(w.shape) for w in weights]
                + [_const_spec(cw.shape), _const_spec(alog.shape), _const_spec(dtb.shape)])
    args = [x, norm_w, *weights, cw, alog, dtb]
    out_specs = [row(GDN_QK_DIM), row(GDN_QK_DIM), row(GDN_V_DIM), row(V7X_LANES), row(GDN_V_DIM)]
    out_shape = [jax.ShapeDtypeStruct((t, n), F32)
                 for n in (GDN_QK_DIM, GDN_QK_DIM, GDN_V_DIM, V7X_LANES, GDN_V_DIM)]
    scratch = []
    if carried:
        nseq = t // seq_len
        out_specs.append(pl.BlockSpec((1, V7X_SUBLANES, CONV_DIM), lambda i: (i // tiles_per_seq, 0, 0)))
        out_shape.append(jax.ShapeDtypeStruct((nseq, V7X_SUBLANES, CONV_DIM), F32))
        scratch.append(pltpu.VMEM((V7X_SUBLANES, CONV_DIM), F32))
    else:
        in_specs.append(row(CONV_DIM))
        args.append(halo)
        out_specs.append(row(CONV_DIM))
        out_shape.append(jax.ShapeDtypeStruct((t, CONV_DIM), F32))
    wbytes = sum(w.size for w in weights) * 2
    kern = functools.partial(_mix_a_kernel, tiles_per_seq=tiles_per_seq, group=group, chunk=chunk)
    return pl.pallas_call(
        kern,
        grid=(t // tm,),
        in_specs=in_specs,
        out_specs=out_specs,
        out_shape=out_shape,
        scratch_shapes=scratch,
        compiler_params=pltpu.CompilerParams(
            dimension_semantics=("arbitrary",),
            vmem_limit_bytes=_vmem_limit(wbytes + 40 * tm * D_MODEL * 4 + (4 << 20))),
        name="mix_a",
    )(*args)


def _rope(x, cos_f, sin_a, sin_b):
    width = x.shape[1]
    reps = width // V7X_LANES
    tile = lambda t: jnp.concatenate([t] * reps, axis=1) if reps > 1 else t
    return (x * tile(cos_f) + pltpu.roll(x, width - ROPE_HALF, 1) * tile(sin_a)
            + pltpu.roll(x, ROPE_HALF, 1) * tile(sin_b))


def _mix_b_kernel(x_ref, nw_ref, wqs_ref, wks_ref, wvs_ref, wgb_ref, cos_ref, sina_ref, sinb_ref,
                  qr_ref, kr_ref, vs_ref, sgb_ref):
    xn = _rms(x_ref[...], nw_ref[...]).astype(BF16)
    cos_f, sin_a, sin_b = cos_ref[...], sina_ref[...], sinb_ref[...]
    qr_ref[...] = _rope(jnp.dot(xn, wqs_ref[...], preferred_element_type=F32), cos_f, sin_a, sin_b)
    kr_ref[...] = _rope(jnp.dot(xn, wks_ref[...], preferred_element_type=F32), cos_f, sin_a, sin_b)
    vs_ref[...] = jnp.dot(xn, wvs_ref[...], preferred_element_type=F32)
    sgb_ref[...] = jax.nn.sigmoid(jnp.dot(xn, wgb_ref[...], preferred_element_type=F32))


def _mix_b(x, norm_w, wqs, wks, wvs, wgb, tables, *, tm, table_tiles):
    t = x.shape[0]
    row = lambda n: pl.BlockSpec((tm, n), lambda i: (i, 0))
    tab = pl.BlockSpec((tm, V7X_LANES), lambda i: (i % table_tiles, 0))
    weights = [wqs, wks, wvs, wgb]
    widths = (SWA_Q_DIM, SWA_KV_DIM, SWA_KV_DIM, D_MODEL)
    wbytes = sum(w.size for w in weights) * 2
    return pl.pallas_call(
        _mix_b_kernel,
        grid=(t // tm,),
        in_specs=[row(D_MODEL), _const_spec((1, D_MODEL))] + [_const_spec(w.shape) for w in weights]
                 + [tab, tab, tab],
        out_specs=[row(n) for n in widths],
        out_shape=[jax.ShapeDtypeStruct((t, n), F32) for n in widths],
        compiler_params=pltpu.CompilerParams(
            dimension_semantics=("parallel",),
            vmem_limit_bytes=_vmem_limit(wbytes + 24 * tm * D_MODEL * 4 + (4 << 20))),
        name="mix_b",
    )(x, norm_w, *weights, *tables)


PK_C = GDN_CHUNK
PK_CAT = GDN_HEADS * PK_C
PK_PAIR = 2 * GDN_DK
PK_NPAIR = GDN_HEADS // 2
PK_GROUP = V7X_MXU_DIM // PK_C


def _pk_masks():
    c = PK_C
    i = lax.broadcasted_iota(jnp.int32, (c, PK_CAT), 0)
    j = lax.broadcasted_iota(jnp.int32, (c, PK_CAT), 1) % c
    r = lax.broadcasted_iota(jnp.int32, (V7X_MXU_DIM, V7X_MXU_DIM), 0) // c
    l = lax.broadcasted_iota(jnp.int32, (V7X_MXU_DIM, V7X_MXU_DIM), 1) // c
    return i, j, (r == l).astype(BF16)


def _pk_prod(x, y, bd_mask):
    outs = []
    for g in range(GDN_HEADS // PK_GROUP):
        sl = slice(g * V7X_MXU_DIM, (g + 1) * V7X_MXU_DIM)
        yb = y[:, sl].astype(BF16)
        bd = jnp.concatenate([yb] * PK_GROUP, axis=0) * bd_mask
        outs.append(jnp.dot(x[:, sl].astype(BF16), bd, preferred_element_type=F32))
    return jnp.concatenate(outs, axis=1)


def _pk_solve(a_list, i, j, bd_mask):
    prod_all = lambda xs, ys: [_pk_prod(x, y, bd_mask) for x, y in zip(xs, ys)]
    eye = (i == j).astype(F32)
    b = SOLVE_BASE
    same = (i // b) == (j // b)
    p = [jnp.where(same, a, 0.0) for a in a_list]
    d = [eye - x for x in p]
    span = 2
    while span < b:
        p = prod_all(p, p)
        d = [x + y for x, y in zip(d, prod_all(d, p))]
        span *= 2
    while b < PK_C:
        same2 = (i // (2 * b)) == (j // (2 * b))
        sel = same2 & jnp.logical_not(same)
        off = [jnp.where(sel, a, 0.0) for a in a_list]
        d = [x - y for x, y in zip(d, prod_all(d, prod_all(off, d)))]
        same = same2
        b *= 2
    return d


def _pk_pre_kernel(q_ref, k_ref, v_ref, gcb_ref, grow_ref,
                   u_ref, w_ref, qg_ref, kg_ref, qk_ref, gl_ref, *, nc):
    c = PK_C
    i, j, bd_mask = _pk_masks()
    lane_lo = lax.broadcasted_iota(jnp.int32, (c, V7X_LANES), 1) < c
    zeros_k = jnp.zeros((c, GDN_DK), BF16)
    a_list, vb_list, kbg_list = [], [], []
    for ch in range(nc):
        rows = slice(ch * c, (ch + 1) * c)
        gcb = gcb_ref[0, rows, :]
        g_end = gcb[c - 1:c, :]
        eg_s = jnp.exp(gcb)
        ek_s = jnp.exp(g_end - gcb)
        col = lambda arr, n: jnp.broadcast_to(arr[:, n:n + 1], (c, V7X_LANES))
        full = lambda arr, base: jnp.concatenate([col(arr, base + h) for h in range(GDN_HEADS)], axis=1)
        gc_cols = [col(gcb, h) for h in range(GDN_HEADS)]
        beta_f = full(gcb, GDN_HEADS)
        eg_f = full(eg_s, 0)
        ek_f = full(ek_s, 0)
        gc_cat = jnp.concatenate([jnp.where(lane_lo, gc_cols[2 * m], gc_cols[2 * m + 1])
                                  for m in range(PK_NPAIR)], axis=1)
        decay = jnp.exp(jnp.where(i >= j, gc_cat - grow_ref[0, ch], -jnp.inf))
        q = q_ref[0, rows, :] * (GDN_DK ** -0.5)
        k = k_ref[0, rows, :]
        kb = k * beta_f
        vb_list.append((v_ref[0, rows, :] * beta_f).astype(BF16))
        kbg_list.append((kb * eg_f).astype(BF16))
        qg_ref[0, rows, :] = (q * eg_f).astype(BF16)
        kg_ref[0, rows, :] = (k * ek_f).astype(BF16)
        gl_ref[0, ch] = eg_f[c - 1:c, :]
        kk = []
        for m in range(PK_NPAIR):
            sl = slice(m * PK_PAIR, (m + 1) * PK_PAIR)
            lhs = jnp.concatenate([kb[:, sl], q[:, sl]], axis=0).astype(BF16)
            k2 = k[:, sl].astype(BF16)
            rhs_nt = jnp.concatenate(
                [jnp.concatenate([k2[:, :GDN_DK], zeros_k], axis=1),
                 jnp.concatenate([zeros_k, k2[:, GDN_DK:]], axis=1)], axis=0)
            kk.append(lax.dot_general(lhs, rhs_nt, (((1,), (1,)), ((), ())), preferred_element_type=F32))
        a_raw = jnp.concatenate([x[:c] for x in kk], axis=1)
        qk_raw = jnp.concatenate([x[c:] for x in kk], axis=1)
        a_list.append(jnp.where(i > j, a_raw * decay, 0.0))
        qk_ref[0, rows, :] = (qk_raw * decay).astype(BF16)
    t_list = _pk_solve(a_list, i, j, bd_mask)
    for ch in range(nc):
        rows = slice(ch * c, (ch + 1) * c)
        t = t_list[ch]
        for m in range(PK_NPAIR):
            tp = t[:, m * V7X_LANES:(m + 1) * V7X_LANES]
            lhs = jnp.concatenate([jnp.where(lane_lo, tp, 0.0), jnp.where(lane_lo, 0.0, tp)],
                                  axis=0).astype(BF16)
            h0, h1 = 2 * m, 2 * m + 1
            hs = lambda h: slice(h * GDN_DK, (h + 1) * GDN_DK)
            rhs = jnp.concatenate(
                [jnp.concatenate([vb_list[ch][:, hs(h0)], kbg_list[ch][:, hs(h0)]], axis=1),
                 jnp.concatenate([vb_list[ch][:, hs(h1)], kbg_list[ch][:, hs(h1)]], axis=1)], axis=0)
            uw = jnp.dot(lhs, rhs, preferred_element_type=F32)
            u_ref[0, rows, hs(h0)] = uw[:c, :GDN_DV]
            w_ref[0, rows, hs(h0)] = uw[:c, GDN_DV:].astype(BF16)
            u_ref[0, rows, hs(h1)] = uw[c:, :GDN_DV]
            w_ref[0, rows, hs(h1)] = uw[c:, GDN_DV:].astype(BF16)


def _pk_pre(q, k, v, gcb, grow_flat, nc):
    b, l, _ = q.shape
    tl = nc * PK_C
    tok = lambda wd: pl.BlockSpec((1, tl, wd), lambda bi, ti: (bi, ti, 0))
    per_chunk = lambda wd: pl.BlockSpec((1, nc, 1, wd), lambda bi, ti: (bi, ti, 0, 0))
    kern = functools.partial(_pk_pre_kernel, nc=nc)
    n = l // PK_C
    return pl.pallas_call(
        kern,
        grid=(b, l // tl),
        in_specs=[tok(GDN_QK_DIM), tok(GDN_QK_DIM), tok(GDN_V_DIM), tok(V7X_LANES), per_chunk(PK_CAT)],
        out_specs=[tok(GDN_V_DIM), tok(GDN_QK_DIM), tok(GDN_QK_DIM), tok(GDN_QK_DIM), tok(PK_CAT),
                   per_chunk(GDN_V_DIM)],
        out_shape=[jax.ShapeDtypeStruct((b, l, GDN_V_DIM), F32),
                   jax.ShapeDtypeStruct((b, l, GDN_QK_DIM), BF16),
                   jax.ShapeDtypeStruct((b, l, GDN_QK_DIM), BF16),
                   jax.ShapeDtypeStruct((b, l, GDN_QK_DIM), BF16),
                   jax.ShapeDtypeStruct((b, l, PK_CAT), BF16),
                   jax.ShapeDtypeStruct((b, n, 1, GDN_V_DIM), F32)],
        compiler_params=pltpu.CompilerParams(dimension_semantics=("parallel", "parallel"),
                                             vmem_limit_bytes=_vmem_limit(48 << 20)),
        name="gdn_pre_packed",
    )(q, k, v, gcb, grow_flat)


def _pk_scan_kernel(u_ref, w_ref, qg_ref, kg_ref, qk_ref, gl_ref, o_ref, sout_ref, s_ref):
    c = PK_C
    nb = u_ref.shape[0]
    step = pl.program_id(0)

    @pl.when(step == 0)
    def _():
        s_ref[...] = jnp.zeros_like(s_ref)

    zeros_s = jnp.zeros((GDN_DK, GDN_DV), BF16)
    zeros_v = jnp.zeros((c, GDN_DV), BF16)
    chains = [(b, m) for b in range(nb) for m in range(PK_NPAIR)]
    pair = lambda m: slice(m * PK_PAIR, (m + 1) * PK_PAIR)
    s_old, wq = {}, {}
    for b, m in chains:
        s_c = s_ref[b, m]
        s_old[b, m] = s_c
        sb = s_c.astype(BF16)
        bd_s = jnp.concatenate([jnp.concatenate([sb[:, :GDN_DV], zeros_s], axis=1),
                                jnp.concatenate([zeros_s, sb[:, GDN_DV:]], axis=1)], axis=0)
        lhs = jnp.concatenate([w_ref[b, :, pair(m)], qg_ref[b, :, pair(m)]], axis=0)
        wq[b, m] = jnp.dot(lhs, bd_s, preferred_element_type=F32)
    bd_v = {}
    for b, m in chains:
        v_new = (u_ref[b, :, pair(m)] - wq[b, m][:c]).astype(BF16)
        bd_v[b, m] = jnp.concatenate([jnp.concatenate([v_new[:, :GDN_DV], zeros_v], axis=1),
                                      jnp.concatenate([zeros_v, v_new[:, GDN_DV:]], axis=1)], axis=0)
    for b, m in chains:
        qk_p = qk_ref[b, :, m * V7X_LANES:(m + 1) * V7X_LANES]
        o_ref[b, :, pair(m)] = wq[b, m][c:] + jnp.dot(qk_p, bd_v[b, m], preferred_element_type=F32)
        kg_p = kg_ref[b, :, pair(m)]
        kg_st = jnp.concatenate([kg_p[:, :GDN_DK], kg_p[:, GDN_DK:]], axis=0)
        upd = lax.dot_general(kg_st, bd_v[b, m], (((0,), (0,)), ((), ())), preferred_element_type=F32)
        s_ref[b, m] = s_old[b, m] * gl_ref[b, 0, :, pair(m)] + upd

    @pl.when(step == pl.num_programs(0) - 1)
    def _():
        for b, m in chains:
            s_c = s_ref[b, m]
            sout_ref[b, 2 * m] = s_c[:, :GDN_DV]
            sout_ref[b, 2 * m + 1] = s_c[:, GDN_DV:]


def _pk_scan(u, w, qg, kg, qk, gl):
    b, l, _ = u.shape
    n = l // PK_C
    tok = lambda wd: pl.BlockSpec((b, PK_C, wd), lambda s: (0, s, 0))
    return pl.pallas_call(
        _pk_scan_kernel,
        grid=(n,),
        in_specs=[tok(GDN_V_DIM), tok(GDN_QK_DIM), tok(GDN_QK_DIM), tok(GDN_QK_DIM), tok(PK_CAT),
                  pl.BlockSpec((b, 1, 1, GDN_V_DIM), lambda s: (0, s, 0, 0))],
        out_specs=[tok(GDN_V_DIM),
                   pl.BlockSpec((b, GDN_HEADS, GDN_DK, GDN_DV), lambda s: (0, 0, 0, 0))],
        out_shape=[jax.ShapeDtypeStruct((b, l, GDN_V_DIM), F32),
                   jax.ShapeDtypeStruct((b, GDN_HEADS, GDN_DK, GDN_DV), F32)],
        scratch_shapes=[pltpu.VMEM((b, PK_NPAIR, GDN_DK, 2 * GDN_DV), F32)],
        compiler_params=pltpu.CompilerParams(dimension_semantics=("arbitrary",)),
        name="gdn_scan_packed",
    )(u, w, qg, kg, qk, gl)


def _head_block_rows(x, t):
    width = x.shape[1] // GDN_HEADS
    z = jnp.zeros((t, width), x.dtype)
    rows = []
    for h in range(GDN_HEADS):
        rows.append(jnp.concatenate([z] * h + [x[:, h * width:(h + 1) * width]]
                                    + [z] * (GDN_HEADS - 1 - h), axis=1))
    return jnp.concatenate(rows, axis=0)


def _gdn_sample_kernel(q_ref, k_ref, v_ref, gcb_ref, grep_ref, grow_ref, s0_ref, o_ref, sout_ref):
    bt, t, _ = q_ref.shape
    cat = GDN_HEADS * t
    i = lax.broadcasted_iota(jnp.int32, (t, cat), 0)
    j = lax.broadcasted_iota(jnp.int32, (t, cat), 1) % t
    r = lax.broadcasted_iota(jnp.int32, (cat, cat), 0) // t
    l = lax.broadcasted_iota(jnp.int32, (cat, cat), 1) // t
    same_head = r == l
    eye = (i == j).astype(F32)
    seqs = range(bt)
    pair = lambda m: slice(m * PK_PAIR, (m + 1) * PK_PAIR)

    vb, kbg, qg, kg, gl, a, qk = {}, {}, {}, {}, {}, {}, {}
    for b in seqs:
        gcb = gcb_ref[b]
        g_end = gcb[t - 1:t, :]
        eg_s = jnp.exp(gcb)
        ek_s = jnp.exp(g_end - gcb)
        col = lambda arr, n: jnp.broadcast_to(arr[:, n:n + 1], (t, V7X_LANES))
        full = lambda arr, base: jnp.concatenate([col(arr, base + h) for h in range(GDN_HEADS)], axis=1)
        beta_f, eg_f, ek_f = full(gcb, GDN_HEADS), full(eg_s, 0), full(ek_s, 0)
        decay = jnp.exp(jnp.where(i >= j, grep_ref[b] - grow_ref[b], -jnp.inf))
        q = q_ref[b] * (GDN_DK ** -0.5)
        k = k_ref[b]
        kb = k * beta_f
        vb[b] = v_ref[b] * beta_f
        kbg[b] = kb * eg_f
        qg[b] = q * eg_f
        kg[b] = k * ek_f
        gl[b] = eg_f[t - 1:t, :]
        kk = _dot_nt(jnp.concatenate([kb, q], axis=0), _head_block_rows(k, t))
        a[b] = jnp.where(i > j, kk[:t] * decay, 0.0)
        qk[b] = kk[t:] * decay

    def prod_all(xs, ys):
        return {b: _dot(xs[b], jnp.where(same_head, jnp.concatenate([ys[b]] * GDN_HEADS, axis=0), 0.0))
                for b in seqs}

    d = {b: eye - a[b] for b in seqs}
    p = a
    span = 2
    while span < t:
        p = prod_all(p, p)
        dp = prod_all(d, p)
        d = {b: d[b] + dp[b] for b in seqs}
        span *= 2

    u, w = {}, {}
    for b in seqs:
        rhs = jnp.concatenate([_head_block_rows(vb[b], t), _head_block_rows(kbg[b], t)], axis=1)
        uw = _dot(d[b], rhs)
        u[b], w[b] = uw[:, :GDN_V_DIM], uw[:, GDN_V_DIM:]

    zeros_s = jnp.zeros((GDN_DK, GDN_DV), BF16)
    wq = {}
    for b in seqs:
        for m in range(PK_NPAIR):
            s0 = s0_ref[b, 2 * m].astype(BF16)
            s1 = s0_ref[b, 2 * m + 1].astype(BF16)
            bd_s = jnp.concatenate([jnp.concatenate([s0, zeros_s], axis=1),
                                    jnp.concatenate([zeros_s, s1], axis=1)], axis=0)
            lhs = jnp.concatenate([w[b][:, pair(m)], qg[b][:, pair(m)]], axis=0).astype(BF16)
            wq[b, m] = jnp.dot(lhs, bd_s, preferred_element_type=F32)
    v_new = {b: u[b] - jnp.concatenate([wq[b, m][:t] for m in range(PK_NPAIR)], axis=1) for b in seqs}
    zeros_v = jnp.zeros((t, GDN_DV), F32)
    for b in seqs:
        o_ref[b] = (jnp.concatenate([wq[b, m][t:] for m in range(PK_NPAIR)], axis=1)
                    + _dot(qk[b], _head_block_rows(v_new[b], t)))
        for m in range(PK_NPAIR):
            vp = v_new[b][:, pair(m)]
            bd_v = jnp.concatenate([jnp.concatenate([vp[:, :GDN_DV], zeros_v], axis=1),
                                    jnp.concatenate([zeros_v, vp[:, GDN_DV:]], axis=1)], axis=0)
            kp = kg[b][:, pair(m)]
            kg_st = jnp.concatenate([kp[:, :GDN_DK], kp[:, GDN_DK:]], axis=0)
            upd = _dot_tn(kg_st, bd_v)
            glp = gl[b][:, pair(m)]
            sout_ref[b, 2 * m] = s0_ref[b, 2 * m] * glp[:, :GDN_DV] + upd[:, :GDN_DV]
            sout_ref[b, 2 * m + 1] = s0_ref[b, 2 * m + 1] * glp[:, GDN_DV:] + upd[:, GDN_DV:]


def _gdn_sample(q, k, v, gcb, grep, grow_flat, s0, bt):
    b, t, _ = q.shape
    assert t == V7X_SUBLANES, "one sublane tile of steps per sequence"
    cat = GDN_HEADS * t
    tok = lambda wd: pl.BlockSpec((bt, t, wd), lambda i: (i, 0, 0))
    state = pl.BlockSpec((bt, GDN_HEADS, GDN_DK, GDN_DV), lambda i: (i, 0, 0, 0))
    return pl.pallas_call(
        _gdn_sample_kernel,
        grid=(b // bt,),
        in_specs=[tok(GDN_QK_DIM), tok(GDN_QK_DIM), tok(GDN_V_DIM), tok(V7X_LANES), tok(cat),
                  pl.BlockSpec((bt, 1, cat), lambda i: (i, 0, 0)), state],
        out_specs=[tok(GDN_V_DIM), state],
        out_shape=[jax.ShapeDtypeStruct((b, t, GDN_V_DIM), F32),
                   jax.ShapeDtypeStruct((b, GDN_HEADS, GDN_DK, GDN_DV), F32)],
        compiler_params=pltpu.CompilerParams(dimension_semantics=("parallel",),
                                             vmem_limit_bytes=_vmem_limit(40 << 20)),
        name="gdn_sample",
    )(q, k, v, gcb, grep, grow_flat, s0)


def _swa_prompt_kernel(q_ref, kc_ref, kp_ref, vc_ref, vp_ref, sgb_ref, sink_ref, o_ref):
    w = WINDOW
    hd = SWA_HEAD_DIM
    nkeys = 2 * w
    ncols = 2 * w
    has_prev = pl.program_id(1) > 0
    kj = lax.broadcasted_iota(jnp.int32, (nkeys, ncols), 0)
    qi = lax.broadcasted_iota(jnp.int32, (nkeys, ncols), 1) % w
    mask = ((kj >= w) & ((kj - w) <= qi)) | ((kj < w) & (kj > qi) & has_prev)
    lo_keys = lax.broadcasted_iota(jnp.int32, (nkeys, V7X_LANES), 1) < hd
    lo_q = lax.broadcasted_iota(jnp.int32, (w, V7X_LANES), 1) < hd
    scale = hd ** -0.5
    pairs = [(h, gp) for h in range(SWA_KV_HEADS) for gp in range(SWA_GROUP // 2)]

    kk, vbd = {}, {}
    for hp in range(SWA_KV_HEADS // 2):
        sl = slice(hp * V7X_LANES, (hp + 1) * V7X_LANES)
        kt = jnp.concatenate([kp_ref[0, :, sl], kc_ref[0, :, sl]], axis=0)
        vt = jnp.concatenate([vp_ref[0, :, sl], vc_ref[0, :, sl]], axis=0)
        kr = pltpu.roll(kt, hd, 1)
        vr = pltpu.roll(vt, hd, 1)
        for par, h in enumerate((2 * hp, 2 * hp + 1)):
            kk[h] = (jnp.where(lo_keys, kt, kr) if par == 0 else jnp.where(lo_keys, kr, kt)).astype(BF16)
            vv = jnp.where(lo_keys, vt, vr) if par == 0 else jnp.where(lo_keys, vr, vt)
            vbd[h] = jnp.concatenate([jnp.where(lo_keys, vv, 0.0), jnp.where(lo_keys, 0.0, vv)],
                                     axis=0).astype(BF16)
    s = {}
    for h, gp in pairs:
        lo = (h * SWA_GROUP + 2 * gp) * hd
        qp = q_ref[0, :, lo:lo + V7X_LANES]
        rq = jnp.concatenate([jnp.where(lo_q, qp, 0.0), jnp.where(lo_q, 0.0, qp)], axis=0).astype(BF16)
        sc = lax.dot_general(kk[h], rq, (((1,), (1,)), ((), ())), preferred_element_type=F32)
        s[h, gp] = jnp.where(mask, sc * scale, -jnp.inf)
    pn = {}
    for h, gp in pairs:
        sink = sink_ref[h * (SWA_GROUP // 2) + gp]
        m = jnp.maximum(jnp.max(s[h, gp], axis=0, keepdims=True), sink)
        p = jnp.exp(s[h, gp] - m)
        denom = jnp.sum(p, axis=0, keepdims=True) + jnp.exp(sink - m)
        p = (p * (1.0 / denom)).astype(BF16)
        pn[h, gp] = jnp.concatenate([p[:, :w], p[:, w:]], axis=0)
    for h, gp in pairs:
        lo = (h * SWA_GROUP + 2 * gp) * hd
        o = lax.dot_general(pn[h, gp], vbd[h], (((0,), (0,)), ((), ())), preferred_element_type=F32)
        o_ref[0, :, lo:lo + V7X_LANES] = o * sgb_ref[0, :, lo:lo + V7X_LANES]


def _swa_prompt(q, k, v, sgb, sink_cols):
    b, l, _ = q.shape
    nb = l // WINDOW
    cur = lambda wd: pl.BlockSpec((1, WINDOW, wd), lambda i, j: (i, j, 0))
    prev = lambda wd: pl.BlockSpec((1, WINDOW, wd), lambda i, j: (i, jnp.maximum(j - 1, 0), 0))
    return pl.pallas_call(
        _swa_prompt_kernel,
        grid=(b, nb),
        in_specs=[cur(SWA_Q_DIM), cur(SWA_KV_DIM), prev(SWA_KV_DIM), cur(SWA_KV_DIM), prev(SWA_KV_DIM),
                  cur(SWA_Q_DIM), pl.BlockSpec(sink_cols.shape, lambda i, j: (0, 0, 0))],
        out_specs=cur(SWA_Q_DIM),
        out_shape=jax.ShapeDtypeStruct((b, l, SWA_Q_DIM), F32),
        compiler_params=pltpu.CompilerParams(dimension_semantics=("parallel", "parallel")),
        name="swa_prompt",
    )(q, k, k, v, v, sgb, sink_cols)


def _swa_sample_kernel(q_ref, kn_ref, vn_ref, kb_ref, vb_ref, sgb_ref, sink_ref,
                       o_ref, ko_ref, vo_ref, *, steps):
    t = steps
    nrows = kb_ref.shape[1]
    rows = SWA_GROUP * t
    qt_b = lax.broadcasted_iota(jnp.int32, (rows, nrows), 0) % t
    ki_b = lax.broadcasted_iota(jnp.int32, (rows, nrows), 1)
    mask_buf = ((nrows - ki_b + qt_b) < WINDOW)[None]
    qt_n = lax.broadcasted_iota(jnp.int32, (rows, t), 0) % t
    ki_n = lax.broadcasted_iota(jnp.int32, (rows, t), 1)
    mask_new = (ki_n <= qt_n)[None]
    scale = SWA_HEAD_DIM ** -0.5
    bdot = lambda a, b_, dims: lax.dot_general(a.astype(BF16), b_.astype(BF16), dims,
                                               preferred_element_type=F32)
    nt_dims = (((2,), (2,)), ((0,), (0,)))
    nn_dims = (((2,), (1,)), ((0,), (0,)))
    for h in range(SWA_KV_HEADS):
        klo = h * SWA_HEAD_DIM
        q_h = jnp.concatenate(
            [q_ref[:, :, (h * SWA_GROUP + g) * SWA_HEAD_DIM:(h * SWA_GROUP + g + 1) * SWA_HEAD_DIM]
             for g in range(SWA_GROUP)], axis=1)
        sink = sink_ref[h][None]
        k_b = kb_ref[:, :, klo:klo + SWA_HEAD_DIM]
        k_n = kn_ref[:, :, klo:klo + SWA_HEAD_DIM]
        v_b = vb_ref[:, :, klo:klo + SWA_HEAD_DIM]
        v_n = vn_ref[:, :, klo:klo + SWA_HEAD_DIM]
        s_b = jnp.where(mask_buf, bdot(q_h, k_b, nt_dims) * scale, -jnp.inf)
        s_n = jnp.where(mask_new, bdot(q_h, k_n, nt_dims) * scale, -jnp.inf)
        m = jnp.maximum(jnp.maximum(jnp.max(s_b, axis=-1, keepdims=True),
                                    jnp.max(s_n, axis=-1, keepdims=True)), sink)
        p_b = jnp.exp(s_b - m)
        p_n = jnp.exp(s_n - m)
        denom = (jnp.sum(p_b, axis=-1, keepdims=True) + jnp.sum(p_n, axis=-1, keepdims=True)
                 + jnp.exp(sink - m))
        o = (bdot(p_b, v_b, nn_dims) + bdot(p_n, v_n, nn_dims)) / denom
        for g in range(SWA_GROUP):
            lo = (h * SWA_GROUP + g) * SWA_HEAD_DIM
            o_ref[:, :, lo:lo + SWA_HEAD_DIM] = (o[:, g * t:(g + 1) * t]
                                                 * sgb_ref[:, :, lo:lo + SWA_HEAD_DIM])
    ko_ref[:, :nrows - t] = kb_ref[:, t:]
    ko_ref[:, nrows - t:] = kn_ref[...]
    vo_ref[:, :nrows - t] = vb_ref[:, t:]
    vo_ref[:, nrows - t:] = vn_ref[...]


def _swa_sample(q, k, v, k_buf, v_buf, sgb, sink_rows, bt):
    b, t, _ = q.shape
    nrows = k_buf.shape[1]
    new = lambda wd: pl.BlockSpec((bt, t, wd), lambda i: (i, 0, 0))
    buf = pl.BlockSpec((bt, nrows, SWA_KV_DIM), lambda i: (i, 0, 0))
    kern = functools.partial(_swa_sample_kernel, steps=t)
    return pl.pallas_call(
        kern,
        grid=(b // bt,),
        in_specs=[new(SWA_Q_DIM), new(SWA_KV_DIM), new(SWA_KV_DIM), buf, buf, new(SWA_Q_DIM),
                  pl.BlockSpec(sink_rows.shape, lambda i: (0, 0, 0))],
        out_specs=[new(SWA_Q_DIM), buf, buf],
        out_shape=[jax.ShapeDtypeStruct((b, t, SWA_Q_DIM), F32),
                   jax.ShapeDtypeStruct((b, nrows, SWA_KV_DIM), F32),
                   jax.ShapeDtypeStruct((b, nrows, SWA_KV_DIM), F32)],
        compiler_params=pltpu.CompilerParams(dimension_semantics=("parallel",)),
        name="swa_sample",
    )(q, k, v, k_buf, v_buf, sgb, sink_rows)


def _out_kernel(x_ref, og_ref, zga_ref, osg_ref, gn_ref, wo_ref, nw2_ref, win_ref, wout_ref, nf_ref, y_ref):
    gn = gn_ref[...]
    heads = []
    for h in range(GDN_HEADS):
        lo = h * GDN_DV
        heads.append(_rms(og_ref[:, lo:lo + GDN_DV], gn))
    mixed = jnp.concatenate(heads, axis=1) * zga_ref[...] + osg_ref[...]
    x2 = x_ref[...] + jnp.dot(mixed.astype(BF16), wo_ref[...], preferred_element_type=F32)
    x3 = _swiglu_half_step(x2, nw2_ref[...], win_ref, wout_ref)
    y_ref[...] = _rms(x3, nf_ref[...])


def _out(x1, og, zga, osg, gnorm, wo, nw2, win, wout, nf, tm):
    t = x1.shape[0]
    row = pl.BlockSpec((tm, D_MODEL), lambda i: (i, 0))
    wbytes = (wo.size + win.size + wout.size) * 2
    return pl.pallas_call(
        _out_kernel,
        grid=(t // tm,),
        in_specs=[row] * 4 + [_const_spec(gnorm.shape), _const_spec(wo.shape), _const_spec(nw2.shape),
                              _const_spec(win.shape), _const_spec(wout.shape), _const_spec(nf.shape)],
        out_specs=row,
        out_shape=jax.ShapeDtypeStruct((t, D_MODEL), F32),
        compiler_params=pltpu.CompilerParams(
            dimension_semantics=("parallel",),
            vmem_limit_bytes=_vmem_limit(wbytes + 18 * tm * D_MODEL * 4 + (8 << 20))),
        name="out",
    )(x1, og, zga, osg, gnorm, wo, nw2, win, wout, nf)


def _rope_tables(pos):
    inv = ROPE_THETA ** (-jnp.arange(0, ROPE_DIM, 2, dtype=F32) / ROPE_DIM)
    ang = pos.astype(F32)[:, None] * inv[None, :]
    cos, sin = jnp.cos(ang), jnp.sin(ang)
    n = pos.shape[0]
    pad = SWA_HEAD_DIM - ROPE_DIM
    one_head = lambda first, second, fill: jnp.concatenate(
        [first, second, jnp.full((n, pad), fill, F32)], axis=1)
    zeros = jnp.zeros_like(sin)
    cos_f = one_head(cos, cos, 1.0)
    sin_a = one_head(-sin, zeros, 0.0)
    sin_b = one_head(zeros, sin, 0.0)
    reps = V7X_LANES // SWA_HEAD_DIM
    return tuple(jnp.tile(t, (1, reps)) for t in (cos_f, sin_a, sin_b))


def _pad_lanes(v, width):
    return jnp.pad(v.astype(F32), (0, width - v.shape[0]))[None, :]


def _branch(x, pos, conv_buf, s0, k_buf, v_buf, p):
    b, l, _ = x.shape
    t = b * l
    is_prompt = conv_buf is None
    chunk = min(GDN_CHUNK, l)
    n = l // chunk
    tm = _token_tile(l if is_prompt else t, TOKEN_TILE)
    x1 = _ffn(x.reshape(t, D_MODEL), p['norm_ffn1'], p['w_ffn1_in'], p['w_ffn1_out'], tm)

    tables = _rope_tables(pos)
    if is_prompt:
        halo = None
        table_tiles = l // tm
    else:
        halo = jnp.pad(conv_buf, ((0, 0), (l - (CONV_W - 1), 0), (0, 0))).reshape(t, CONV_DIM)
        tables = tuple(jnp.tile(tb, (tm // l, 1)) for tb in tables)
        table_tiles = 1
    qg, kg, vg, gcb, zga, tail = _mix_a(x1, p['norm_mix'], *p['w_mix_a'], p['conv_w'], p['a_log_pad'],
                                        p['dt_bias_pad'], halo, tm=tm, seq_len=l, group=l, chunk=chunk)
    qr, kr, vs, sgb = _mix_b(x1, p['norm_mix'], *p['w_mix_b'], tables, tm=tm, table_tiles=table_tiles)
    seq = lambda a: a.reshape(b, l, a.shape[-1])
    qg, kg, vg, gcb, qr, kr, vs, sgb = (seq(a) for a in (qg, kg, vg, gcb, qr, kr, vs, sgb))
    new_conv = tail.reshape(b, -1, CONV_DIM)[:, -(CONV_W - 1):]

    gc = gcb[:, :, :GDN_HEADS]
    grow = gc.reshape(b, n, chunk, GDN_HEADS).transpose(0, 1, 3, 2)
    if is_prompt:
        assert chunk == PK_C and s0 is None
        u, w, qgd, kgd, qk, gl = _pk_pre(qg, kg, vg, gcb, grow.reshape(b, n, 1, PK_CAT), _token_tile(n, 4))
        og, new_s = _pk_scan(u, w, qgd, kgd, qk, gl)
        sink_cols = jnp.repeat(p['sinks'].reshape(SWA_Q_HEADS // 2, 2), WINDOW, axis=1)
        osg = _swa_prompt(qr, kr, vs, sgb, sink_cols.reshape(SWA_Q_HEADS // 2, 1, 2 * WINDOW))
        new_k, new_v = kr[:, l - WINDOW:], vs[:, l - WINDOW:]
    else:
        assert n == 1
        og, new_s = _gdn_sample(qg, kg, vg, gcb, jnp.repeat(gc, l, axis=-1),
                                grow.reshape(b, 1, GDN_HEADS * l), s0, _token_tile(b, 8))
        sink_rows = jnp.repeat(p['sinks'], l, axis=1).reshape(SWA_KV_HEADS, SWA_GROUP * l, 1)
        nrows = k_buf.shape[1]
        osg, new_k, new_v = _swa_sample(qr, kr, vs, k_buf.reshape(b, nrows, SWA_KV_DIM),
                                        v_buf.reshape(b, nrows, SWA_KV_DIM), sgb, sink_rows,
                                        _token_tile(b, 16))
    y = _out(x1, og.reshape(t, GDN_V_DIM), zga, osg.reshape(t, SWA_Q_DIM), p['gdn_norm'], p['w_out'],
             p['norm_ffn2'], p['w_ffn2_in'], p['w_ffn2_out'], p['norm_final'], tm)
    kv_shape = (b, new_k.shape[1], SWA_KV_HEADS, SWA_HEAD_DIM)
    return (y.reshape(b, l, D_MODEL), new_conv, new_s, new_k.reshape(kv_shape), new_v.reshape(kv_shape))


def _layer_params(l, norm_ffn1, w_ffn1_in, w_ffn1_out, norm_mix, w_in, conv_w, gdn_a_log, gdn_dt_bias,
                  gdn_norm, swa_sinks, w_out, norm_ffn2, w_ffn2_in, w_ffn2_out, norm_final):
    bounds = [0]
    for s in IN_SPLITS:
        bounds.append(bounds[-1] + s)
    col = lambda i: w_in[l][:, bounds[i]:bounds[i + 1]].astype(BF16)
    w_ab = jnp.pad(jnp.concatenate([col(2), col(3)], axis=1), ((0, 0), (0, V7X_LANES - 2 * GDN_HEADS)))
    row = lambda v: v.astype(F32)[None, :]
    return {
        'norm_ffn1': row(norm_ffn1[l]), 'w_ffn1_in': w_ffn1_in[l].astype(BF16),
        'w_ffn1_out': w_ffn1_out[l].astype(BF16),
        'norm_mix': row(norm_mix[l]),
        'w_mix_a': [col(0), w_ab, col(1), col(7)],
        'w_mix_b': [col(4), col(5), col(6), col(8)],
        'conv_w': conv_w[l].astype(F32),
        'a_log_pad': _pad_lanes(gdn_a_log[l], V7X_LANES), 'dt_bias_pad': _pad_lanes(gdn_dt_bias[l], V7X_LANES),
        'gdn_norm': row(gdn_norm[l]),
        'sinks': swa_sinks[l].astype(F32).reshape(SWA_KV_HEADS, SWA_GROUP),
        'w_out': w_out[l].astype(BF16),
        'norm_ffn2': row(norm_ffn2[l]), 'w_ffn2_in': w_ffn2_in[l].astype(BF16),
        'w_ffn2_out': w_ffn2_out[l].astype(BF16),
        'norm_final': row(norm_final),
    }


def kernel(x_prompt, x_sample, state_conv, state_gdn, cache_swa_k, cache_swa_v, norm_ffn1, w_ffn1_in, w_ffn1_out, norm_mix, w_in, conv_w, gdn_a_log, gdn_dt_bias, gdn_norm, swa_sinks, w_out, norm_ffn2, w_ffn2_in, w_ffn2_out, norm_final):
    depth = w_in.shape[0]
    assert depth == 1, "the final norm is fused into the layer's last kernel"
    l_p = x_prompt.shape[1]
    l_s = x_sample.shape[1]
    pos_p = jnp.arange(l_p, dtype=jnp.int32)
    pos_s = PAST_LEN + jnp.arange(l_s, dtype=jnp.int32)
    p = _layer_params(0, norm_ffn1, w_ffn1_in, w_ffn1_out, norm_mix, w_in, conv_w, gdn_a_log, gdn_dt_bias,
                      gdn_norm, swa_sinks, w_out, norm_ffn2, w_ffn2_in, w_ffn2_out, norm_final)
    yp, c1, s1, k1, v1 = _branch(x_prompt, pos_p, None, None, None, None, p)
    ys, c2, s2, k2, v2 = _branch(x_sample, pos_s, state_conv[0], state_gdn[0], cache_swa_k[0],
                                 cache_swa_v[0], p)
    return (yp, ys, c1[None], c2[None], s1[None], s2[None], k1[None], k2[None], v1[None], v2[None])
```

```python
import functools

import jax
import jax.numpy as jnp
from jax import lax
from jax.experimental import pallas as pl
from jax.experimental.pallas import tpu as pltpu

D_MODEL = 1024
PAST_LEN = 16384
GDN_HEADS = 8
GDN_DK = 128
GDN_DV = 128
GDN_QK_DIM = GDN_HEADS * GDN_DK
GDN_V_DIM = GDN_HEADS * GDN_DV
CONV_W = 4
GDN_CHUNK = 64
CONV_DIM = 2 * GDN_QK_DIM + GDN_V_DIM
SWA_Q_HEADS = 16
SWA_KV_HEADS = 4
SWA_GROUP = SWA_Q_HEADS // SWA_KV_HEADS
SWA_HEAD_DIM = 64
SWA_Q_DIM = SWA_Q_HEADS * SWA_HEAD_DIM
SWA_KV_DIM = SWA_KV_HEADS * SWA_HEAD_DIM
WINDOW = 128
ROPE_DIM = SWA_HEAD_DIM // 4
ROPE_HALF = ROPE_DIM // 2
ROPE_THETA = 500000.0
D_FF = 2816
EPS = 1e-6
LOG2_E = 1.4426950408889634
IN_SPLITS = (CONV_DIM, GDN_V_DIM, GDN_HEADS, GDN_HEADS, SWA_Q_DIM, SWA_KV_DIM, SWA_KV_DIM, D_MODEL, D_MODEL)

V7X_LANES = 128
V7X_SUBLANES = 8
V7X_MXU_DIM = 256
V7X_VMEM_BYTES = 64 * 1024 * 1024

TOKEN_TILE = 2 * V7X_MXU_DIM
FFN_CHUNK = V7X_MXU_DIM
CONV_PIECE = V7X_MXU_DIM
SOLVE_BASE = V7X_SUBLANES

F32 = jnp.float32
BF16 = jnp.bfloat16


def _dot(a, b):
    return jnp.dot(a.astype(BF16), b.astype(BF16), preferred_element_type=F32)


def _dot_nt(a, b):
    return lax.dot_general(a.astype(BF16), b.astype(BF16), (((1,), (1,)), ((), ())),
                           preferred_element_type=F32)


def _dot_tn(a, b):
    return lax.dot_general(a.astype(BF16), b.astype(BF16), (((0,), (0,)), ((), ())),
                           preferred_element_type=F32)


def _proj(xn, w_t):
    return lax.dot_general(xn, w_t, (((1,), (1,)), ((), ())), preferred_element_type=F32)


def _rms(x, w):
    return x * lax.rsqrt(jnp.mean(x * x, axis=-1, keepdims=True) + EPS) * w


def _sigmoid(x):
    return 0.5 * jnp.tanh(0.5 * x) + 0.5


def _silu(x):
    h = 0.5 * x
    return h * jnp.tanh(h) + h


def _vmem_limit(nbytes):
    return int(min(V7X_VMEM_BYTES - (4 << 20), max(nbytes, 16 << 20)))


def _const_spec(shape):
    nd = len(shape)
    return pl.BlockSpec(shape, lambda *_: (0,) * nd, pipeline_mode=pl.Buffered(1))


def _token_tile(t, want):
    tm = min(want, t)
    while t % tm:
        tm //= 2
    return tm


FFN_NCHUNK = D_FF // FFN_CHUNK


def _swiglu_chunk(xn, win_ref, wout_ref, c):
    lo = c * FFN_CHUNK
    g = jnp.dot(xn, win_ref[:, lo:lo + FFN_CHUNK], preferred_element_type=F32)
    u = jnp.dot(xn, win_ref[:, D_FF + lo:D_FF + lo + FFN_CHUNK], preferred_element_type=F32)
    act = (_silu(g) * u).astype(BF16)
    return jnp.dot(act, wout_ref[lo:lo + FFN_CHUNK, :], preferred_element_type=F32)


def _swiglu_half_step(x, norm_w, win_ref, wout_ref):
    xn = _rms(x, norm_w).astype(BF16)
    acc = _swiglu_chunk(xn, win_ref, wout_ref, 0)
    for c in range(1, FFN_NCHUNK):
        acc = acc + _swiglu_chunk(xn, win_ref, wout_ref, c)
    return x + 0.5 * acc


def _ffn_conv_kernel(*refs, tiles_per_seq, group):
    carried = tiles_per_seq > 0
    if carried:
        (x_ref, nw1_ref, win_ref, wout_ref, nw_ref, wqkv_ref, cw_ref,
         x1_ref, qg_ref, kg_ref, vg_ref, tail_ref, xn_ref, carry_ref) = refs
    else:
        (x_ref, nw1_ref, win_ref, wout_ref, nw_ref, wqkv_ref, cw_ref, halo_ref,
         x1_ref, qg_ref, kg_ref, vg_ref, tail_ref, xn_ref) = refs
    rows = x_ref.shape[0]
    step = pl.program_id(0)
    n_tiles = pl.num_programs(0) - 1
    if carried:
        seq_start = ((step - 1) % tiles_per_seq) == 0
    else:
        pos = lax.broadcasted_iota(jnp.int32, (rows, 1), 0) % group

    def conv_piece(c):
        cols = slice(c * CONV_PIECE, (c + 1) * CONV_PIECE)
        raw = _proj(xn_ref[...], wqkv_ref[cols, :])
        cw = cw_ref[:, cols]
        acc = raw * cw[CONV_W - 1:CONV_W, :]
        if carried:
            halo = jnp.where(seq_start, 0.0, carry_ref[:, cols])
            ext = jnp.concatenate([halo, raw], axis=0)
            for s in range(1, CONV_W):
                acc = acc + pltpu.roll(ext, s, 0)[V7X_SUBLANES:] * cw[CONV_W - 1 - s:CONV_W - s, :]
            carry_ref[:, cols] = raw[rows - V7X_SUBLANES:]
            tail_ref[0, :, cols] = raw[rows - V7X_SUBLANES:]
        else:
            halo = halo_ref[:, cols]
            for s in range(1, CONV_W):
                shifted = jnp.where(pos >= s, pltpu.roll(raw, s, 0),
                                    pltpu.roll(halo, (rows + s - group) % rows, 0))
                acc = acc + shifted * cw[CONV_W - 1 - s:CONV_W - s, :]
            tail_ref[:, cols] = raw
        conv = _silu(acc)
        part, lo = divmod(c * CONV_PIECE, GDN_QK_DIM)
        out_ref = (qg_ref, kg_ref, vg_ref)[part]
        if part == 2:
            out_ref[:, lo:lo + CONV_PIECE] = conv.astype(BF16)
        else:
            for h in range(CONV_PIECE // GDN_DK):
                ch = conv[:, h * GDN_DK:(h + 1) * GDN_DK]
                out_ref[:, lo + h * GDN_DK:lo + (h + 1) * GDN_DK] = (
                    ch * lax.rsqrt(jnp.sum(ch * ch, axis=-1, keepdims=True) + EPS)).astype(BF16)

    def body(do_ffn, do_conv):
        if do_ffn:
            x = x_ref[...]
            xn = _rms(x, nw1_ref[...]).astype(BF16)
            acc = None
        for c in range(max(FFN_NCHUNK, CONV_DIM // CONV_PIECE)):
            if do_ffn and c < FFN_NCHUNK:
                part = _swiglu_chunk(xn, win_ref, wout_ref, c)
                acc = part if acc is None else acc + part
            if do_conv and c < CONV_DIM // CONV_PIECE:
                conv_piece(c)
        if do_ffn:
            x1 = x + 0.5 * acc
            x1_ref[...] = x1
            xn_ref[...] = _rms(x1, nw_ref[...]).astype(BF16)

    @pl.when(step == 0)
    def _():
        if carried:
            carry_ref[...] = jnp.zeros_like(carry_ref)
        body(True, False)

    @pl.when((step > 0) & (step < n_tiles))
    def _():
        body(True, True)

    @pl.when(step == n_tiles)
    def _():
        body(False, True)


def _ffn_conv(x, norm1, win, wout, norm_w, wqkv, cw, halo, *, tm, seq_len, group):
    t = x.shape[0]
    carried = halo is None
    tiles_per_seq = seq_len // tm if carried else 0
    n_tiles = t // tm
    cur = lambda n: pl.BlockSpec((tm, n), lambda i: (jnp.minimum(i, n_tiles - 1), 0))
    prev = lambda n: pl.BlockSpec((tm, n), lambda i: (jnp.maximum(i - 1, 0), 0))
    weights = [win, wout, wqkv]
    in_specs = [cur(D_MODEL), _const_spec((1, D_MODEL)), _const_spec(win.shape), _const_spec(wout.shape),
                _const_spec((1, D_MODEL)), _const_spec(wqkv.shape), _const_spec(cw.shape)]
    args = [x, norm1, win, wout, norm_w, wqkv, cw]
    out_specs = [cur(D_MODEL), prev(GDN_QK_DIM), prev(GDN_QK_DIM), prev(GDN_V_DIM)]
    out_shape = [jax.ShapeDtypeStruct((t, D_MODEL), F32)] + [
        jax.ShapeDtypeStruct((t, n), BF16) for n in (GDN_QK_DIM, GDN_QK_DIM, GDN_V_DIM)]
    scratch = [pltpu.VMEM((tm, D_MODEL), BF16)]
    if carried:
        nseq = t // seq_len
        out_specs.append(pl.BlockSpec((1, V7X_SUBLANES, CONV_DIM),
                                      lambda i: (jnp.maximum(i - 1, 0) // tiles_per_seq, 0, 0)))
        out_shape.append(jax.ShapeDtypeStruct((nseq, V7X_SUBLANES, CONV_DIM), F32))
        scratch.append(pltpu.VMEM((V7X_SUBLANES, CONV_DIM), F32))
    else:
        in_specs.append(prev(CONV_DIM))
        args.append(halo)
        out_specs.append(prev(CONV_DIM))
        out_shape.append(jax.ShapeDtypeStruct((t, CONV_DIM), F32))
    wbytes = sum(w.size for w in weights) * 2
    kern = functools.partial(_ffn_conv_kernel, tiles_per_seq=tiles_per_seq, group=group)
    return pl.pallas_call(
        kern,
        grid=(n_tiles + 1,),
        in_specs=in_specs,
        out_specs=out_specs,
        out_shape=out_shape,
        scratch_shapes=scratch,
        compiler_params=pltpu.CompilerParams(
            dimension_semantics=("arbitrary",),
            vmem_limit_bytes=_vmem_limit(wbytes + 32 * tm * D_MODEL * 4 + (4 << 20))),
        name="ffn_conv",
    )(*args)


def _rope(x, cos_f, sin_a, sin_b):
    width = x.shape[1]
    reps = width // V7X_LANES
    tile = lambda t: jnp.concatenate([t] * reps, axis=1) if reps > 1 else t
    return (x * tile(cos_f) + pltpu.roll(x, width - ROPE_HALF, 1) * tile(sin_a)
            + pltpu.roll(x, ROPE_HALF, 1) * tile(sin_b))


def _mix_kernel(x_ref, nw_ref, wab_ref, wz_ref, wga_ref, wqs_ref, wks_ref, wvs_ref, wgb_ref,
                alog_ref, dtb_ref, cos_ref, sina_ref, sinb_ref,
                gcb_ref, zga_ref, qr_ref, kr_ref, vs_ref, sgb_ref, *, chunk):
    rows = x_ref.shape[0]
    xn = _rms(x_ref[...], nw_ref[...]).astype(BF16)

    ab = _proj(xn, wab_ref[...])
    sp_in = ab + dtb_ref[...]
    softplus = jnp.maximum(sp_in, 0.0) + jnp.log1p(jnp.exp(-jnp.abs(sp_in)))
    gc = -jnp.exp(alog_ref[...]) * softplus
    cpos = lax.broadcasted_iota(jnp.int32, (rows, 1), 0) % chunk
    s = 1
    while s < chunk:
        gc = gc + jnp.where(cpos >= s, pltpu.roll(gc, s, 0), 0.0)
        s *= 2
    lane = lax.broadcasted_iota(jnp.int32, (1, V7X_LANES), 1)
    gcb_ref[...] = jnp.where(lane < GDN_HEADS, gc, jax.nn.sigmoid(ab))

    zga_ref[...] = _silu(_proj(xn, wz_ref[...])) * _sigmoid(_proj(xn, wga_ref[...]))
    cos_f, sin_a, sin_b = cos_ref[...], sina_ref[...], sinb_ref[...]
    qr_ref[...] = _rope(_proj(xn, wqs_ref[...]), cos_f, sin_a, sin_b)
    kr_ref[...] = _rope(_proj(xn, wks_ref[...]), cos_f, sin_a, sin_b)
    vs_ref[...] = _proj(xn, wvs_ref[...])
    sgb_ref[...] = _sigmoid(_proj(xn, wgb_ref[...]))


def _mix(x, norm_w, weights, alog, dtb, tables, *, tm, table_tiles, chunk):
    t = x.shape[0]
    row = lambda n: pl.BlockSpec((tm, n), lambda i: (i, 0))
    tab = pl.BlockSpec((tm, V7X_LANES), lambda i: (i % table_tiles, 0))
    widths = (V7X_LANES, GDN_V_DIM, SWA_Q_DIM, SWA_KV_DIM, SWA_KV_DIM, D_MODEL)
    wbytes = sum(w.size for w in weights) * 2
    return pl.pallas_call(
        functools.partial(_mix_kernel, chunk=chunk),
        grid=(t // tm,),
        in_specs=[row(D_MODEL), _const_spec((1, D_MODEL))] + [_const_spec(w.shape) for w in weights]
                 + [_const_spec(alog.shape), _const_spec(dtb.shape), tab, tab, tab],
        out_specs=[row(n) for n in widths],
        out_shape=[jax.ShapeDtypeStruct((t, n), F32) for n in widths],
        compiler_params=pltpu.CompilerParams(
            dimension_semantics=("parallel",),
            vmem_limit_bytes=_vmem_limit(wbytes + 32 * tm * D_MODEL * 4 + (4 << 20))),
        name="mix",
    )(x, norm_w, *weights, alog, dtb, *tables)


PK_C = GDN_CHUNK
PK_CAT = GDN_HEADS * PK_C
PK_PAIR = 2 * GDN_DK
PK_NPAIR = GDN_HEADS // 2
PK_GROUP = V7X_MXU_DIM // PK_C


def _pk_masks():
    c = PK_C
    i = lax.broadcasted_iota(jnp.int32, (c, PK_CAT), 0)
    j = lax.broadcasted_iota(jnp.int32, (c, PK_CAT), 1) % c
    lo = (lax.broadcasted_iota(jnp.int32, (c, V7X_LANES), 1) < c).astype(BF16)
    return i, j, (lo, 1 - lo)


def _pk_prod(x, y, bd_mask):
    lo, hi = bd_mask
    zero = jnp.zeros((PK_C, V7X_LANES), BF16)
    outs = []
    for g in range(GDN_HEADS // PK_GROUP):
        sl = slice(g * V7X_MXU_DIM, (g + 1) * V7X_MXU_DIM)
        yb = y[:, sl].astype(BF16)
        t0, t1 = yb[:, :V7X_LANES], yb[:, V7X_LANES:]
        bd = jnp.concatenate([jnp.concatenate([t0 * lo, zero], axis=1),
                              jnp.concatenate([t0 * hi, zero], axis=1),
                              jnp.concatenate([zero, t1 * lo], axis=1),
                              jnp.concatenate([zero, t1 * hi], axis=1)], axis=0)
        outs.append(jnp.dot(x[:, sl].astype(BF16), bd, preferred_element_type=F32))
    return jnp.concatenate(outs, axis=1)


def _pk_solve(a_list, i, j, bd_mask):
    prod_all = lambda xs, ys: [_pk_prod(x, y, bd_mask) for x, y in zip(xs, ys)]
    eye = (i == j).astype(F32)
    b = SOLVE_BASE
    same = (i // b) == (j // b)
    p = [jnp.where(same, a, 0.0) for a in a_list]
    d = [eye - x for x in p]
    span = 2
    while span < b:
        p = prod_all(p, p)
        d = [x + y for x, y in zip(d, prod_all(d, p))]
        span *= 2
    while b < PK_C:
        same2 = (i // (2 * b)) == (j // (2 * b))
        sel = same2 & jnp.logical_not(same)
        off = [jnp.where(sel, a, 0.0) for a in a_list]
        d = [x - y for x, y in zip(d, prod_all(d, prod_all(off, d)))]
        same = same2
        b *= 2
    return d


def _pk_pre_kernel(q_ref, k_ref, v_ref, gcb_ref, grow_ref,
                   u_ref, w_ref, qg_ref, kg_ref, qk_ref, gl_ref, *, nc):
    c = PK_C
    i, j, bd_mask = _pk_masks()
    lane_lo = lax.broadcasted_iota(jnp.int32, (c, V7X_LANES), 1) < c
    zeros_k = jnp.zeros((c, GDN_DK), BF16)
    a_list, vb_list, kbg_list = [], [], []
    for ch in range(nc):
        rows = slice(ch * c, (ch + 1) * c)
        gcb = gcb_ref[0, rows, :]
        g_end = gcb[c - 1:c, :]
        eg_s = jnp.exp(gcb)
        ek_s = jnp.exp(g_end - gcb)
        col = lambda arr, n: jnp.broadcast_to(arr[:, n:n + 1], (c, V7X_LANES))
        full = lambda arr, base: jnp.concatenate([col(arr, base + h) for h in range(GDN_HEADS)], axis=1)
        gc_cols = [col(gcb, h) for h in range(GDN_HEADS)]
        beta_f = full(gcb, GDN_HEADS)
        eg_f = full(eg_s, 0)
        ek_f = full(ek_s, 0)
        gc_cat = jnp.concatenate([jnp.where(lane_lo, gc_cols[2 * m], gc_cols[2 * m + 1])
                                  for m in range(PK_NPAIR)], axis=1)
        decay = jnp.exp(jnp.where(i >= j, gc_cat - grow_ref[0, ch], -jnp.inf))
        q = q_ref[0, rows, :].astype(F32) * (GDN_DK ** -0.5)
        k = k_ref[0, rows, :].astype(F32)
        kb = k * beta_f
        vb_list.append((v_ref[0, rows, :].astype(F32) * beta_f).astype(BF16))
        kbg_list.append((kb * eg_f).astype(BF16))
        qg_ref[0, rows, :] = (q * eg_f).astype(BF16)
        kg_ref[0, rows, :] = (k * ek_f).astype(BF16)
        gl_ref[0, ch] = eg_f[c - 1:c, :]
        kk = []
        for m in range(PK_NPAIR):
            sl = slice(m * PK_PAIR, (m + 1) * PK_PAIR)
            lhs = jnp.concatenate([kb[:, sl], q[:, sl]], axis=0).astype(BF16)
            k2 = k[:, sl].astype(BF16)
            rhs_nt = jnp.concatenate(
                [jnp.concatenate([k2[:, :GDN_DK], zeros_k], axis=1),
                 jnp.concatenate([zeros_k, k2[:, GDN_DK:]], axis=1)], axis=0)
            kk.append(lax.dot_general(lhs, rhs_nt, (((1,), (1,)), ((), ())), preferred_element_type=F32))
        a_raw = jnp.concatenate([x[:c] for x in kk], axis=1)
        qk_raw = jnp.concatenate([x[c:] for x in kk], axis=1)
        a_list.append(jnp.where(i > j, a_raw * decay, 0.0))
        qk_ref[0, rows, :] = (qk_raw * decay).astype(BF16)
    t_list = _pk_solve(a_list, i, j, bd_mask)
    for ch in range(nc):
        rows = slice(ch * c, (ch + 1) * c)
        t = t_list[ch]
        for m in range(PK_NPAIR):
            tp = t[:, m * V7X_LANES:(m + 1) * V7X_LANES]
            lhs = jnp.concatenate([jnp.where(lane_lo, tp, 0.0), jnp.where(lane_lo, 0.0, tp)],
                                  axis=0).astype(BF16)
            h0, h1 = 2 * m, 2 * m + 1
            hs = lambda h: slice(h * GDN_DK, (h + 1) * GDN_DK)
            rhs = jnp.concatenate(
                [jnp.concatenate([vb_list[ch][:, hs(h0)], kbg_list[ch][:, hs(h0)]], axis=1),
                 jnp.concatenate([vb_list[ch][:, hs(h1)], kbg_list[ch][:, hs(h1)]], axis=1)], axis=0)
            uw = jnp.dot(lhs, rhs, preferred_element_type=F32)
            u_ref[0, rows, hs(h0)] = uw[:c, :GDN_DV]
            w_ref[0, rows, hs(h0)] = uw[:c, GDN_DV:].astype(BF16)
            u_ref[0, rows, hs(h1)] = uw[c:, :GDN_DV]
            w_ref[0, rows, hs(h1)] = uw[c:, GDN_DV:].astype(BF16)


def _pk_pre(q, k, v, gcb, grow_flat, nc):
    b, l, _ = q.shape
    tl = nc * PK_C
    tok = lambda wd: pl.BlockSpec((1, tl, wd), lambda bi, ti: (bi, ti, 0))
    per_chunk = lambda wd: pl.BlockSpec((1, nc, 1, wd), lambda bi, ti: (bi, ti, 0, 0))
    kern = functools.partial(_pk_pre_kernel, nc=nc)
    n = l // PK_C
    return pl.pallas_call(
        kern,
        grid=(b, l // tl),
        in_specs=[tok(GDN_QK_DIM), tok(GDN_QK_DIM), tok(GDN_V_DIM), tok(V7X_LANES), per_chunk(PK_CAT)],
        out_specs=[tok(GDN_V_DIM), tok(GDN_QK_DIM), tok(GDN_QK_DIM), tok(GDN_QK_DIM), tok(PK_CAT),
                   per_chunk(GDN_V_DIM)],
        out_shape=[jax.ShapeDtypeStruct((b, l, GDN_V_DIM), F32),
                   jax.ShapeDtypeStruct((b, l, GDN_QK_DIM), BF16),
                   jax.ShapeDtypeStruct((b, l, GDN_QK_DIM), BF16),
                   jax.ShapeDtypeStruct((b, l, GDN_QK_DIM), BF16),
                   jax.ShapeDtypeStruct((b, l, PK_CAT), BF16),
                   jax.ShapeDtypeStruct((b, n, 1, GDN_V_DIM), F32)],
        compiler_params=pltpu.CompilerParams(dimension_semantics=("parallel", "parallel"),
                                             vmem_limit_bytes=_vmem_limit(48 << 20)),
        name="gdn_pre_packed",
    )(q, k, v, gcb, grow_flat)


def _pk_scan_kernel(u_ref, w_ref, qg_ref, kg_ref, qk_ref, gl_ref, o_ref, sout_ref, s_ref):
    c = PK_C
    nb = u_ref.shape[0]
    step = pl.program_id(0)

    @pl.when(step == 0)
    def _():
        s_ref[...] = jnp.zeros_like(s_ref)

    zeros_s = jnp.zeros((GDN_DK, GDN_DV), BF16)
    zeros_v = jnp.zeros((c, GDN_DV), BF16)
    chains = [(b, m) for b in range(nb) for m in range(PK_NPAIR)]
    pair = lambda m: slice(m * PK_PAIR, (m + 1) * PK_PAIR)
    s_old, wq = {}, {}
    for b, m in chains:
        s_c = s_ref[b, m]
        s_old[b, m] = s_c
        sb = s_c.astype(BF16)
        bd_s = jnp.concatenate([jnp.concatenate([sb[:, :GDN_DV], zeros_s], axis=1),
                                jnp.concatenate([zeros_s, sb[:, GDN_DV:]], axis=1)], axis=0)
        lhs = jnp.concatenate([w_ref[b, :, pair(m)], qg_ref[b, :, pair(m)]], axis=0)
        wq[b, m] = jnp.dot(lhs, bd_s, preferred_element_type=F32)
    bd_v = {}
    for b, m in chains:
        v_new = (u_ref[b, :, pair(m)] - wq[b, m][:c]).astype(BF16)
        bd_v[b, m] = jnp.concatenate([jnp.concatenate([v_new[:, :GDN_DV], zeros_v], axis=1),
                                      jnp.concatenate([zeros_v, v_new[:, GDN_DV:]], axis=1)], axis=0)
    for b, m in chains:
        qk_p = qk_ref[b, :, m * V7X_LANES:(m + 1) * V7X_LANES]
        o_ref[b, :, pair(m)] = wq[b, m][c:] + jnp.dot(qk_p, bd_v[b, m], preferred_element_type=F32)
        kg_p = kg_ref[b, :, pair(m)]
        kg_st = jnp.concatenate([kg_p[:, :GDN_DK], kg_p[:, GDN_DK:]], axis=0)
        upd = lax.dot_general(kg_st, bd_v[b, m], (((0,), (0,)), ((), ())), preferred_element_type=F32)
        s_ref[b, m] = s_old[b, m] * gl_ref[b, 0, :, pair(m)] + upd

    @pl.when(step == pl.num_programs(0) - 1)
    def _():
        for b, m in chains:
            s_c = s_ref[b, m]
            sout_ref[b, 2 * m] = s_c[:, :GDN_DV]
            sout_ref[b, 2 * m + 1] = s_c[:, GDN_DV:]


def _pk_scan(u, w, qg, kg, qk, gl):
    b, l, _ = u.shape
    n = l // PK_C
    tok = lambda wd: pl.BlockSpec((b, PK_C, wd), lambda s: (0, s, 0))
    return pl.pallas_call(
        _pk_scan_kernel,
        grid=(n,),
        in_specs=[tok(GDN_V_DIM), tok(GDN_QK_DIM), tok(GDN_QK_DIM), tok(GDN_QK_DIM), tok(PK_CAT),
                  pl.BlockSpec((b, 1, 1, GDN_V_DIM), lambda s: (0, s, 0, 0))],
        out_specs=[tok(GDN_V_DIM),
                   pl.BlockSpec((b, GDN_HEADS, GDN_DK, GDN_DV), lambda s: (0, 0, 0, 0))],
        out_shape=[jax.ShapeDtypeStruct((b, l, GDN_V_DIM), F32),
                   jax.ShapeDtypeStruct((b, GDN_HEADS, GDN_DK, GDN_DV), F32)],
        scratch_shapes=[pltpu.VMEM((b, PK_NPAIR, GDN_DK, 2 * GDN_DV), F32)],
        compiler_params=pltpu.CompilerParams(dimension_semantics=("arbitrary",)),
        name="gdn_scan_packed",
    )(u, w, qg, kg, qk, gl)


def _head_block_rows(x, t):
    width = x.shape[1] // GDN_HEADS
    z = jnp.zeros((t, width), x.dtype)
    rows = []
    for h in range(GDN_HEADS):
        rows.append(jnp.concatenate([z] * h + [x[:, h * width:(h + 1) * width]]
                                    + [z] * (GDN_HEADS - 1 - h), axis=1))
    return jnp.concatenate(rows, axis=0)


def _gdn_sample_kernel(q_ref, k_ref, v_ref, gcb_ref, grep_ref, grow_ref, s0_ref, o_ref, sout_ref):
    bt, t, _ = q_ref.shape
    cat = GDN_HEADS * t
    i = lax.broadcasted_iota(jnp.int32, (t, cat), 0)
    j = lax.broadcasted_iota(jnp.int32, (t, cat), 1) % t
    r = lax.broadcasted_iota(jnp.int32, (cat, cat), 0) // t
    l = lax.broadcasted_iota(jnp.int32, (cat, cat), 1) // t
    same_head = r == l
    eye = (i == j).astype(F32)
    seqs = range(bt)
    pair = lambda m: slice(m * PK_PAIR, (m + 1) * PK_PAIR)

    vb, kbg, qg, kg, gl, a, qk = {}, {}, {}, {}, {}, {}, {}
    for b in seqs:
        gcb = gcb_ref[b]
        g_end = gcb[t - 1:t, :]
        eg_s = jnp.exp(gcb)
        ek_s = jnp.exp(g_end - gcb)
        col = lambda arr, n: jnp.broadcast_to(arr[:, n:n + 1], (t, V7X_LANES))
        full = lambda arr, base: jnp.concatenate([col(arr, base + h) for h in range(GDN_HEADS)], axis=1)
        beta_f, eg_f, ek_f = full(gcb, GDN_HEADS), full(eg_s, 0), full(ek_s, 0)
        decay = jnp.exp(jnp.where(i >= j, grep_ref[b] - grow_ref[b], -jnp.inf))
        q = q_ref[b].astype(F32) * (GDN_DK ** -0.5)
        k = k_ref[b].astype(F32)
        kb = k * beta_f
        vb[b] = v_ref[b].astype(F32) * beta_f
        kbg[b] = kb * eg_f
        qg[b] = q * eg_f
        kg[b] = k * ek_f
        gl[b] = eg_f[t - 1:t, :]
        kk = _dot_nt(jnp.concatenate([kb, q], axis=0), _head_block_rows(k, t))
        a[b] = jnp.where(i > j, kk[:t] * decay, 0.0)
        qk[b] = kk[t:] * decay

    def prod_all(xs, ys):
        return {b: _dot(xs[b], jnp.where(same_head, jnp.concatenate([ys[b]] * GDN_HEADS, axis=0), 0.0))
                for b in seqs}

    d = {b: eye - a[b] for b in seqs}
    p = a
    span = 2
    while span < t:
        p = prod_all(p, p)
        dp = prod_all(d, p)
        d = {b: d[b] + dp[b] for b in seqs}
        span *= 2

    u, w = {}, {}
    for b in seqs:
        rhs = jnp.concatenate([_head_block_rows(vb[b], t), _head_block_rows(kbg[b], t)], axis=1)
        uw = _dot(d[b], rhs)
        u[b], w[b] = uw[:, :GDN_V_DIM], uw[:, GDN_V_DIM:]

    zeros_s = jnp.zeros((GDN_DK, GDN_DV), BF16)
    wq = {}
    for b in seqs:
        for m in range(PK_NPAIR):
            s0 = s0_ref[b, 2 * m].astype(BF16)
            s1 = s0_ref[b, 2 * m + 1].astype(BF16)
            bd_s = jnp.concatenate([jnp.concatenate([s0, zeros_s], axis=1),
                                    jnp.concatenate([zeros_s, s1], axis=1)], axis=0)
            lhs = jnp.concatenate([w[b][:, pair(m)], qg[b][:, pair(m)]], axis=0).astype(BF16)
            wq[b, m] = jnp.dot(lhs, bd_s, preferred_element_type=F32)
    v_new = {b: u[b] - jnp.concatenate([wq[b, m][:t] for m in range(PK_NPAIR)], axis=1) for b in seqs}
    zeros_v = jnp.zeros((t, GDN_DV), F32)
    for b in seqs:
        o_ref[b] = (jnp.concatenate([wq[b, m][t:] for m in range(PK_NPAIR)], axis=1)
                    + _dot(qk[b], _head_block_rows(v_new[b], t)))
        for m in range(PK_NPAIR):
            vp = v_new[b][:, pair(m)]
            bd_v = jnp.concatenate([jnp.concatenate([vp[:, :GDN_DV], zeros_v], axis=1),
                                    jnp.concatenate([zeros_v, vp[:, GDN_DV:]], axis=1)], axis=0)
            kp = kg[b][:, pair(m)]
            kg_st = jnp.concatenate([kp[:, :GDN_DK], kp[:, GDN_DK:]], axis=0)
            upd = _dot_tn(kg_st, bd_v)
            glp = gl[b][:, pair(m)]
            sout_ref[b, 2 * m] = s0_ref[b, 2 * m] * glp[:, :GDN_DV] + upd[:, :GDN_DV]
            sout_ref[b, 2 * m + 1] = s0_ref[b, 2 * m + 1] * glp[:, GDN_DV:] + upd[:, GDN_DV:]


def _gdn_sample(q, k, v, gcb, grep, grow_flat, s0, bt):
    b, t, _ = q.shape
    assert t == V7X_SUBLANES, "one sublane tile of steps per sequence"
    cat = GDN_HEADS * t
    tok = lambda wd: pl.BlockSpec((bt, t, wd), lambda i: (i, 0, 0))
    state = pl.BlockSpec((bt, GDN_HEADS, GDN_DK, GDN_DV), lambda i: (i, 0, 0, 0))
    return pl.pallas_call(
        _gdn_sample_kernel,
        grid=(b // bt,),
        in_specs=[tok(GDN_QK_DIM), tok(GDN_QK_DIM), tok(GDN_V_DIM), tok(V7X_LANES), tok(cat),
                  pl.BlockSpec((bt, 1, cat), lambda i: (i, 0, 0)), state],
        out_specs=[tok(GDN_V_DIM), state],
        out_shape=[jax.ShapeDtypeStruct((b, t, GDN_V_DIM), F32),
                   jax.ShapeDtypeStruct((b, GDN_HEADS, GDN_DK, GDN_DV), F32)],
        compiler_params=pltpu.CompilerParams(dimension_semantics=("parallel",),
                                             vmem_limit_bytes=_vmem_limit(40 << 20)),
        name="gdn_sample",
    )(q, k, v, gcb, grep, grow_flat, s0)


def _swa_prompt_kernel(q_ref, kc_ref, kp_ref, vc_ref, vp_ref, sgb_ref, sink_ref, o_ref):
    w = WINDOW
    hd = SWA_HEAD_DIM
    nkeys = 2 * w
    ncols = 2 * w
    has_prev = pl.program_id(1) > 0
    kj = lax.broadcasted_iota(jnp.int32, (nkeys, ncols), 0)
    qi = lax.broadcasted_iota(jnp.int32, (nkeys, ncols), 1) % w
    mask = ((kj >= w) & ((kj - w) <= qi)) | ((kj < w) & (kj > qi) & has_prev)
    lo_keys = lax.broadcasted_iota(jnp.int32, (nkeys, V7X_LANES), 1) < hd
    lo_q = lax.broadcasted_iota(jnp.int32, (w, V7X_LANES), 1) < hd
    scale = hd ** -0.5 * LOG2_E
    pairs = [(h, gp) for h in range(SWA_KV_HEADS) for gp in range(SWA_GROUP // 2)]

    kk, vbd = {}, {}
    for hp in range(SWA_KV_HEADS // 2):
        sl = slice(hp * V7X_LANES, (hp + 1) * V7X_LANES)
        kt = jnp.concatenate([kp_ref[0, :, sl], kc_ref[0, :, sl]], axis=0)
        vt = jnp.concatenate([vp_ref[0, :, sl], vc_ref[0, :, sl]], axis=0)
        kr = pltpu.roll(kt, hd, 1)
        vr = pltpu.roll(vt, hd, 1)
        for par, h in enumerate((2 * hp, 2 * hp + 1)):
            kk[h] = (jnp.where(lo_keys, kt, kr) if par == 0 else jnp.where(lo_keys, kr, kt)).astype(BF16)
            vv = jnp.where(lo_keys, vt, vr) if par == 0 else jnp.where(lo_keys, vr, vt)
            vbd[h] = jnp.concatenate([jnp.where(lo_keys, vv, 0.0), jnp.where(lo_keys, 0.0, vv)],
                                     axis=0).astype(BF16)
    s = {}
    for h, gp in pairs:
        lo = (h * SWA_GROUP + 2 * gp) * hd
        qp = q_ref[0, :, lo:lo + V7X_LANES]
        rq = jnp.concatenate([jnp.where(lo_q, qp, 0.0), jnp.where(lo_q, 0.0, qp)], axis=0).astype(BF16)
        sc = lax.dot_general(kk[h], rq, (((1,), (1,)), ((), ())), preferred_element_type=F32)
        s[h, gp] = jnp.where(mask, sc * scale, -jnp.inf)
    pn = {}
    for h, gp in pairs:
        sink = sink_ref[h * (SWA_GROUP // 2) + gp] * LOG2_E
        m = jnp.maximum(jnp.max(s[h, gp], axis=0, keepdims=True), sink)
        p = jnp.exp2(s[h, gp] - m)
        denom = jnp.sum(p, axis=0, keepdims=True) + jnp.exp2(sink - m)
        p = (p * (1.0 / denom)).astype(BF16)
        pn[h, gp] = jnp.concatenate([p[:, :w], p[:, w:]], axis=0)
    for h, gp in pairs:
        lo = (h * SWA_GROUP + 2 * gp) * hd
        o = lax.dot_general(pn[h, gp], vbd[h], (((0,), (0,)), ((), ())), preferred_element_type=F32)
        o_ref[0, :, lo:lo + V7X_LANES] = o * sgb_ref[0, :, lo:lo + V7X_LANES]


def _swa_prompt(q, k, v, sgb, sink_cols):
    b, l, _ = q.shape
    nb = l // WINDOW
    cur = lambda wd: pl.BlockSpec((1, WINDOW, wd), lambda i, j: (i, j, 0))
    prev = lambda wd: pl.BlockSpec((1, WINDOW, wd), lambda i, j: (i, jnp.maximum(j - 1, 0), 0))
    return pl.pallas_call(
        _swa_prompt_kernel,
        grid=(b, nb),
        in_specs=[cur(SWA_Q_DIM), cur(SWA_KV_DIM), prev(SWA_KV_DIM), cur(SWA_KV_DIM), prev(SWA_KV_DIM),
                  cur(SWA_Q_DIM), pl.BlockSpec(sink_cols.shape, lambda i, j: (0, 0, 0))],
        out_specs=cur(SWA_Q_DIM),
        out_shape=jax.ShapeDtypeStruct((b, l, SWA_Q_DIM), F32),
        compiler_params=pltpu.CompilerParams(dimension_semantics=("parallel", "parallel")),
        name="swa_prompt",
    )(q, k, k, v, v, sgb, sink_cols)


def _swa_sample_kernel(q_ref, kn_ref, vn_ref, kb_ref, vb_ref, sgb_ref, sink_ref,
                       o_ref, ko_ref, vo_ref, *, steps):
    t = steps
    nrows = kb_ref.shape[1]
    rows = SWA_GROUP * t
    qt_b = lax.broadcasted_iota(jnp.int32, (rows, nrows), 0) % t
    ki_b = lax.broadcasted_iota(jnp.int32, (rows, nrows), 1)
    mask_buf = ((nrows - ki_b + qt_b) < WINDOW)[None]
    qt_n = lax.broadcasted_iota(jnp.int32, (rows, t), 0) % t
    ki_n = lax.broadcasted_iota(jnp.int32, (rows, t), 1)
    mask_new = (ki_n <= qt_n)[None]
    scale = SWA_HEAD_DIM ** -0.5
    bdot = lambda a, b_, dims: lax.dot_general(a.astype(BF16), b_.astype(BF16), dims,
                                               preferred_element_type=F32)
    nt_dims = (((2,), (2,)), ((0,), (0,)))
    nn_dims = (((2,), (1,)), ((0,), (0,)))
    for h in range(SWA_KV_HEADS):
        klo = h * SWA_HEAD_DIM
        q_h = jnp.concatenate(
            [q_ref[:, :, (h * SWA_GROUP + g) * SWA_HEAD_DIM:(h * SWA_GROUP + g + 1) * SWA_HEAD_DIM]
             for g in range(SWA_GROUP)], axis=1)
        sink = sink_ref[h][None]
        k_b = kb_ref[:, :, klo:klo + SWA_HEAD_DIM]
        k_n = kn_ref[:, :, klo:klo + SWA_HEAD_DIM]
        v_b = vb_ref[:, :, klo:klo + SWA_HEAD_DIM]
        v_n = vn_ref[:, :, klo:klo + SWA_HEAD_DIM]
        s_b = jnp.where(mask_buf, bdot(q_h, k_b, nt_dims) * scale, -jnp.inf)
        s_n = jnp.where(mask_new, bdot(q_h, k_n, nt_dims) * scale, -jnp.inf)
        m = jnp.maximum(jnp.maximum(jnp.max(s_b, axis=-1, keepdims=True),
                                    jnp.max(s_n, axis=-1, keepdims=True)), sink)
        p_b = jnp.exp(s_b - m)
        p_n = jnp.exp(s_n - m)
        denom = (jnp.sum(p_b, axis=-1, keepdims=True) + jnp.sum(p_n, axis=-1, keepdims=True)
                 + jnp.exp(sink - m))
        o = (bdot(p_b, v_b, nn_dims) + bdot(p_n, v_n, nn_dims)) / denom
        for g in range(SWA_GROUP):
            lo = (h * SWA_GROUP + g) * SWA_HEAD_DIM
            o_ref[:, :, lo:lo + SWA_HEAD_DIM] = (o[:, g * t:(g + 1) * t]
                                                 * sgb_ref[:, :, lo:lo + SWA_HEAD_DIM])
    ko_ref[:, :nrows - t] = kb_ref[:, t:]
    ko_ref[:, nrows - t:] = kn_ref[...]
    vo_ref[:, :nrows - t] = vb_ref[:, t:]
    vo_ref[:, nrows - t:] = vn_ref[...]


def _swa_sample(q, k, v, k_buf, v_buf, sgb, sink_rows, bt):
    b, t, _ = q.shape
    nrows = k_buf.shape[1]
    new = lambda wd: pl.BlockSpec((bt, t, wd), lambda i: (i, 0, 0))
    buf = pl.BlockSpec((bt, nrows, SWA_KV_DIM), lambda i: (i, 0, 0))
    kern = functools.partial(_swa_sample_kernel, steps=t)
    return pl.pallas_call(
        kern,
        grid=(b // bt,),
        in_specs=[new(SWA_Q_DIM), new(SWA_KV_DIM), new(SWA_KV_DIM), buf, buf, new(SWA_Q_DIM),
                  pl.BlockSpec(sink_rows.shape, lambda i: (0, 0, 0))],
        out_specs=[new(SWA_Q_DIM), buf, buf],
        out_shape=[jax.ShapeDtypeStruct((b, t, SWA_Q_DIM), F32),
                   jax.ShapeDtypeStruct((b, nrows, SWA_KV_DIM), F32),
                   jax.ShapeDtypeStruct((b, nrows, SWA_KV_DIM), F32)],
        compiler_params=pltpu.CompilerParams(dimension_semantics=("parallel",)),
        name="swa_sample",
    )(q, k, v, k_buf, v_buf, sgb, sink_rows)


def _out_kernel(x_ref, og_ref, zga_ref, osg_ref, gn_ref, wo_ref, nw2_ref, win_ref, wout_ref, nf_ref, y_ref):
    gn = gn_ref[...]
    heads = []
    for h in range(GDN_HEADS):
        lo = h * GDN_DV
        heads.append(_rms(og_ref[:, lo:lo + GDN_DV], gn))
    mixed = jnp.concatenate(heads, axis=1) * zga_ref[...] + osg_ref[...]
    x2 = x_ref[...] + jnp.dot(mixed.astype(BF16), wo_ref[...], preferred_element_type=F32)
    x3 = _swiglu_half_step(x2, nw2_ref[...], win_ref, wout_ref)
    y_ref[...] = _rms(x3, nf_ref[...])


def _out(x1, og, zga, osg, gnorm, wo, nw2, win, wout, nf, tm):
    t = x1.shape[0]
    row = pl.BlockSpec((tm, D_MODEL), lambda i: (i, 0))
    wbytes = (wo.size + win.size + wout.size) * 2
    return pl.pallas_call(
        _out_kernel,
        grid=(t // tm,),
        in_specs=[row] * 4 + [_const_spec(gnorm.shape), _const_spec(wo.shape), _const_spec(nw2.shape),
                              _const_spec(win.shape), _const_spec(wout.shape), _const_spec(nf.shape)],
        out_specs=row,
        out_shape=jax.ShapeDtypeStruct((t, D_MODEL), F32),
        compiler_params=pltpu.CompilerParams(
            dimension_semantics=("parallel",),
            vmem_limit_bytes=_vmem_limit(wbytes + 18 * tm * D_MODEL * 4 + (8 << 20))),
        name="out",
    )(x1, og, zga, osg, gnorm, wo, nw2, win, wout, nf)


def _rope_tables(pos):
    inv = ROPE_THETA ** (-jnp.arange(0, ROPE_DIM, 2, dtype=F32) / ROPE_DIM)
    ang = pos.astype(F32)[:, None] * inv[None, :]
    cos, sin = jnp.cos(ang), jnp.sin(ang)
    n = pos.shape[0]
    pad = SWA_HEAD_DIM - ROPE_DIM
    one_head = lambda first, second, fill: jnp.concatenate(
        [first, second, jnp.full((n, pad), fill, F32)], axis=1)
    zeros = jnp.zeros_like(sin)
    cos_f = one_head(cos, cos, 1.0)
    sin_a = one_head(-sin, zeros, 0.0)
    sin_b = one_head(zeros, sin, 0.0)
    reps = V7X_LANES // SWA_HEAD_DIM
    return tuple(jnp.tile(t, (1, reps)) for t in (cos_f, sin_a, sin_b))


def _pad_lanes(v, width):
    return jnp.pad(v.astype(F32), (0, width - v.shape[0]))[None, :]


def _branch(x, pos, conv_buf, s0, k_buf, v_buf, p):
    b, l, _ = x.shape
    t = b * l
    is_prompt = conv_buf is None
    chunk = min(GDN_CHUNK, l)
    n = l // chunk
    tm = _token_tile(l if is_prompt else t, TOKEN_TILE)
    tables = _rope_tables(pos)
    if is_prompt:
        halo = None
        table_tiles = l // tm
    else:
        halo = jnp.pad(conv_buf, ((0, 0), (l - (CONV_W - 1), 0), (0, 0))).reshape(t, CONV_DIM)
        tables = tuple(jnp.tile(tb, (tm // l, 1)) for tb in tables)
        table_tiles = 1
    tm_conv = tm if is_prompt else _token_tile(t, TOKEN_TILE // 2)
    x1, qg, kg, vg, tail = _ffn_conv(x.reshape(t, D_MODEL), p['norm_ffn1'], p['w_ffn1_in'], p['w_ffn1_out'],
                                     p['norm_mix'], p['w_qkv'], p['conv_w'], halo,
                                     tm=tm_conv, seq_len=l, group=l)
    gcb, zga, qr, kr, vs, sgb = _mix(x1, p['norm_mix'], p['w_mix'], p['a_log_pad'], p['dt_bias_pad'],
                                     tables, tm=tm, table_tiles=table_tiles, chunk=chunk)
    seq = lambda a: a.reshape(b, l, a.shape[-1])
    qg, kg, vg, gcb, qr, kr, vs, sgb = (seq(a) for a in (qg, kg, vg, gcb, qr, kr, vs, sgb))
    new_conv = tail.reshape(b, -1, CONV_DIM)[:, -(CONV_W - 1):]

    gc = gcb[:, :, :GDN_HEADS]
    grow = gc.reshape(b, n, chunk, GDN_HEADS).transpose(0, 1, 3, 2)
    if is_prompt:
        assert chunk == PK_C and s0 is None
        u, w, qgd, kgd, qk, gl = _pk_pre(qg, kg, vg, gcb, grow.reshape(b, n, 1, PK_CAT), _token_tile(n, 4))
        og, new_s = _pk_scan(u, w, qgd, kgd, qk, gl)
        sink_cols = jnp.repeat(p['sinks'].reshape(SWA_Q_HEADS // 2, 2), WINDOW, axis=1)
        osg = _swa_prompt(qr, kr, vs, sgb, sink_cols.reshape(SWA_Q_HEADS // 2, 1, 2 * WINDOW))
        new_k, new_v = kr[:, l - WINDOW:], vs[:, l - WINDOW:]
    else:
        assert n == 1
        og, new_s = _gdn_sample(qg, kg, vg, gcb, jnp.repeat(gc, l, axis=-1),
                                grow.reshape(b, 1, GDN_HEADS * l), s0, _token_tile(b, 8))
        sink_rows = jnp.repeat(p['sinks'], l, axis=1).reshape(SWA_KV_HEADS, SWA_GROUP * l, 1)
        nrows = k_buf.shape[1]
        osg, new_k, new_v = _swa_sample(qr, kr, vs, k_buf.reshape(b, nrows, SWA_KV_DIM),
                                        v_buf.reshape(b, nrows, SWA_KV_DIM), sgb, sink_rows,
                                        _token_tile(b, 16))
    y = _out(x1, og.reshape(t, GDN_V_DIM), zga, osg.reshape(t, SWA_Q_DIM), p['gdn_norm'], p['w_out'],
             p['norm_ffn2'], p['w_ffn2_in'], p['w_ffn2_out'], p['norm_final'], tm)
    kv_shape = (b, new_k.shape[1], SWA_KV_HEADS, SWA_HEAD_DIM)
    return (y.reshape(b, l, D_MODEL), new_conv, new_s, new_k.reshape(kv_shape), new_v.reshape(kv_shape))


def _layer_params(l, norm_ffn1, w_ffn1_in, w_ffn1_out, norm_mix, w_in, conv_w, gdn_a_log, gdn_dt_bias,
                  gdn_norm, swa_sinks, w_out, norm_ffn2, w_ffn2_in, w_ffn2_out, norm_final):
    bounds = [0]
    for s in IN_SPLITS:
        bounds.append(bounds[-1] + s)
    w_in_t = jnp.swapaxes(w_in[l], 0, 1)
    col = lambda i: w_in_t[bounds[i]:bounds[i + 1]].astype(BF16)
    w_ab = jnp.pad(jnp.concatenate([col(2), col(3)], axis=0), ((0, V7X_LANES - 2 * GDN_HEADS), (0, 0)))
    row = lambda v: v.astype(F32)[None, :]
    return {
        'norm_ffn1': row(norm_ffn1[l]), 'w_ffn1_in': w_ffn1_in[l].astype(BF16),
        'w_ffn1_out': w_ffn1_out[l].astype(BF16),
        'norm_mix': row(norm_mix[l]),
        'w_qkv': col(0),
        'w_mix': [w_ab, col(1), col(7), col(4), col(5), col(6), col(8)],
        'conv_w': conv_w[l].astype(F32),
        'a_log_pad': _pad_lanes(gdn_a_log[l], V7X_LANES), 'dt_bias_pad': _pad_lanes(gdn_dt_bias[l], V7X_LANES),
        'gdn_norm': row(gdn_norm[l]),
        'sinks': swa_sinks[l].astype(F32).reshape(SWA_KV_HEADS, SWA_GROUP),
        'w_out': w_out[l].astype(BF16),
        'norm_ffn2': row(norm_ffn2[l]), 'w_ffn2_in': w_ffn2_in[l].astype(BF16),
        'w_ffn2_out': w_ffn2_out[l].astype(BF16),
        'norm_final': row(norm_final),
    }


def kernel(x_prompt, x_sample, state_conv, state_gdn, cache_swa_k, cache_swa_v, norm_ffn1, w_ffn1_in, w_ffn1_out, norm_mix, w_in, conv_w, gdn_a_log, gdn_dt_bias, gdn_norm, swa_sinks, w_out, norm_ffn2, w_ffn2_in, w_ffn2_out, norm_final):
    depth = w_in.shape[0]
    assert depth == 1, "the final norm is fused into the layer's last kernel"
    l_p = x_prompt.shape[1]
    l_s = x_sample.shape[1]
    pos_p = jnp.arange(l_p, dtype=jnp.int32)
    pos_s = PAST_LEN + jnp.arange(l_s, dtype=jnp.int32)
    p = _layer_params(0, norm_ffn1, w_ffn1_in, w_ffn1_out, norm_mix, w_in, conv_w, gdn_a_log, gdn_dt_bias,
                      gdn_norm, swa_sinks, w_out, norm_ffn2, w_ffn2_in, w_ffn2_out, norm_final)
    yp, c1, s1, k1, v1 = _branch(x_prompt, pos_p, None, None, None, None, p)
    ys, c2, s2, k2, v2 = _branch(x_sample, pos_s, state_conv[0], state_gdn[0], cache_swa_k[0],
                                 cache_swa_v[0], p)
    return (yp, ys, c1[None], c2[None], s1[None], s2[None], k1[None], k2[None], v1[None], v2[None])
```

```python
import functools

import jax
import jax.numpy as jnp
from jax import lax
from jax.experimental import pallas as pl
from jax.experimental.pallas import tpu as pltpu

D_MODEL = 1024
PAST_LEN = 16384
GDN_HEADS = 8
GDN_DK = 128
GDN_DV = 128
GDN_QK_DIM = GDN_HEADS * GDN_DK
GDN_V_DIM = GDN_HEADS * GDN_DV
CONV_W = 4
GDN_CHUNK = 64
CONV_DIM = 2 * GDN_QK_DIM + GDN_V_DIM
SWA_Q_HEADS = 16
SWA_KV_HEADS = 4
SWA_GROUP = SWA_Q_HEADS // SWA_KV_HEADS
SWA_HEAD_DIM = 64
SWA_Q_DIM = SWA_Q_HEADS * SWA_HEAD_DIM
SWA_KV_DIM = SWA_KV_HEADS * SWA_HEAD_DIM
WINDOW = 128
ROPE_DIM = SWA_HEAD_DIM // 4
ROPE_HALF = ROPE_DIM // 2
ROPE_THETA = 500000.0
D_FF = 2816
EPS = 1e-6
LOG2_E = 1.4426950408889634
IN_SPLITS = (CONV_DIM, GDN_V_DIM, GDN_HEADS, GDN_HEADS, SWA_Q_DIM, SWA_KV_DIM, SWA_KV_DIM, D_MODEL, D_MODEL)

V7X_LANES = 128
V7X_SUBLANES = 8
V7X_MXU_DIM = 256
V7X_VMEM_BYTES = 64 * 1024 * 1024

TOKEN_TILE = 2 * V7X_MXU_DIM
FFN_CHUNK = V7X_MXU_DIM
CONV_PIECE = V7X_MXU_DIM
SOLVE_BASE = V7X_SUBLANES
SWA_WINDOWS_PER_STEP = 4

F32 = jnp.float32
BF16 = jnp.bfloat16


def _dot(a, b):
    return jnp.dot(a.astype(BF16), b.astype(BF16), preferred_element_type=F32)


def _dot_nt(a, b):
    return lax.dot_general(a.astype(BF16), b.astype(BF16), (((1,), (1,)), ((), ())),
                           preferred_element_type=F32)


def _dot_tn(a, b):
    return lax.dot_general(a.astype(BF16), b.astype(BF16), (((0,), (0,)), ((), ())),
                           preferred_element_type=F32)


def _proj(xn, w_t):
    return lax.dot_general(xn, w_t, (((1,), (1,)), ((), ())), preferred_element_type=F32)


def _rms(x, w):
    return x * lax.rsqrt(jnp.mean(x * x, axis=-1, keepdims=True) + EPS) * w


def _sigmoid(x):
    return 0.5 * jnp.tanh(0.5 * x) + 0.5


def _silu(x):
    h = 0.5 * x
    return h * jnp.tanh(h) + h


def _vmem_limit(nbytes):
    return int(min(V7X_VMEM_BYTES - (4 << 20), max(nbytes, 16 << 20)))


def _const_spec(shape):
    nd = len(shape)
    return pl.BlockSpec(shape, lambda *_: (0,) * nd, pipeline_mode=pl.Buffered(1))


def _token_tile(t, want):
    tm = min(want, t)
    while t % tm:
        tm //= 2
    return tm


FFN_NCHUNK = D_FF // FFN_CHUNK


def _swiglu_act(xn, win_ref, c):
    lo = c * FFN_CHUNK
    g = jnp.dot(xn, win_ref[:, lo:lo + FFN_CHUNK], preferred_element_type=F32)
    u = jnp.dot(xn, win_ref[:, D_FF + lo:D_FF + lo + FFN_CHUNK], preferred_element_type=F32)
    return (_silu(g) * u).astype(BF16)


def _swiglu_half_step(x, norm_w, win_ref, wout_ref):
    xn = _rms(x, norm_w).astype(BF16)
    act = jnp.concatenate([_swiglu_act(xn, win_ref, c) for c in range(FFN_NCHUNK)], axis=1)
    return x + 0.5 * jnp.dot(act, wout_ref[...], preferred_element_type=F32)


def _ffn_conv_kernel(*refs, tiles_per_seq, group):
    carried = tiles_per_seq > 0
    if carried:
        (x_ref, nw1_ref, win_ref, wout_ref, nw_ref, wqkv_ref, cw_ref,
         x1_ref, qg_ref, kg_ref, vg_ref, tail_ref, xn_ref, carry_ref) = refs
    else:
        (x_ref, nw1_ref, win_ref, wout_ref, nw_ref, wqkv_ref, cw_ref, halo_ref,
         x1_ref, qg_ref, kg_ref, vg_ref, tail_ref, xn_ref) = refs
    rows = x_ref.shape[0]
    step = pl.program_id(0)
    n_tiles = pl.num_programs(0) - 1
    if carried:
        seq_start = ((step - 1) % tiles_per_seq) == 0
    else:
        pos = lax.broadcasted_iota(jnp.int32, (rows, 1), 0) % group

    def conv_piece(c):
        cols = slice(c * CONV_PIECE, (c + 1) * CONV_PIECE)
        raw = _proj(xn_ref[...], wqkv_ref[cols, :])
        cw = cw_ref[:, cols]
        acc = raw * cw[CONV_W - 1:CONV_W, :]
        if carried:
            halo = jnp.where(seq_start, 0.0, carry_ref[:, cols])
            ext = jnp.concatenate([halo, raw], axis=0)
            for s in range(1, CONV_W):
                acc = acc + pltpu.roll(ext, s, 0)[V7X_SUBLANES:] * cw[CONV_W - 1 - s:CONV_W - s, :]
            carry_ref[:, cols] = raw[rows - V7X_SUBLANES:]
            tail_ref[0, :, cols] = raw[rows - V7X_SUBLANES:]
        else:
            halo = halo_ref[:, cols]
            for s in range(1, CONV_W):
                shifted = jnp.where(pos >= s, pltpu.roll(raw, s, 0),
                                    pltpu.roll(halo, (rows + s - group) % rows, 0))
                acc = acc + shifted * cw[CONV_W - 1 - s:CONV_W - s, :]
            tail_ref[:, cols] = raw
        conv = _silu(acc)
        part, lo = divmod(c * CONV_PIECE, GDN_QK_DIM)
        out_ref = (qg_ref, kg_ref, vg_ref)[part]
        if part == 2:
            out_ref[:, lo:lo + CONV_PIECE] = conv.astype(BF16)
        else:
            for h in range(CONV_PIECE // GDN_DK):
                ch = conv[:, h * GDN_DK:(h + 1) * GDN_DK]
                out_ref[:, lo + h * GDN_DK:lo + (h + 1) * GDN_DK] = (
                    ch * lax.rsqrt(jnp.sum(ch * ch, axis=-1, keepdims=True) + EPS)).astype(BF16)

    def body(do_ffn, do_conv):
        n_pieces = CONV_DIM // CONV_PIECE
        n_out = D_MODEL // FFN_CHUNK
        slots = FFN_NCHUNK + n_out
        due = [((k + 1) * n_pieces) // slots for k in range(slots)] if do_ffn else [n_pieces]
        done = 0
        if do_ffn:
            x = x_ref[...]
            xn = _rms(x, nw1_ref[...]).astype(BF16)
            acts, outs = [], []
        for k, upto in enumerate(due):
            if do_ffn and k < FFN_NCHUNK:
                acts.append(_swiglu_act(xn, win_ref, k))
            elif do_ffn:
                if k == FFN_NCHUNK:
                    act = jnp.concatenate(acts, axis=1)
                lo = (k - FFN_NCHUNK) * FFN_CHUNK
                outs.append(jnp.dot(act, wout_ref[:, lo:lo + FFN_CHUNK], preferred_element_type=F32))
            if do_conv:
                for c in range(done, upto):
                    conv_piece(c)
                done = upto
        if do_ffn:
            x1 = x + 0.5 * jnp.concatenate(outs, axis=1)
            x1_ref[...] = x1
            xn_ref[...] = _rms(x1, nw_ref[...]).astype(BF16)

    @pl.when(step == 0)
    def _():
        if carried:
            carry_ref[...] = jnp.zeros_like(carry_ref)
        body(True, False)

    @pl.when((step > 0) & (step < n_tiles))
    def _():
        body(True, True)

    @pl.when(step == n_tiles)
    def _():
        body(False, True)


def _ffn_conv(x, norm1, win, wout, norm_w, wqkv, cw, halo, *, tm, seq_len, group):
    t = x.shape[0]
    carried = halo is None
    tiles_per_seq = seq_len // tm if carried else 0
    n_tiles = t // tm
    cur = lambda n: pl.BlockSpec((tm, n), lambda i: (jnp.minimum(i, n_tiles - 1), 0))
    prev = lambda n: pl.BlockSpec((tm, n), lambda i: (jnp.maximum(i - 1, 0), 0))
    weights = [win, wout, wqkv]
    in_specs = [cur(D_MODEL), _const_spec((1, D_MODEL)), _const_spec(win.shape), _const_spec(wout.shape),
                _const_spec((1, D_MODEL)), _const_spec(wqkv.shape), _const_spec(cw.shape)]
    args = [x, norm1, win, wout, norm_w, wqkv, cw]
    out_specs = [cur(D_MODEL), prev(GDN_QK_DIM), prev(GDN_QK_DIM), prev(GDN_V_DIM)]
    out_shape = [jax.ShapeDtypeStruct((t, D_MODEL), F32)] + [
        jax.ShapeDtypeStruct((t, n), BF16) for n in (GDN_QK_DIM, GDN_QK_DIM, GDN_V_DIM)]
    scratch = [pltpu.VMEM((tm, D_MODEL), BF16)]
    if carried:
        nseq = t // seq_len
        out_specs.append(pl.BlockSpec((1, V7X_SUBLANES, CONV_DIM),
                                      lambda i: (jnp.maximum(i - 1, 0) // tiles_per_seq, 0, 0)))
        out_shape.append(jax.ShapeDtypeStruct((nseq, V7X_SUBLANES, CONV_DIM), F32))
        scratch.append(pltpu.VMEM((V7X_SUBLANES, CONV_DIM), F32))
    else:
        in_specs.append(prev(CONV_DIM))
        args.append(halo)
        out_specs.append(prev(CONV_DIM))
        out_shape.append(jax.ShapeDtypeStruct((t, CONV_DIM), F32))
    wbytes = sum(w.size for w in weights) * 2
    kern = functools.partial(_ffn_conv_kernel, tiles_per_seq=tiles_per_seq, group=group)
    return pl.pallas_call(
        kern,
        grid=(n_tiles + 1,),
        in_specs=in_specs,
        out_specs=out_specs,
        out_shape=out_shape,
        scratch_shapes=scratch,
        compiler_params=pltpu.CompilerParams(
            dimension_semantics=("arbitrary",),
            vmem_limit_bytes=_vmem_limit(wbytes + 32 * tm * D_MODEL * 4 + (4 << 20))),
        name="ffn_conv",
    )(*args)


def _rope(x, cos_f, sin_a, sin_b):
    width = x.shape[1]
    reps = width // V7X_LANES
    tile = lambda t: jnp.concatenate([t] * reps, axis=1) if reps > 1 else t
    return (x * tile(cos_f) + pltpu.roll(x, width - ROPE_HALF, 1) * tile(sin_a)
            + pltpu.roll(x, ROPE_HALF, 1) * tile(sin_b))


def _mix_kernel(x_ref, nw_ref, wab_ref, wz_ref, wga_ref, wqs_ref, wks_ref, wvs_ref, wgb_ref,
                alog_ref, dtb_ref, cos_ref, sina_ref, sinb_ref,
                gcb_ref, zga_ref, qr_ref, kr_ref, vs_ref, sgb_ref, *, chunk):
    rows = x_ref.shape[0]
    xn = _rms(x_ref[...], nw_ref[...]).astype(BF16)

    ab = _proj(xn, wab_ref[...])
    sp_in = ab + dtb_ref[...]
    softplus = jnp.maximum(sp_in, 0.0) + jnp.log1p(jnp.exp(-jnp.abs(sp_in)))
    gc = -jnp.exp(alog_ref[...]) * softplus
    cpos = lax.broadcasted_iota(jnp.int32, (rows, 1), 0) % chunk
    s = 1
    while s < chunk:
        gc = gc + jnp.where(cpos >= s, pltpu.roll(gc, s, 0), 0.0)
        s *= 2
    lane = lax.broadcasted_iota(jnp.int32, (1, V7X_LANES), 1)
    gcb_ref[...] = jnp.where(lane < GDN_HEADS, gc, jax.nn.sigmoid(ab))

    zga_ref[...] = _silu(_proj(xn, wz_ref[...])) * _sigmoid(_proj(xn, wga_ref[...]))
    cos_f, sin_a, sin_b = cos_ref[...], sina_ref[...], sinb_ref[...]
    qr_ref[...] = _rope(_proj(xn, wqs_ref[...]), cos_f, sin_a, sin_b)
    kr_ref[...] = _rope(_proj(xn, wks_ref[...]), cos_f, sin_a, sin_b)
    vs_ref[...] = _proj(xn, wvs_ref[...])
    sgb_ref[...] = _sigmoid(_proj(xn, wgb_ref[...]))


def _mix(x, norm_w, weights, alog, dtb, tables, *, tm, table_tiles, chunk):
    t = x.shape[0]
    row = lambda n: pl.BlockSpec((tm, n), lambda i: (i, 0))
    tab = pl.BlockSpec((tm, V7X_LANES), lambda i: (i % table_tiles, 0))
    widths = (V7X_LANES, GDN_V_DIM, SWA_Q_DIM, SWA_KV_DIM, SWA_KV_DIM, D_MODEL)
    wbytes = sum(w.size for w in weights) * 2
    return pl.pallas_call(
        functools.partial(_mix_kernel, chunk=chunk),
        grid=(t // tm,),
        in_specs=[row(D_MODEL), _const_spec((1, D_MODEL))] + [_const_spec(w.shape) for w in weights]
                 + [_const_spec(alog.shape), _const_spec(dtb.shape), tab, tab, tab],
        out_specs=[row(n) for n in widths],
        out_shape=[jax.ShapeDtypeStruct((t, n), F32) for n in widths],
        compiler_params=pltpu.CompilerParams(
            dimension_semantics=("parallel",),
            vmem_limit_bytes=_vmem_limit(wbytes + 32 * tm * D_MODEL * 4 + (4 << 20))),
        name="mix",
    )(x, norm_w, *weights, alog, dtb, *tables)


PK_C = GDN_CHUNK
PK_CAT = GDN_HEADS * PK_C
PK_PAIR = 2 * GDN_DK
PK_NPAIR = GDN_HEADS // 2
PK_GROUP = V7X_MXU_DIM // PK_C
PK_CHUNKS_PER_STEP = 8
PK_SCAN_CHUNKS_PER_STEP = 4


def _pk_masks():
    c = PK_C
    i = lax.broadcasted_iota(jnp.int32, (c, PK_CAT), 0)
    j = lax.broadcasted_iota(jnp.int32, (c, PK_CAT), 1) % c
    lo = (lax.broadcasted_iota(jnp.int32, (c, V7X_LANES), 1) < c).astype(BF16)
    return i, j, (lo, 1 - lo)


def _pk_prod(x, y, bd_mask):
    lo, hi = bd_mask
    zero = jnp.zeros((PK_C, V7X_LANES), BF16)
    outs = []
    for g in range(GDN_HEADS // PK_GROUP):
        sl = slice(g * V7X_MXU_DIM, (g + 1) * V7X_MXU_DIM)
        yb = y[:, sl].astype(BF16)
        t0, t1 = yb[:, :V7X_LANES], yb[:, V7X_LANES:]
        bd = jnp.concatenate([jnp.concatenate([t0 * lo, zero], axis=1),
                              jnp.concatenate([t0 * hi, zero], axis=1),
                              jnp.concatenate([zero, t1 * lo], axis=1),
                              jnp.concatenate([zero, t1 * hi], axis=1)], axis=0)
        outs.append(jnp.dot(x[:, sl].astype(BF16), bd, preferred_element_type=F32))
    return jnp.concatenate(outs, axis=1)


def _pk_solve(a_list, i, j, bd_mask):
    prod_all = lambda xs, ys: [_pk_prod(x, y, bd_mask) for x, y in zip(xs, ys)]
    eye = (i == j).astype(F32)
    b = SOLVE_BASE
    same = (i // b) == (j // b)
    p = [jnp.where(same, a, 0.0) for a in a_list]
    d = [eye - x for x in p]
    span = 2
    while span < b:
        p = prod_all(p, p)
        d = [x + y for x, y in zip(d, prod_all(d, p))]
        span *= 2
    while b < PK_C:
        same2 = (i // (2 * b)) == (j // (2 * b))
        sel = same2 & jnp.logical_not(same)
        off = [jnp.where(sel, a, 0.0) for a in a_list]
        d = [x - y for x, y in zip(d, prod_all(d, prod_all(off, d)))]
        same = same2
        b *= 2
    return d


def _pk_pre_kernel(q_ref, k_ref, v_ref, gcb_ref, grow_ref,
                   u_ref, w_ref, qg_ref, kg_ref, qk_ref, gl_ref, *, nc):
    c = PK_C
    i, j, bd_mask = _pk_masks()
    lane_lo = lax.broadcasted_iota(jnp.int32, (c, V7X_LANES), 1) < c
    zeros_k = jnp.zeros((c, GDN_DK), BF16)
    a_list, vb_list, kbg_list = [], [], []
    for ch in range(nc):
        rows = slice(ch * c, (ch + 1) * c)
        gcb = gcb_ref[0, rows, :]
        g_end = gcb[c - 1:c, :]
        eg_s = jnp.exp(gcb)
        ek_s = jnp.exp(g_end - gcb)
        col = lambda arr, n: jnp.broadcast_to(arr[:, n:n + 1], (c, V7X_LANES))
        full = lambda arr, base: jnp.concatenate([col(arr, base + h) for h in range(GDN_HEADS)], axis=1)
        gc_cols = [col(gcb, h) for h in range(GDN_HEADS)]
        beta_f = full(gcb, GDN_HEADS)
        eg_f = full(eg_s, 0)
        ek_f = full(ek_s, 0)
        gc_cat = jnp.concatenate([jnp.where(lane_lo, gc_cols[2 * m], gc_cols[2 * m + 1])
                                  for m in range(PK_NPAIR)], axis=1)
        decay = jnp.exp(jnp.where(i >= j, gc_cat - grow_ref[0, ch], -jnp.inf))
        q = q_ref[0, rows, :].astype(F32) * (GDN_DK ** -0.5)
        k = k_ref[0, rows, :].astype(F32)
        kb = k * beta_f
        vb_list.append((v_ref[0, rows, :].astype(F32) * beta_f).astype(BF16))
        kbg_list.append((kb * eg_f).astype(BF16))
        qg_ref[0, rows, :] = (q * eg_f).astype(BF16)
        kg_ref[0, rows, :] = (k * ek_f).astype(BF16)
        gl_ref[0, ch] = eg_f[c - 1:c, :]
        kk = []
        for m in range(PK_NPAIR):
            sl = slice(m * PK_PAIR, (m + 1) * PK_PAIR)
            lhs = jnp.concatenate([kb[:, sl], q[:, sl]], axis=0).astype(BF16)
            k2 = k[:, sl].astype(BF16)
            rhs_nt = jnp.concatenate(
                [jnp.concatenate([k2[:, :GDN_DK], zeros_k], axis=1),
                 jnp.concatenate([zeros_k, k2[:, GDN_DK:]], axis=1)], axis=0)
            kk.append(lax.dot_general(lhs, rhs_nt, (((1,), (1,)), ((), ())), preferred_element_type=F32))
        a_raw = jnp.concatenate([x[:c] for x in kk], axis=1)
        qk_raw = jnp.concatenate([x[c:] for x in kk], axis=1)
        a_list.append(jnp.where(i > j, a_raw * decay, 0.0))
        qk_ref[0, rows, :] = (qk_raw * decay).astype(BF16)
    t_list = _pk_solve(a_list, i, j, bd_mask)
    for ch in range(nc):
        rows = slice(ch * c, (ch + 1) * c)
        t = t_list[ch]
        for m in range(PK_NPAIR):
            tp = t[:, m * V7X_LANES:(m + 1) * V7X_LANES]
            lhs = jnp.concatenate([jnp.where(lane_lo, tp, 0.0), jnp.where(lane_lo, 0.0, tp)],
                                  axis=0).astype(BF16)
            h0, h1 = 2 * m, 2 * m + 1
            hs = lambda h: slice(h * GDN_DK, (h + 1) * GDN_DK)
            rhs = jnp.concatenate(
                [jnp.concatenate([vb_list[ch][:, hs(h0)], kbg_list[ch][:, hs(h0)]], axis=1),
                 jnp.concatenate([vb_list[ch][:, hs(h1)], kbg_list[ch][:, hs(h1)]], axis=1)], axis=0)
            uw = jnp.dot(lhs, rhs, preferred_element_type=F32)
            u_ref[0, rows, hs(h0)] = uw[:c, :GDN_DV]
            w_ref[0, rows, hs(h0)] = uw[:c, GDN_DV:].astype(BF16)
            u_ref[0, rows, hs(h1)] = uw[c:, :GDN_DV]
            w_ref[0, rows, hs(h1)] = uw[c:, GDN_DV:].astype(BF16)


def _pk_pre(q, k, v, gcb, grow_flat, nc):
    b, l, _ = q.shape
    tl = nc * PK_C
    tok = lambda wd: pl.BlockSpec((1, tl, wd), lambda bi, ti: (bi, ti, 0))
    per_chunk = lambda wd: pl.BlockSpec((1, nc, 1, wd), lambda bi, ti: (bi, ti, 0, 0))
    kern = functools.partial(_pk_pre_kernel, nc=nc)
    n = l // PK_C
    return pl.pallas_call(
        kern,
        grid=(b, l // tl),
        in_specs=[tok(GDN_QK_DIM), tok(GDN_QK_DIM), tok(GDN_V_DIM), tok(V7X_LANES), per_chunk(PK_CAT)],
        out_specs=[tok(GDN_V_DIM), tok(GDN_QK_DIM), tok(GDN_QK_DIM), tok(GDN_QK_DIM), tok(PK_CAT),
                   per_chunk(GDN_V_DIM)],
        out_shape=[jax.ShapeDtypeStruct((b, l, GDN_V_DIM), F32),
                   jax.ShapeDtypeStruct((b, l, GDN_QK_DIM), BF16),
                   jax.ShapeDtypeStruct((b, l, GDN_QK_DIM), BF16),
                   jax.ShapeDtypeStruct((b, l, GDN_QK_DIM), BF16),
                   jax.ShapeDtypeStruct((b, l, PK_CAT), BF16),
                   jax.ShapeDtypeStruct((b, n, 1, GDN_V_DIM), F32)],
        compiler_params=pltpu.CompilerParams(dimension_semantics=("parallel", "parallel"),
                                             vmem_limit_bytes=_vmem_limit(48 << 20)),
        name="gdn_pre_packed",
    )(q, k, v, gcb, grow_flat)


def _pk_scan_kernel(u_ref, w_ref, qg_ref, kg_ref, qk_ref, gl_ref, o_ref, sout_ref, s_ref):
    c = PK_C
    nb = u_ref.shape[0]
    step = pl.program_id(0)

    @pl.when(step == 0)
    def _():
        s_ref[...] = jnp.zeros_like(s_ref)

    zeros_s = jnp.zeros((GDN_DK, GDN_DV), BF16)
    zeros_v = jnp.zeros((c, GDN_DV), BF16)
    chains = [(b, m) for b in range(nb) for m in range(PK_NPAIR)]
    pair = lambda m: slice(m * PK_PAIR, (m + 1) * PK_PAIR)
    state = {bm: s_ref[bm] for bm in chains}

    for ch in range(u_ref.shape[1] // c):
        rows = slice(ch * c, (ch + 1) * c)
        wq, bd_v = {}, {}

        def state_products(b, m):
            sb = state[b, m].astype(BF16)
            bd_s = jnp.concatenate([jnp.concatenate([sb[:, :GDN_DV], zeros_s], axis=1),
                                    jnp.concatenate([zeros_s, sb[:, GDN_DV:]], axis=1)], axis=0)
            lhs = jnp.concatenate([w_ref[b, rows, pair(m)], qg_ref[b, rows, pair(m)]], axis=0)
            wq[b, m] = jnp.dot(lhs, bd_s, preferred_element_type=F32)

        def new_values(b, m):
            v_new = (u_ref[b, rows, pair(m)] - wq[b, m][:c]).astype(BF16)
            bd_v[b, m] = jnp.concatenate([jnp.concatenate([v_new[:, :GDN_DV], zeros_v], axis=1),
                                          jnp.concatenate([zeros_v, v_new[:, GDN_DV:]], axis=1)], axis=0)

        def outputs_and_update(b, m):
            qk_p = qk_ref[b, rows, m * V7X_LANES:(m + 1) * V7X_LANES]
            o_ref[b, rows, pair(m)] = wq[b, m][c:] + jnp.dot(qk_p, bd_v[b, m], preferred_element_type=F32)
            kg_p = kg_ref[b, rows, pair(m)]
            kg_st = jnp.concatenate([kg_p[:, :GDN_DK], kg_p[:, GDN_DK:]], axis=0)
            upd = lax.dot_general(kg_st, bd_v[b, m], (((0,), (0,)), ((), ())),
                                  preferred_element_type=F32)
            state[b, m] = state[b, m] * gl_ref[b, ch, :, pair(m)] + upd

        for stage in (state_products, new_values, outputs_and_update):
            for b, m in chains:
                stage(b, m)

    for bm in chains:
        s_ref[bm] = state[bm]

    @pl.when(step == pl.num_programs(0) - 1)
    def _():
        for b, m in chains:
            s_c = s_ref[b, m]
            sout_ref[b, 2 * m] = s_c[:, :GDN_DV]
            sout_ref[b, 2 * m + 1] = s_c[:, GDN_DV:]


def _pk_scan(u, w, qg, kg, qk, gl):
    b, l, _ = u.shape
    n = l // PK_C
    per = _token_tile(n, PK_SCAN_CHUNKS_PER_STEP)
    tok = lambda wd: pl.BlockSpec((b, per * PK_C, wd), lambda s: (0, s, 0))
    return pl.pallas_call(
        _pk_scan_kernel,
        grid=(n // per,),
        in_specs=[tok(GDN_V_DIM), tok(GDN_QK_DIM), tok(GDN_QK_DIM), tok(GDN_QK_DIM), tok(PK_CAT),
                  pl.BlockSpec((b, per, 1, GDN_V_DIM), lambda s: (0, s, 0, 0))],
        out_specs=[tok(GDN_V_DIM),
                   pl.BlockSpec((b, GDN_HEADS, GDN_DK, GDN_DV), lambda s: (0, 0, 0, 0))],
        out_shape=[jax.ShapeDtypeStruct((b, l, GDN_V_DIM), F32),
                   jax.ShapeDtypeStruct((b, GDN_HEADS, GDN_DK, GDN_DV), F32)],
        scratch_shapes=[pltpu.VMEM((b, PK_NPAIR, GDN_DK, 2 * GDN_DV), F32)],
        compiler_params=pltpu.CompilerParams(dimension_semantics=("arbitrary",)),
        name="gdn_scan_packed",
    )(u, w, qg, kg, qk, gl)


def _head_block_rows(x, t):
    width = x.shape[1] // GDN_HEADS
    z = jnp.zeros((t, width), x.dtype)
    rows = []
    for h in range(GDN_HEADS):
        rows.append(jnp.concatenate([z] * h + [x[:, h * width:(h + 1) * width]]
                                    + [z] * (GDN_HEADS - 1 - h), axis=1))
    return jnp.concatenate(rows, axis=0)


def _gdn_sample_kernel(q_ref, k_ref, v_ref, gcb_ref, grep_ref, grow_ref, s0_ref, o_ref, sout_ref):
    bt, t, _ = q_ref.shape
    cat = GDN_HEADS * t
    i = lax.broadcasted_iota(jnp.int32, (t, cat), 0)
    j = lax.broadcasted_iota(jnp.int32, (t, cat), 1) % t
    r = lax.broadcasted_iota(jnp.int32, (cat, cat), 0) // t
    l = lax.broadcasted_iota(jnp.int32, (cat, cat), 1) // t
    same_head = r == l
    eye = (i == j).astype(F32)
    seqs = range(bt)
    pair = lambda m: slice(m * PK_PAIR, (m + 1) * PK_PAIR)

    vb, kbg, qg, kg, gl, a, qk = {}, {}, {}, {}, {}, {}, {}
    for b in seqs:
        gcb = gcb_ref[b]
        g_end = gcb[t - 1:t, :]
        eg_s = jnp.exp(gcb)
        ek_s = jnp.exp(g_end - gcb)
        col = lambda arr, n: jnp.broadcast_to(arr[:, n:n + 1], (t, V7X_LANES))
        full = lambda arr, base: jnp.concatenate([col(arr, base + h) for h in range(GDN_HEADS)], axis=1)
        beta_f, eg_f, ek_f = full(gcb, GDN_HEADS), full(eg_s, 0), full(ek_s, 0)
        decay = jnp.exp(jnp.where(i >= j, grep_ref[b] - grow_ref[b], -jnp.inf))
        q = q_ref[b].astype(F32) * (GDN_DK ** -0.5)
        k = k_ref[b].astype(F32)
        kb = k * beta_f
        vb[b] = v_ref[b].astype(F32) * beta_f
        kbg[b] = kb * eg_f
        qg[b] = q * eg_f
        kg[b] = k * ek_f
        gl[b] = eg_f[t - 1:t, :]
        kk = _dot_nt(jnp.concatenate([kb, q], axis=0), _head_block_rows(k, t))
        a[b] = jnp.where(i > j, kk[:t] * decay, 0.0)
        qk[b] = kk[t:] * decay

    def prod_all(xs, ys):
        return {b: _dot(xs[b], jnp.where(same_head, jnp.concatenate([ys[b]] * GDN_HEADS, axis=0), 0.0))
                for b in seqs}

    d = {b: eye - a[b] for b in seqs}
    p = a
    span = 2
    while span < t:
        p = prod_all(p, p)
        dp = prod_all(d, p)
        d = {b: d[b] + dp[b] for b in seqs}
        span *= 2

    u, w = {}, {}
    for b in seqs:
        rhs = jnp.concatenate([_head_block_rows(vb[b], t), _head_block_rows(kbg[b], t)], axis=1)
        uw = _dot(d[b], rhs)
        u[b], w[b] = uw[:, :GDN_V_DIM], uw[:, GDN_V_DIM:]

    zeros_s = jnp.zeros((GDN_DK, GDN_DV), BF16)
    wq = {}
    for b in seqs:
        for m in range(PK_NPAIR):
            s0 = s0_ref[b, 2 * m].astype(BF16)
            s1 = s0_ref[b, 2 * m + 1].astype(BF16)
            bd_s = jnp.concatenate([jnp.concatenate([s0, zeros_s], axis=1),
                                    jnp.concatenate([zeros_s, s1], axis=1)], axis=0)
            lhs = jnp.concatenate([w[b][:, pair(m)], qg[b][:, pair(m)]], axis=0).astype(BF16)
            wq[b, m] = jnp.dot(lhs, bd_s, preferred_element_type=F32)
    v_new = {b: u[b] - jnp.concatenate([wq[b, m][:t] for m in range(PK_NPAIR)], axis=1) for b in seqs}
    zeros_v = jnp.zeros((t, GDN_DV), F32)
    for b in seqs:
        o_ref[b] = (jnp.concatenate([wq[b, m][t:] for m in range(PK_NPAIR)], axis=1)
                    + _dot(qk[b], _head_block_rows(v_new[b], t)))
        for m in range(PK_NPAIR):
            vp = v_new[b][:, pair(m)]
            bd_v = jnp.concatenate([jnp.concatenate([vp[:, :GDN_DV], zeros_v], axis=1),
                                    jnp.concatenate([zeros_v, vp[:, GDN_DV:]], axis=1)], axis=0)
            kp = kg[b][:, pair(m)]
            kg_st = jnp.concatenate([kp[:, :GDN_DK], kp[:, GDN_DK:]], axis=0)
            upd = _dot_tn(kg_st, bd_v)
            glp = gl[b][:, pair(m)]
            sout_ref[b, 2 * m] = s0_ref[b, 2 * m] * glp[:, :GDN_DV] + upd[:, :GDN_DV]
            sout_ref[b, 2 * m + 1] = s0_ref[b, 2 * m + 1] * glp[:, GDN_DV:] + upd[:, GDN_DV:]


def _gdn_sample(q, k, v, gcb, grep, grow_flat, s0, bt):
    b, t, _ = q.shape
    assert t == V7X_SUBLANES, "one sublane tile of steps per sequence"
    cat = GDN_HEADS * t
    tok = lambda wd: pl.BlockSpec((bt, t, wd), lambda i: (i, 0, 0))
    state = pl.BlockSpec((bt, GDN_HEADS, GDN_DK, GDN_DV), lambda i: (i, 0, 0, 0))
    return pl.pallas_call(
        _gdn_sample_kernel,
        grid=(b // bt,),
        in_specs=[tok(GDN_QK_DIM), tok(GDN_QK_DIM), tok(GDN_V_DIM), tok(V7X_LANES), tok(cat),
                  pl.BlockSpec((bt, 1, cat), lambda i: (i, 0, 0)), state],
        out_specs=[tok(GDN_V_DIM), state],
        out_shape=[jax.ShapeDtypeStruct((b, t, GDN_V_DIM), F32),
                   jax.ShapeDtypeStruct((b, GDN_HEADS, GDN_DK, GDN_DV), F32)],
        compiler_params=pltpu.CompilerParams(dimension_semantics=("parallel",),
                                             vmem_limit_bytes=_vmem_limit(40 << 20)),
        name="gdn_sample",
    )(q, k, v, gcb, grep, grow_flat, s0)


def _swa_prompt_kernel(q_ref, kc_ref, kp_ref, vc_ref, vp_ref, sgb_ref, sink_ref, o_ref):
    w = WINDOW
    hd = SWA_HEAD_DIM
    nkeys = 2 * w
    ncols = 2 * w
    nwin = q_ref.shape[1] // w
    has_prev = pl.program_id(1) > 0
    kj = lax.broadcasted_iota(jnp.int32, (nkeys, ncols), 0)
    qi = lax.broadcasted_iota(jnp.int32, (nkeys, ncols), 1) % w
    mask_inner = ((kj >= w) & ((kj - w) <= qi)) | ((kj < w) & (kj > qi))
    masks = [mask_inner & ((kj >= w) | has_prev)] + [mask_inner] * (nwin - 1)
    lo_keys = lax.broadcasted_iota(jnp.int32, (nkeys, V7X_LANES), 1) < hd
    lo_q = lax.broadcasted_iota(jnp.int32, (w, V7X_LANES), 1) < hd
    scale = hd ** -0.5 * LOG2_E
    problems = [(t, h, gp) for t in range(nwin) for h in range(SWA_KV_HEADS)
                for gp in range(SWA_GROUP // 2)]

    kk, vbd = {}, {}
    for hp in range(SWA_KV_HEADS // 2):
        sl = slice(hp * V7X_LANES, (hp + 1) * V7X_LANES)
        k_all = jnp.concatenate([kp_ref[0, :, sl], kc_ref[0, :, sl]], axis=0)
        v_all = jnp.concatenate([vp_ref[0, :, sl], vc_ref[0, :, sl]], axis=0)
        kr_all = pltpu.roll(k_all, hd, 1)
        vr_all = pltpu.roll(v_all, hd, 1)
        for t in range(nwin):
            keys = slice(t * w, (t + 2) * w)
            kt, kr, vt, vr = k_all[keys], kr_all[keys], v_all[keys], vr_all[keys]
            for par, h in enumerate((2 * hp, 2 * hp + 1)):
                kk[t, h] = (jnp.where(lo_keys, kt, kr) if par == 0
                            else jnp.where(lo_keys, kr, kt)).astype(BF16)
                vv = jnp.where(lo_keys, vt, vr) if par == 0 else jnp.where(lo_keys, vr, vt)
                vbd[t, h] = jnp.concatenate([jnp.where(lo_keys, vv, 0.0), jnp.where(lo_keys, 0.0, vv)],
                                            axis=0).astype(BF16)
    s, pn = {}, {}

    def scores(t, h, gp):
        lo = (h * SWA_GROUP + 2 * gp) * hd
        qp = q_ref[0, t * w:(t + 1) * w, lo:lo + V7X_LANES]
        rq = jnp.concatenate([jnp.where(lo_q, qp, 0.0), jnp.where(lo_q, 0.0, qp)], axis=0).astype(BF16)
        sc = lax.dot_general(kk[t, h], rq, (((1,), (1,)), ((), ())), preferred_element_type=F32)
        s[t, h, gp] = jnp.where(masks[t], sc * scale, -jnp.inf)

    def softmax(t, h, gp):
        sink = sink_ref[h * (SWA_GROUP // 2) + gp] * LOG2_E
        m = jnp.maximum(jnp.max(s[t, h, gp], axis=0, keepdims=True), sink)
        p = jnp.exp2(s[t, h, gp] - m)
        denom = jnp.sum(p, axis=0, keepdims=True) + jnp.exp2(sink - m)
        p = (p * (1.0 / denom)).astype(BF16)
        pn[t, h, gp] = jnp.concatenate([p[:, :w], p[:, w:]], axis=0)

    def values(t, h, gp):
        lo = (h * SWA_GROUP + 2 * gp) * hd
        rows = slice(t * w, (t + 1) * w)
        o = lax.dot_general(pn[t, h, gp], vbd[t, h], (((0,), (0,)), ((), ())), preferred_element_type=F32)
        o_ref[0, rows, lo:lo + V7X_LANES] = o * sgb_ref[0, rows, lo:lo + V7X_LANES]

    for stage in (scores, softmax, values):
        for prob in problems:
            stage(*prob)


def _swa_prompt(q, k, v, sgb, sink_cols):
    b, l, _ = q.shape
    nwin = _token_tile(l // WINDOW, SWA_WINDOWS_PER_STEP)
    nb = l // (nwin * WINDOW)
    cur = lambda wd: pl.BlockSpec((1, nwin * WINDOW, wd), lambda i, j: (i, j, 0))
    prev = lambda wd: pl.BlockSpec((1, WINDOW, wd), lambda i, j: (i, jnp.maximum(j * nwin - 1, 0), 0))
    return pl.pallas_call(
        _swa_prompt_kernel,
        grid=(b, nb),
        in_specs=[cur(SWA_Q_DIM), cur(SWA_KV_DIM), prev(SWA_KV_DIM), cur(SWA_KV_DIM), prev(SWA_KV_DIM),
                  cur(SWA_Q_DIM), pl.BlockSpec(sink_cols.shape, lambda i, j: (0, 0, 0))],
        out_specs=cur(SWA_Q_DIM),
        out_shape=jax.ShapeDtypeStruct((b, l, SWA_Q_DIM), F32),
        compiler_params=pltpu.CompilerParams(dimension_semantics=("parallel", "parallel")),
        name="swa_prompt",
    )(q, k, k, v, v, sgb, sink_cols)


def _swa_sample_kernel(q_ref, kn_ref, vn_ref, kb_ref, vb_ref, sgb_ref, sink_ref,
                       o_ref, ko_ref, vo_ref, *, steps):
    t = steps
    nrows = kb_ref.shape[1]
    rows = SWA_GROUP * t
    qt_b = lax.broadcasted_iota(jnp.int32, (rows, nrows), 0) % t
    ki_b = lax.broadcasted_iota(jnp.int32, (rows, nrows), 1)
    mask_buf = ((nrows - ki_b + qt_b) < WINDOW)[None]
    qt_n = lax.broadcasted_iota(jnp.int32, (rows, t), 0) % t
    ki_n = lax.broadcasted_iota(jnp.int32, (rows, t), 1)
    mask_new = (ki_n <= qt_n)[None]
    scale = SWA_HEAD_DIM ** -0.5
    bdot = lambda a, b_, dims: lax.dot_general(a.astype(BF16), b_.astype(BF16), dims,
                                               preferred_element_type=F32)
    nt_dims = (((2,), (2,)), ((0,), (0,)))
    nn_dims = (((2,), (1,)), ((0,), (0,)))
    for h in range(SWA_KV_HEADS):
        klo = h * SWA_HEAD_DIM
        q_h = jnp.concatenate(
            [q_ref[:, :, (h * SWA_GROUP + g) * SWA_HEAD_DIM:(h * SWA_GROUP + g + 1) * SWA_HEAD_DIM]
             for g in range(SWA_GROUP)], axis=1)
        sink = sink_ref[h][None]
        k_b = kb_ref[:, :, klo:klo + SWA_HEAD_DIM]
        k_n = kn_ref[:, :, klo:klo + SWA_HEAD_DIM]
        v_b = vb_ref[:, :, klo:klo + SWA_HEAD_DIM]
        v_n = vn_ref[:, :, klo:klo + SWA_HEAD_DIM]
        s_b = jnp.where(mask_buf, bdot(q_h, k_b, nt_dims) * scale, -jnp.inf)
        s_n = jnp.where(mask_new, bdot(q_h, k_n, nt_dims) * scale, -jnp.inf)
        m = jnp.maximum(jnp.maximum(jnp.max(s_b, axis=-1, keepdims=True),
                                    jnp.max(s_n, axis=-1, keepdims=True)), sink)
        p_b = jnp.exp(s_b - m)
        p_n = jnp.exp(s_n - m)
        denom = (jnp.sum(p_b, axis=-1, keepdims=True) + jnp.sum(p_n, axis=-1, keepdims=True)
                 + jnp.exp(sink - m))
        o = (bdot(p_b, v_b, nn_dims) + bdot(p_n, v_n, nn_dims)) / denom
        for g in range(SWA_GROUP):
            lo = (h * SWA_GROUP + g) * SWA_HEAD_DIM
            o_ref[:, :, lo:lo + SWA_HEAD_DIM] = (o[:, g * t:(g + 1) * t]
                                                 * sgb_ref[:, :, lo:lo + SWA_HEAD_DIM])
    ko_ref[:, :nrows - t] = kb_ref[:, t:]
    ko_ref[:, nrows - t:] = kn_ref[...]
    vo_ref[:, :nrows - t] = vb_ref[:, t:]
    vo_ref[:, nrows - t:] = vn_ref[...]


def _swa_sample(q, k, v, k_buf, v_buf, sgb, sink_rows, bt):
    b, t, _ = q.shape
    nrows = k_buf.shape[1]
    new = lambda wd: pl.BlockSpec((bt, t, wd), lambda i: (i, 0, 0))
    buf = pl.BlockSpec((bt, nrows, SWA_KV_DIM), lambda i: (i, 0, 0))
    kern = functools.partial(_swa_sample_kernel, steps=t)
    return pl.pallas_call(
        kern,
        grid=(b // bt,),
        in_specs=[new(SWA_Q_DIM), new(SWA_KV_DIM), new(SWA_KV_DIM), buf, buf, new(SWA_Q_DIM),
                  pl.BlockSpec(sink_rows.shape, lambda i: (0, 0, 0))],
        out_specs=[new(SWA_Q_DIM), buf, buf],
        out_shape=[jax.ShapeDtypeStruct((b, t, SWA_Q_DIM), F32),
                   jax.ShapeDtypeStruct((b, nrows, SWA_KV_DIM), F32),
                   jax.ShapeDtypeStruct((b, nrows, SWA_KV_DIM), F32)],
        compiler_params=pltpu.CompilerParams(dimension_semantics=("parallel",)),
        name="swa_sample",
    )(q, k, v, k_buf, v_buf, sgb, sink_rows)


def _out_kernel(x_ref, og_ref, zga_ref, osg_ref, gn_ref, wo_ref, nw2_ref, win_ref, wout_ref, nf_ref, y_ref):
    gn = gn_ref[...]
    heads = []
    for h in range(GDN_HEADS):
        lo = h * GDN_DV
        heads.append(_rms(og_ref[:, lo:lo + GDN_DV], gn))
    mixed = jnp.concatenate(heads, axis=1) * zga_ref[...] + osg_ref[...]
    x2 = x_ref[...] + jnp.dot(mixed.astype(BF16), wo_ref[...], preferred_element_type=F32)
    x3 = _swiglu_half_step(x2, nw2_ref[...], win_ref, wout_ref)
    y_ref[...] = _rms(x3, nf_ref[...])


def _out(x1, og, zga, osg, gnorm, wo, nw2, win, wout, nf, tm):
    t = x1.shape[0]
    row = pl.BlockSpec((tm, D_MODEL), lambda i: (i, 0))
    wbytes = (wo.size + win.size + wout.size) * 2
    return pl.pallas_call(
        _out_kernel,
        grid=(t // tm,),
        in_specs=[row] * 4 + [_const_spec(gnorm.shape), _const_spec(wo.shape), _const_spec(nw2.shape),
                              _const_spec(win.shape), _const_spec(wout.shape), _const_spec(nf.shape)],
        out_specs=row,
        out_shape=jax.ShapeDtypeStruct((t, D_MODEL), F32),
        compiler_params=pltpu.CompilerParams(
            dimension_semantics=("parallel",),
            vmem_limit_bytes=_vmem_limit(wbytes + 18 * tm * D_MODEL * 4 + (8 << 20))),
        name="out",
    )(x1, og, zga, osg, gnorm, wo, nw2, win, wout, nf)


def _rope_tables(pos):
    inv = ROPE_THETA ** (-jnp.arange(0, ROPE_DIM, 2, dtype=F32) / ROPE_DIM)
    ang = pos.astype(F32)[:, None] * inv[None, :]
    cos, sin = jnp.cos(ang), jnp.sin(ang)
    n = pos.shape[0]
    pad = SWA_HEAD_DIM - ROPE_DIM
    one_head = lambda first, second, fill: jnp.concatenate(
        [first, second, jnp.full((n, pad), fill, F32)], axis=1)
    zeros = jnp.zeros_like(sin)
    cos_f = one_head(cos, cos, 1.0)
    sin_a = one_head(-sin, zeros, 0.0)
    sin_b = one_head(zeros, sin, 0.0)
    reps = V7X_LANES // SWA_HEAD_DIM
    return tuple(jnp.tile(t, (1, reps)) for t in (cos_f, sin_a, sin_b))


def _pad_lanes(v, width):
    return jnp.pad(v.astype(F32), (0, width - v.shape[0]))[None, :]


def _branch(x, pos, conv_buf, s0, k_buf, v_buf, p):
    b, l, _ = x.shape
    t = b * l
    is_prompt = conv_buf is None
    chunk = min(GDN_CHUNK, l)
    n = l // chunk
    tm = _token_tile(l if is_prompt else t, TOKEN_TILE)
    tables = _rope_tables(pos)
    if is_prompt:
        halo = None
        table_tiles = l // tm
    else:
        halo = jnp.pad(conv_buf, ((0, 0), (l - (CONV_W - 1), 0), (0, 0))).reshape(t, CONV_DIM)
        tables = tuple(jnp.tile(tb, (tm // l, 1)) for tb in tables)
        table_tiles = 1
    tm_conv = tm if is_prompt else _token_tile(t, TOKEN_TILE // 2)
    x1, qg, kg, vg, tail = _ffn_conv(x.reshape(t, D_MODEL), p['norm_ffn1'], p['w_ffn1_in'], p['w_ffn1_out'],
                                     p['norm_mix'], p['w_qkv'], p['conv_w'], halo,
                                     tm=tm_conv, seq_len=l, group=l)
    gcb, zga, qr, kr, vs, sgb = _mix(x1, p['norm_mix'], p['w_mix'], p['a_log_pad'], p['dt_bias_pad'],
                                     tables, tm=tm, table_tiles=table_tiles, chunk=chunk)
    seq = lambda a: a.reshape(b, l, a.shape[-1])
    qg, kg, vg, gcb, qr, kr, vs, sgb = (seq(a) for a in (qg, kg, vg, gcb, qr, kr, vs, sgb))
    new_conv = tail.reshape(b, -1, CONV_DIM)[:, -(CONV_W - 1):]

    gc = gcb[:, :, :GDN_HEADS]
    grow = gc.reshape(b, n, chunk, GDN_HEADS).transpose(0, 1, 3, 2)
    if is_prompt:
        assert chunk == PK_C and s0 is None
        u, w, qgd, kgd, qk, gl = _pk_pre(qg, kg, vg, gcb, grow.reshape(b, n, 1, PK_CAT),
                                         _token_tile(n, PK_CHUNKS_PER_STEP))
        og, new_s = _pk_scan(u, w, qgd, kgd, qk, gl)
        sink_cols = jnp.repeat(p['sinks'].reshape(SWA_Q_HEADS // 2, 2), WINDOW, axis=1)
        osg = _swa_prompt(qr, kr, vs, sgb, sink_cols.reshape(SWA_Q_HEADS // 2, 1, 2 * WINDOW))
        new_k, new_v = kr[:, l - WINDOW:], vs[:, l - WINDOW:]
    else:
        assert n == 1
        og, new_s = _gdn_sample(qg, kg, vg, gcb, jnp.repeat(gc, l, axis=-1),
                                grow.reshape(b, 1, GDN_HEADS * l), s0, _token_tile(b, 8))
        sink_rows = jnp.repeat(p['sinks'], l, axis=1).reshape(SWA_KV_HEADS, SWA_GROUP * l, 1)
        nrows = k_buf.shape[1]
        osg, new_k, new_v = _swa_sample(qr, kr, vs, k_buf.reshape(b, nrows, SWA_KV_DIM),
                                        v_buf.reshape(b, nrows, SWA_KV_DIM), sgb, sink_rows,
                                        _token_tile(b, 16))
    y = _out(x1, og.reshape(t, GDN_V_DIM), zga, osg.reshape(t, SWA_Q_DIM), p['gdn_norm'], p['w_out'],
             p['norm_ffn2'], p['w_ffn2_in'], p['w_ffn2_out'], p['norm_final'], tm)
    kv_shape = (b, new_k.shape[1], SWA_KV_HEADS, SWA_HEAD_DIM)
    return (y.reshape(b, l, D_MODEL), new_conv, new_s, new_k.reshape(kv_shape), new_v.reshape(kv_shape))


def _layer_params(l, norm_ffn1, w_ffn1_in, w_ffn1_out, norm_mix, w_in, conv_w, gdn_a_log, gdn_dt_bias,
                  gdn_norm, swa_sinks, w_out, norm_ffn2, w_ffn2_in, w_ffn2_out, norm_final):
    bounds = [0]
    for s in IN_SPLITS:
        bounds.append(bounds[-1] + s)
    w_in_t = jnp.swapaxes(w_in[l], 0, 1)
    col = lambda i: w_in_t[bounds[i]:bounds[i + 1]].astype(BF16)
    w_ab = jnp.pad(jnp.concatenate([col(2), col(3)], axis=0), ((0, V7X_LANES - 2 * GDN_HEADS), (0, 0)))
    row = lambda v: v.astype(F32)[None, :]
    return {
        'norm_ffn1': row(norm_ffn1[l]), 'w_ffn1_in': w_ffn1_in[l].astype(BF16),
        'w_ffn1_out': w_ffn1_out[l].astype(BF16),
        'norm_mix': row(norm_mix[l]),
        'w_qkv': col(0),
        'w_mix': [w_ab, col(1), col(7), col(4), col(5), col(6), col(8)],
        'conv_w': conv_w[l].astype(F32),
        'a_log_pad': _pad_lanes(gdn_a_log[l], V7X_LANES), 'dt_bias_pad': _pad_lanes(gdn_dt_bias[l], V7X_LANES),
        'gdn_norm': row(gdn_norm[l]),
        'sinks': swa_sinks[l].astype(F32).reshape(SWA_KV_HEADS, SWA_GROUP),
        'w_out': w_out[l].astype(BF16),
        'norm_ffn2': row(norm_ffn2[l]), 'w_ffn2_in': w_ffn2_in[l].astype(BF16),
        'w_ffn2_out': w_ffn2_out[l].astype(BF16),
        'norm_final': row(norm_final),
    }


def kernel(x_prompt, x_sample, state_conv, state_gdn, cache_swa_k, cache_swa_v, norm_ffn1, w_ffn1_in, w_ffn1_out, norm_mix, w_in, conv_w, gdn_a_log, gdn_dt_bias, gdn_norm, swa_sinks, w_out, norm_ffn2, w_ffn2_in, w_ffn2_out, norm_final):
    depth = w_in.shape[0]
    assert depth == 1, "the final norm is fused into the layer's last kernel"
    l_p = x_prompt.shape[1]
    l_s = x_sample.shape[1]
    pos_p = jnp.arange(l_p, dtype=jnp.int32)
    pos_s = PAST_LEN + jnp.arange(l_s, dtype=jnp.int32)
    p = _layer_params(0, norm_ffn1, w_ffn1_in, w_ffn1_out, norm_mix, w_in, conv_w, gdn_a_log, gdn_dt_bias,
                      gdn_norm, swa_sinks, w_out, norm_ffn2, w_ffn2_in, w_ffn2_out, norm_final)
    yp, c1, s1, k1, v1 = _branch(x_prompt, pos_p, None, None, None, None, p)
    ys, c2, s2, k2, v2 = _branch(x_sample, pos_s, state_conv[0], state_gdn[0], cache_swa_k[0],
                                 cache_swa_v[0], p)
    return (yp, ys, c1[None], c2[None], s1[None], s2[None], k1[None], k2[None], v1[None], v2[None])
```

```python
import functools

import jax
import jax.numpy as jnp
from jax import lax
from jax.experimental import pallas as pl
from jax.experimental.pallas import tpu as pltpu

D_MODEL = 1024
PAST_LEN = 16384
GDN_HEADS = 8
GDN_DK = 128
GDN_DV = 128
GDN_QK_DIM = GDN_HEADS * GDN_DK
GDN_V_DIM = GDN_HEADS * GDN_DV
CONV_W = 4
GDN_CHUNK = 64
CONV_DIM = 2 * GDN_QK_DIM + GDN_V_DIM
SWA_Q_HEADS = 16
SWA_KV_HEADS = 4
SWA_GROUP = SWA_Q_HEADS // SWA_KV_HEADS
SWA_HEAD_DIM = 64
SWA_Q_DIM = SWA_Q_HEADS * SWA_HEAD_DIM
SWA_KV_DIM = SWA_KV_HEADS * SWA_HEAD_DIM
WINDOW = 128
ROPE_DIM = SWA_HEAD_DIM // 4
ROPE_HALF = ROPE_DIM // 2
ROPE_THETA = 500000.0
D_FF = 2816
EPS = 1e-6
LOG2_E = 1.4426950408889634
IN_SPLITS = (CONV_DIM, GDN_V_DIM, GDN_HEADS, GDN_HEADS, SWA_Q_DIM, SWA_KV_DIM, SWA_KV_DIM, D_MODEL, D_MODEL)

V7X_LANES = 128
V7X_SUBLANES = 8
V7X_MXU_DIM = 256
V7X_VMEM_BYTES = 64 * 1024 * 1024

TOKEN_TILE = 2 * V7X_MXU_DIM
FFN_CHUNK = V7X_MXU_DIM
CONV_PIECE = V7X_MXU_DIM
SOLVE_BASE = V7X_SUBLANES
SWA_WINDOWS_PER_STEP = 4

F32 = jnp.float32
BF16 = jnp.bfloat16


def _dot(a, b):
    return jnp.dot(a.astype(BF16), b.astype(BF16), preferred_element_type=F32)


def _dot_nt(a, b):
    return lax.dot_general(a.astype(BF16), b.astype(BF16), (((1,), (1,)), ((), ())),
                           preferred_element_type=F32)


def _dot_tn(a, b):
    return lax.dot_general(a.astype(BF16), b.astype(BF16), (((0,), (0,)), ((), ())),
                           preferred_element_type=F32)


def _proj(xn, w_t):
    return lax.dot_general(xn, w_t, (((1,), (1,)), ((), ())), preferred_element_type=F32)


def _rms(x, w):
    return x * lax.rsqrt(jnp.mean(x * x, axis=-1, keepdims=True) + EPS) * w


def _sigmoid(x):
    return 0.5 * jnp.tanh(0.5 * x) + 0.5


def _silu(x):
    h = 0.5 * x
    return h * jnp.tanh(h) + h


def _vmem_limit(nbytes):
    return int(min(V7X_VMEM_BYTES - (4 << 20), max(nbytes, 16 << 20)))


def _const_spec(shape):
    nd = len(shape)
    return pl.BlockSpec(shape, lambda *_: (0,) * nd, pipeline_mode=pl.Buffered(1))


def _token_tile(t, want):
    tm = min(want, t)
    while t % tm:
        tm //= 2
    return tm


FFN_NCHUNK = D_FF // FFN_CHUNK


def _swiglu_act(xn, win_ref, c):
    lo = c * FFN_CHUNK
    g = jnp.dot(xn, win_ref[:, lo:lo + FFN_CHUNK], preferred_element_type=F32)
    u = jnp.dot(xn, win_ref[:, D_FF + lo:D_FF + lo + FFN_CHUNK], preferred_element_type=F32)
    return (_silu(g) * u).astype(BF16)


def _swiglu_half_step(x, norm_w, win_ref, wout_ref):
    xn = _rms(x, norm_w).astype(BF16)
    act = jnp.concatenate([_swiglu_act(xn, win_ref, c) for c in range(FFN_NCHUNK)], axis=1)
    return x + 0.5 * jnp.dot(act, wout_ref[...], preferred_element_type=F32)


def _ffn_conv_kernel(*refs, tiles_per_seq, group):
    carried = tiles_per_seq > 0
    if carried:
        (x_ref, nw1_ref, win_ref, wout_ref, nw_ref, wqkv_ref, cw_ref,
         x1_ref, qg_ref, kg_ref, vg_ref, tail_ref, xn_ref, carry_ref) = refs
    else:
        (x_ref, nw1_ref, win_ref, wout_ref, nw_ref, wqkv_ref, cw_ref, halo_ref,
         x1_ref, qg_ref, kg_ref, vg_ref, tail_ref, xn_ref) = refs
    rows = x_ref.shape[0]
    step = pl.program_id(0)
    n_tiles = pl.num_programs(0) - 1
    if carried:
        seq_start = ((step - 1) % tiles_per_seq) == 0
    else:
        pos = lax.broadcasted_iota(jnp.int32, (rows, 1), 0) % group

    def conv_piece(c):
        cols = slice(c * CONV_PIECE, (c + 1) * CONV_PIECE)
        raw = _proj(xn_ref[...], wqkv_ref[cols, :])
        cw = cw_ref[:, cols]
        acc = raw * cw[CONV_W - 1:CONV_W, :]
        if carried:
            halo = jnp.where(seq_start, 0.0, carry_ref[:, cols])
            ext = jnp.concatenate([halo, raw], axis=0)
            for s in range(1, CONV_W):
                acc = acc + pltpu.roll(ext, s, 0)[V7X_SUBLANES:] * cw[CONV_W - 1 - s:CONV_W - s, :]
            carry_ref[:, cols] = raw[rows - V7X_SUBLANES:]
            tail_ref[0, :, cols] = raw[rows - V7X_SUBLANES:]
        else:
            halo = halo_ref[:, cols]
            for s in range(1, CONV_W):
                shifted = jnp.where(pos >= s, pltpu.roll(raw, s, 0),
                                    pltpu.roll(halo, (rows + s - group) % rows, 0))
                acc = acc + shifted * cw[CONV_W - 1 - s:CONV_W - s, :]
            tail_ref[:, cols] = raw
        conv = _silu(acc)
        part, lo = divmod(c * CONV_PIECE, GDN_QK_DIM)
        out_ref = (qg_ref, kg_ref, vg_ref)[part]
        if part == 2:
            out_ref[:, lo:lo + CONV_PIECE] = conv.astype(BF16)
        else:
            for h in range(CONV_PIECE // GDN_DK):
                ch = conv[:, h * GDN_DK:(h + 1) * GDN_DK]
                out_ref[:, lo + h * GDN_DK:lo + (h + 1) * GDN_DK] = (
                    ch * lax.rsqrt(jnp.sum(ch * ch, axis=-1, keepdims=True) + EPS)).astype(BF16)

    def body(do_ffn, do_conv):
        n_pieces = CONV_DIM // CONV_PIECE
        n_out = D_MODEL // FFN_CHUNK
        slots = FFN_NCHUNK + n_out
        due = [((k + 1) * n_pieces) // slots for k in range(slots)] if do_ffn else [n_pieces]
        done = 0
        if do_ffn:
            x = x_ref[...]
            xn = _rms(x, nw1_ref[...]).astype(BF16)
            acts, outs = [], []
        for k, upto in enumerate(due):
            if do_ffn and k < FFN_NCHUNK:
                acts.append(_swiglu_act(xn, win_ref, k))
            elif do_ffn:
                if k == FFN_NCHUNK:
                    act = jnp.concatenate(acts, axis=1)
                lo = (k - FFN_NCHUNK) * FFN_CHUNK
                outs.append(jnp.dot(act, wout_ref[:, lo:lo + FFN_CHUNK], preferred_element_type=F32))
            if do_conv:
                for c in range(done, upto):
                    conv_piece(c)
                done = upto
        if do_ffn:
            x1 = x + 0.5 * jnp.concatenate(outs, axis=1)
            x1_ref[...] = x1
            xn_ref[...] = _rms(x1, nw_ref[...]).astype(BF16)

    @pl.when(step == 0)
    def _():
        if carried:
            carry_ref[...] = jnp.zeros_like(carry_ref)
        body(True, False)

    @pl.when((step > 0) & (step < n_tiles))
    def _():
        body(True, True)

    @pl.when(step == n_tiles)
    def _():
        body(False, True)


def _ffn_conv(x, norm1, win, wout, norm_w, wqkv, cw, halo, *, tm, seq_len, group):
    t = x.shape[0]
    carried = halo is None
    tiles_per_seq = seq_len // tm if carried else 0
    n_tiles = t // tm
    cur = lambda n: pl.BlockSpec((tm, n), lambda i: (jnp.minimum(i, n_tiles - 1), 0))
    prev = lambda n: pl.BlockSpec((tm, n), lambda i: (jnp.maximum(i - 1, 0), 0))
    weights = [win, wout, wqkv]
    in_specs = [cur(D_MODEL), _const_spec((1, D_MODEL)), _const_spec(win.shape), _const_spec(wout.shape),
                _const_spec((1, D_MODEL)), _const_spec(wqkv.shape), _const_spec(cw.shape)]
    args = [x, norm1, win, wout, norm_w, wqkv, cw]
    out_specs = [cur(D_MODEL), prev(GDN_QK_DIM), prev(GDN_QK_DIM), prev(GDN_V_DIM)]
    out_shape = [jax.ShapeDtypeStruct((t, D_MODEL), F32)] + [
        jax.ShapeDtypeStruct((t, n), BF16) for n in (GDN_QK_DIM, GDN_QK_DIM, GDN_V_DIM)]
    scratch = [pltpu.VMEM((tm, D_MODEL), BF16)]
    if carried:
        nseq = t // seq_len
        out_specs.append(pl.BlockSpec((1, V7X_SUBLANES, CONV_DIM),
                                      lambda i: (jnp.maximum(i - 1, 0) // tiles_per_seq, 0, 0)))
        out_shape.append(jax.ShapeDtypeStruct((nseq, V7X_SUBLANES, CONV_DIM), F32))
        scratch.append(pltpu.VMEM((V7X_SUBLANES, CONV_DIM), F32))
    else:
        wide = pl.BlockSpec((tm, CONV_DIM), lambda i: (jnp.maximum(i - 1, 0), 0),
                            pipeline_mode=pl.Buffered(1))
        in_specs.append(wide)
        args.append(halo)
        out_specs.append(wide)
        out_shape.append(jax.ShapeDtypeStruct((t, CONV_DIM), F32))
    wbytes = sum(w.size for w in weights) * 2
    kern = functools.partial(_ffn_conv_kernel, tiles_per_seq=tiles_per_seq, group=group)
    return pl.pallas_call(
        kern,
        grid=(n_tiles + 1,),
        in_specs=in_specs,
        out_specs=out_specs,
        out_shape=out_shape,
        scratch_shapes=scratch,
        compiler_params=pltpu.CompilerParams(
            dimension_semantics=("arbitrary",),
            vmem_limit_bytes=_vmem_limit(wbytes + 32 * tm * D_MODEL * 4 + (4 << 20))),
        name="ffn_conv",
    )(*args)


def _rope(x, cos_f, sin_a, sin_b):
    width = x.shape[1]
    reps = width // V7X_LANES
    tile = lambda t: jnp.concatenate([t] * reps, axis=1) if reps > 1 else t
    return (x * tile(cos_f) + pltpu.roll(x, width - ROPE_HALF, 1) * tile(sin_a)
            + pltpu.roll(x, ROPE_HALF, 1) * tile(sin_b))


def _mix_kernel(*refs, chunk, kv_t):
    (x_ref, nw_ref, wab_ref, wz_ref, wga_ref, wqs_ref, wks_ref, wvs_ref, wgb_ref,
     alog_ref, dtb_ref, cos_ref, sina_ref, sinb_ref) = refs[:14]
    if kv_t:
        cost_ref, sinat_ref, sinbt_ref = refs[14:17]
        gcb_ref, zga_ref, qr_ref, kr_ref, vs_ref, sgb_ref, krt_ref, vst_ref = refs[17:]
    else:
        gcb_ref, zga_ref, qr_ref, kr_ref, vs_ref, sgb_ref = refs[14:]
    rows = x_ref.shape[0]
    xn = _rms(x_ref[...], nw_ref[...]).astype(BF16)

    ab = _proj(xn, wab_ref[...])
    sp_in = ab + dtb_ref[...]
    softplus = jnp.maximum(sp_in, 0.0) + jnp.log1p(jnp.exp(-jnp.abs(sp_in)))
    gc = -jnp.exp(alog_ref[...]) * softplus
    cpos = lax.broadcasted_iota(jnp.int32, (rows, 1), 0) % chunk
    s = 1
    while s < chunk:
        gc = gc + jnp.where(cpos >= s, pltpu.roll(gc, s, 0), 0.0)
        s *= 2
    lane = lax.broadcasted_iota(jnp.int32, (1, V7X_LANES), 1)
    gcb_ref[...] = jnp.where(lane < GDN_HEADS, gc, jax.nn.sigmoid(ab))

    zga_ref[...] = _silu(_proj(xn, wz_ref[...])) * _sigmoid(_proj(xn, wga_ref[...]))
    cos_f, sin_a, sin_b = cos_ref[...], sina_ref[...], sinb_ref[...]
    qr_ref[...] = _rope(_proj(xn, wqs_ref[...]), cos_f, sin_a, sin_b)
    kr_ref[...] = _rope(_proj(xn, wks_ref[...]), cos_f, sin_a, sin_b)
    vs_ref[...] = _proj(xn, wvs_ref[...])
    sgb_ref[...] = _sigmoid(_proj(xn, wgb_ref[...]))
    if kv_t:
        ks_t = lax.dot_general(wks_ref[...], xn, (((1,), (1,)), ((), ())), preferred_element_type=F32)
        krt_ref[...] = (ks_t * cost_ref[...] + pltpu.roll(ks_t, SWA_KV_DIM - ROPE_HALF, 0) * sinat_ref[...]
                        + pltpu.roll(ks_t, ROPE_HALF, 0) * sinbt_ref[...])
        vst_ref[...] = lax.dot_general(wvs_ref[...], xn, (((1,), (1,)), ((), ())),
                                       preferred_element_type=F32)


def _mix(x, norm_w, weights, alog, dtb, tables, tables_t, *, tm, table_tiles, chunk):
    t = x.shape[0]
    kv_t = tables_t is not None
    row = lambda n: pl.BlockSpec((tm, n), lambda i: (i, 0))
    tab = pl.BlockSpec((tm, V7X_LANES), lambda i: (i % table_tiles, 0))
    widths = (V7X_LANES, GDN_V_DIM, SWA_Q_DIM, SWA_KV_DIM, SWA_KV_DIM, D_MODEL)
    in_specs = ([row(D_MODEL), _const_spec((1, D_MODEL))] + [_const_spec(w.shape) for w in weights]
                + [_const_spec(alog.shape), _const_spec(dtb.shape), tab, tab, tab])
    args = [x, norm_w, *weights, alog, dtb, *tables]
    out_specs = [row(n) for n in widths]
    out_shape = [jax.ShapeDtypeStruct((t, n), F32) for n in widths]
    if kv_t:
        in_specs += [_const_spec((SWA_KV_DIM, tm))] * 3
        args += list(tables_t)
        out_specs += [pl.BlockSpec((SWA_KV_DIM, tm), lambda i: (0, i))] * 2
        out_shape += [jax.ShapeDtypeStruct((SWA_KV_DIM, t), F32)] * 2
    wbytes = sum(w.size for w in weights) * 2
    return pl.pallas_call(
        functools.partial(_mix_kernel, chunk=chunk, kv_t=kv_t),
        grid=(t // tm,),
        in_specs=in_specs,
        out_specs=out_specs,
        out_shape=out_shape,
        compiler_params=pltpu.CompilerParams(
            dimension_semantics=("parallel",),
            vmem_limit_bytes=_vmem_limit(wbytes + 36 * tm * D_MODEL * 4 + (4 << 20))),
        name="mix",
    )(*args)


PK_C = GDN_CHUNK
PK_CAT = GDN_HEADS * PK_C
PK_PAIR = 2 * GDN_DK
PK_NPAIR = GDN_HEADS // 2
PK_GROUP = V7X_MXU_DIM // PK_C
PK_CHUNKS_PER_STEP = 8
PK_SCAN_CHUNKS_PER_STEP = 4


def _pk_masks():
    c = PK_C
    i = lax.broadcasted_iota(jnp.int32, (c, PK_CAT), 0)
    j = lax.broadcasted_iota(jnp.int32, (c, PK_CAT), 1) % c
    lo = (lax.broadcasted_iota(jnp.int32, (c, V7X_LANES), 1) < c).astype(BF16)
    return i, j, (lo, 1 - lo)


def _pk_prod(x, y, bd_mask):
    lo, hi = bd_mask
    zero = jnp.zeros((PK_C, V7X_LANES), BF16)
    outs = []
    for g in range(GDN_HEADS // PK_GROUP):
        sl = slice(g * V7X_MXU_DIM, (g + 1) * V7X_MXU_DIM)
        yb = y[:, sl].astype(BF16)
        t0, t1 = yb[:, :V7X_LANES], yb[:, V7X_LANES:]
        bd = jnp.concatenate([jnp.concatenate([t0 * lo, zero], axis=1),
                              jnp.concatenate([t0 * hi, zero], axis=1),
                              jnp.concatenate([zero, t1 * lo], axis=1),
                              jnp.concatenate([zero, t1 * hi], axis=1)], axis=0)
        outs.append(jnp.dot(x[:, sl].astype(BF16), bd, preferred_element_type=F32))
    return jnp.concatenate(outs, axis=1)


def _pk_solve(a_list, i, j, bd_mask):
    prod_all = lambda xs, ys: [_pk_prod(x, y, bd_mask) for x, y in zip(xs, ys)]
    eye = (i == j).astype(F32)
    b = SOLVE_BASE
    same = (i // b) == (j // b)
    p = [jnp.where(same, a, 0.0) for a in a_list]
    d = [eye - x for x in p]
    span = 2
    while span < b:
        p = prod_all(p, p)
        d = [x + y for x, y in zip(d, prod_all(d, p))]
        span *= 2
    while b < PK_C:
        same2 = (i // (2 * b)) == (j // (2 * b))
        sel = same2 & jnp.logical_not(same)
        off = [jnp.where(sel, a, 0.0) for a in a_list]
        d = [x - y for x, y in zip(d, prod_all(d, prod_all(off, d)))]
        same = same2
        b *= 2
    return d


def _pk_pre_kernel(q_ref, k_ref, v_ref, gcb_ref, grow_ref,
                   u_ref, w_ref, qg_ref, kg_ref, qk_ref, gl_ref, *, nc):
    c = PK_C
    i, j, bd_mask = _pk_masks()
    lane_lo = lax.broadcasted_iota(jnp.int32, (c, V7X_LANES), 1) < c
    zeros_k = jnp.zeros((c, GDN_DK), BF16)
    a_list, vb_list, kbg_list = [], [], []
    for ch in range(nc):
        rows = slice(ch * c, (ch + 1) * c)
        gcb = gcb_ref[0, rows, :]
        g_end = gcb[c - 1:c, :]
        eg_s = jnp.exp(gcb)
        ek_s = jnp.exp(g_end - gcb)
        col = lambda arr, n: jnp.broadcast_to(arr[:, n:n + 1], (c, V7X_LANES))
        full = lambda arr, base: jnp.concatenate([col(arr, base + h) for h in range(GDN_HEADS)], axis=1)
        gc_cols = [col(gcb, h) for h in range(GDN_HEADS)]
        beta_f = full(gcb, GDN_HEADS)
        eg_f = full(eg_s, 0)
        ek_f = full(ek_s, 0)
        gc_cat = jnp.concatenate([jnp.where(lane_lo, gc_cols[2 * m], gc_cols[2 * m + 1])
                                  for m in range(PK_NPAIR)], axis=1)
        decay = jnp.exp(jnp.where(i >= j, gc_cat - grow_ref[0, ch], -jnp.inf))
        q = q_ref[0, rows, :].astype(F32) * (GDN_DK ** -0.5)
        k = k_ref[0, rows, :].astype(F32)
        kb = k * beta_f
        vb_list.append((v_ref[0, rows, :].astype(F32) * beta_f).astype(BF16))
        kbg_list.append((kb * eg_f).astype(BF16))
        qg_ref[0, rows, :] = (q * eg_f).astype(BF16)
        kg_ref[0, rows, :] = (k * ek_f).astype(BF16)
        gl_ref[0, ch] = eg_f[c - 1:c, :]
        kk = []
        for m in range(PK_NPAIR):
            sl = slice(m * PK_PAIR, (m + 1) * PK_PAIR)
            lhs = jnp.concatenate([kb[:, sl], q[:, sl]], axis=0).astype(BF16)
            k2 = k[:, sl].astype(BF16)
            rhs_nt = jnp.concatenate(
                [jnp.concatenate([k2[:, :GDN_DK], zeros_k], axis=1),
                 jnp.concatenate([zeros_k, k2[:, GDN_DK:]], axis=1)], axis=0)
            kk.append(lax.dot_general(lhs, rhs_nt, (((1,), (1,)), ((), ())), preferred_element_type=F32))
        a_raw = jnp.concatenate([x[:c] for x in kk], axis=1)
        qk_raw = jnp.concatenate([x[c:] for x in kk], axis=1)
        a_list.append(jnp.where(i > j, a_raw * decay, 0.0))
        qk_ref[0, rows, :] = (qk_raw * decay).astype(BF16)
    t_list = _pk_solve(a_list, i, j, bd_mask)
    for ch in range(nc):
        rows = slice(ch * c, (ch + 1) * c)
        t = t_list[ch]
        for m in range(PK_NPAIR):
            tp = t[:, m * V7X_LANES:(m + 1) * V7X_LANES]
            lhs = jnp.concatenate([jnp.where(lane_lo, tp, 0.0), jnp.where(lane_lo, 0.0, tp)],
                                  axis=0).astype(BF16)
            h0, h1 = 2 * m, 2 * m + 1
            hs = lambda h: slice(h * GDN_DK, (h + 1) * GDN_DK)
            rhs = jnp.concatenate(
                [jnp.concatenate([vb_list[ch][:, hs(h0)], kbg_list[ch][:, hs(h0)]], axis=1),
                 jnp.concatenate([vb_list[ch][:, hs(h1)], kbg_list[ch][:, hs(h1)]], axis=1)], axis=0)
            uw = jnp.dot(lhs, rhs, preferred_element_type=F32)
            u_ref[0, rows, hs(h0)] = uw[:c, :GDN_DV]
            w_ref[0, rows, hs(h0)] = uw[:c, GDN_DV:].astype(BF16)
            u_ref[0, rows, hs(h1)] = uw[c:, :GDN_DV]
            w_ref[0, rows, hs(h1)] = uw[c:, GDN_DV:].astype(BF16)


def _pk_pre(q, k, v, gcb, grow_flat, nc):
    b, l, _ = q.shape
    tl = nc * PK_C
    tok = lambda wd: pl.BlockSpec((1, tl, wd), lambda bi, ti: (bi, ti, 0))
    per_chunk = lambda wd: pl.BlockSpec((1, nc, 1, wd), lambda bi, ti: (bi, ti, 0, 0))
    kern = functools.partial(_pk_pre_kernel, nc=nc)
    n = l // PK_C
    return pl.pallas_call(
        kern,
        grid=(b, l // tl),
        in_specs=[tok(GDN_QK_DIM), tok(GDN_QK_DIM), tok(GDN_V_DIM), tok(V7X_LANES), per_chunk(PK_CAT)],
        out_specs=[tok(GDN_V_DIM), tok(GDN_QK_DIM), tok(GDN_QK_DIM), tok(GDN_QK_DIM), tok(PK_CAT),
                   per_chunk(GDN_V_DIM)],
        out_shape=[jax.ShapeDtypeStruct((b, l, GDN_V_DIM), F32),
                   jax.ShapeDtypeStruct((b, l, GDN_QK_DIM), BF16),
                   jax.ShapeDtypeStruct((b, l, GDN_QK_DIM), BF16),
                   jax.ShapeDtypeStruct((b, l, GDN_QK_DIM), BF16),
                   jax.ShapeDtypeStruct((b, l, PK_CAT), BF16),
                   jax.ShapeDtypeStruct((b, n, 1, GDN_V_DIM), F32)],
        compiler_params=pltpu.CompilerParams(dimension_semantics=("parallel", "parallel"),
                                             vmem_limit_bytes=_vmem_limit(48 << 20)),
        name="gdn_pre_packed",
    )(q, k, v, gcb, grow_flat)


def _pk_scan_kernel(u_ref, w_ref, qg_ref, kg_ref, qk_ref, gl_ref, o_ref, sout_ref, s_ref):
    c = PK_C
    nb = u_ref.shape[0]
    step = pl.program_id(0)

    @pl.when(step == 0)
    def _():
        s_ref[...] = jnp.zeros_like(s_ref)

    zeros_s = jnp.zeros((GDN_DK, GDN_DV), BF16)
    zeros_v = jnp.zeros((c, GDN_DV), BF16)
    chains = [(b, m) for b in range(nb) for m in range(PK_NPAIR)]
    pair = lambda m: slice(m * PK_PAIR, (m + 1) * PK_PAIR)
    state = {bm: s_ref[bm] for bm in chains}

    for ch in range(u_ref.shape[1] // c):
        rows = slice(ch * c, (ch + 1) * c)
        wq, bd_v = {}, {}

        def state_products(b, m):
            sb = state[b, m].astype(BF16)
            bd_s = jnp.concatenate([jnp.concatenate([sb[:, :GDN_DV], zeros_s], axis=1),
                                    jnp.concatenate([zeros_s, sb[:, GDN_DV:]], axis=1)], axis=0)
            lhs = jnp.concatenate([w_ref[b, rows, pair(m)], qg_ref[b, rows, pair(m)]], axis=0)
            wq[b, m] = jnp.dot(lhs, bd_s, preferred_element_type=F32)

        def new_values(b, m):
            v_new = (u_ref[b, rows, pair(m)] - wq[b, m][:c]).astype(BF16)
            bd_v[b, m] = jnp.concatenate([jnp.concatenate([v_new[:, :GDN_DV], zeros_v], axis=1),
                                          jnp.concatenate([zeros_v, v_new[:, GDN_DV:]], axis=1)], axis=0)

        def outputs_and_update(b, m):
            qk_p = qk_ref[b, rows, m * V7X_LANES:(m + 1) * V7X_LANES]
            o_ref[b, rows, pair(m)] = wq[b, m][c:] + jnp.dot(qk_p, bd_v[b, m], preferred_element_type=F32)
            kg_p = kg_ref[b, rows, pair(m)]
            kg_st = jnp.concatenate([kg_p[:, :GDN_DK], kg_p[:, GDN_DK:]], axis=0)
            upd = lax.dot_general(kg_st, bd_v[b, m], (((0,), (0,)), ((), ())),
                                  preferred_element_type=F32)
            state[b, m] = state[b, m] * gl_ref[b, ch, :, pair(m)] + upd

        for stage in (state_products, new_values, outputs_and_update):
            for b, m in chains:
                stage(b, m)

    for bm in chains:
        s_ref[bm] = state[bm]

    @pl.when(step == pl.num_programs(0) - 1)
    def _():
        for b, m in chains:
            s_c = s_ref[b, m]
            sout_ref[b, 2 * m] = s_c[:, :GDN_DV]
            sout_ref[b, 2 * m + 1] = s_c[:, GDN_DV:]


def _pk_scan(u, w, qg, kg, qk, gl):
    b, l, _ = u.shape
    n = l // PK_C
    per = _token_tile(n, PK_SCAN_CHUNKS_PER_STEP)
    tok = lambda wd: pl.BlockSpec((b, per * PK_C, wd), lambda s: (0, s, 0))
    return pl.pallas_call(
        _pk_scan_kernel,
        grid=(n // per,),
        in_specs=[tok(GDN_V_DIM), tok(GDN_QK_DIM), tok(GDN_QK_DIM), tok(GDN_QK_DIM), tok(PK_CAT),
                  pl.BlockSpec((b, per, 1, GDN_V_DIM), lambda s: (0, s, 0, 0))],
        out_specs=[tok(GDN_V_DIM),
                   pl.BlockSpec((b, GDN_HEADS, GDN_DK, GDN_DV), lambda s: (0, 0, 0, 0))],
        out_shape=[jax.ShapeDtypeStruct((b, l, GDN_V_DIM), F32),
                   jax.ShapeDtypeStruct((b, GDN_HEADS, GDN_DK, GDN_DV), F32)],
        scratch_shapes=[pltpu.VMEM((b, PK_NPAIR, GDN_DK, 2 * GDN_DV), F32)],
        compiler_params=pltpu.CompilerParams(dimension_semantics=("arbitrary",)),
        name="gdn_scan_packed",
    )(u, w, qg, kg, qk, gl)


def _head_block_rows(x, t):
    width = x.shape[1] // GDN_HEADS
    z = jnp.zeros((t, width), x.dtype)
    rows = []
    for h in range(GDN_HEADS):
        rows.append(jnp.concatenate([z] * h + [x[:, h * width:(h + 1) * width]]
                                    + [z] * (GDN_HEADS - 1 - h), axis=1))
    return jnp.concatenate(rows, axis=0)


def _gdn_sample_kernel(q_ref, k_ref, v_ref, gcb_ref, grep_ref, grow_ref, s0_ref, o_ref, sout_ref):
    bt, t, _ = q_ref.shape
    cat = GDN_HEADS * t
    i = lax.broadcasted_iota(jnp.int32, (t, cat), 0)
    j = lax.broadcasted_iota(jnp.int32, (t, cat), 1) % t
    r = lax.broadcasted_iota(jnp.int32, (cat, cat), 0) // t
    l = lax.broadcasted_iota(jnp.int32, (cat, cat), 1) // t
    same_head = r == l
    eye = (i == j).astype(F32)
    seqs = range(bt)
    pair = lambda m: slice(m * PK_PAIR, (m + 1) * PK_PAIR)

    vb, kbg, qg, kg, gl, a, qk = {}, {}, {}, {}, {}, {}, {}
    for b in seqs:
        gcb = gcb_ref[b]
        g_end = gcb[t - 1:t, :]
        eg_s = jnp.exp(gcb)
        ek_s = jnp.exp(g_end - gcb)
        col = lambda arr, n: jnp.broadcast_to(arr[:, n:n + 1], (t, V7X_LANES))
        full = lambda arr, base: jnp.concatenate([col(arr, base + h) for h in range(GDN_HEADS)], axis=1)
        beta_f, eg_f, ek_f = full(gcb, GDN_HEADS), full(eg_s, 0), full(ek_s, 0)
        decay = jnp.exp(jnp.where(i >= j, grep_ref[b] - grow_ref[b], -jnp.inf))
        q = q_ref[b].astype(F32) * (GDN_DK ** -0.5)
        k = k_ref[b].astype(F32)
        kb = k * beta_f
        vb[b] = v_ref[b].astype(F32) * beta_f
        kbg[b] = kb * eg_f
        qg[b] = q * eg_f
        kg[b] = k * ek_f
        gl[b] = eg_f[t - 1:t, :]
        kk = _dot_nt(jnp.concatenate([kb, q], axis=0), _head_block_rows(k, t))
        a[b] = jnp.where(i > j, kk[:t] * decay, 0.0)
        qk[b] = kk[t:] * decay

    def prod_all(xs, ys):
        return {b: _dot(xs[b], jnp.where(same_head, jnp.concatenate([ys[b]] * GDN_HEADS, axis=0), 0.0))
                for b in seqs}

    d = {b: eye - a[b] for b in seqs}
    p = a
    span = 2
    while span < t:
        p = prod_all(p, p)
        dp = prod_all(d, p)
        d = {b: d[b] + dp[b] for b in seqs}
        span *= 2

    u, w = {}, {}
    for b in seqs:
        rhs = jnp.concatenate([_head_block_rows(vb[b], t), _head_block_rows(kbg[b], t)], axis=1)
        uw = _dot(d[b], rhs)
        u[b], w[b] = uw[:, :GDN_V_DIM], uw[:, GDN_V_DIM:]

    zeros_s = jnp.zeros((GDN_DK, GDN_DV), BF16)
    wq = {}
    for b in seqs:
        for m in range(PK_NPAIR):
            s0 = s0_ref[b, 2 * m].astype(BF16)
            s1 = s0_ref[b, 2 * m + 1].astype(BF16)
            bd_s = jnp.concatenate([jnp.concatenate([s0, zeros_s], axis=1),
                                    jnp.concatenate([zeros_s, s1], axis=1)], axis=0)
            lhs = jnp.concatenate([w[b][:, pair(m)], qg[b][:, pair(m)]], axis=0).astype(BF16)
            wq[b, m] = jnp.dot(lhs, bd_s, preferred_element_type=F32)
    v_new = {b: u[b] - jnp.concatenate([wq[b, m][:t] for m in range(PK_NPAIR)], axis=1) for b in seqs}
    zeros_v = jnp.zeros((t, GDN_DV), F32)
    for b in seqs:
        o_ref[b] = (jnp.concatenate([wq[b, m][t:] for m in range(PK_NPAIR)], axis=1)
                    + _dot(qk[b], _head_block_rows(v_new[b], t)))
        for m in range(PK_NPAIR):
            vp = v_new[b][:, pair(m)]
            bd_v = jnp.concatenate([jnp.concatenate([vp[:, :GDN_DV], zeros_v], axis=1),
                                    jnp.concatenate([zeros_v, vp[:, GDN_DV:]], axis=1)], axis=0)
            kp = kg[b][:, pair(m)]
            kg_st = jnp.concatenate([kp[:, :GDN_DK], kp[:, GDN_DK:]], axis=0)
            upd = _dot_tn(kg_st, bd_v)
            glp = gl[b][:, pair(m)]
            sout_ref[b, 2 * m] = s0_ref[b, 2 * m] * glp[:, :GDN_DV] + upd[:, :GDN_DV]
            sout_ref[b, 2 * m + 1] = s0_ref[b, 2 * m + 1] * glp[:, GDN_DV:] + upd[:, GDN_DV:]


def _gdn_sample(q, k, v, gcb, grep, grow_flat, s0, bt):
    b, t, _ = q.shape
    assert t == V7X_SUBLANES, "one sublane tile of steps per sequence"
    cat = GDN_HEADS * t
    tok = lambda wd: pl.BlockSpec((bt, t, wd), lambda i: (i, 0, 0))
    state = pl.BlockSpec((bt, GDN_HEADS, GDN_DK, GDN_DV), lambda i: (i, 0, 0, 0))
    return pl.pallas_call(
        _gdn_sample_kernel,
        grid=(b // bt,),
        in_specs=[tok(GDN_QK_DIM), tok(GDN_QK_DIM), tok(GDN_V_DIM), tok(V7X_LANES), tok(cat),
                  pl.BlockSpec((bt, 1, cat), lambda i: (i, 0, 0)), state],
        out_specs=[tok(GDN_V_DIM), state],
        out_shape=[jax.ShapeDtypeStruct((b, t, GDN_V_DIM), F32),
                   jax.ShapeDtypeStruct((b, GDN_HEADS, GDN_DK, GDN_DV), F32)],
        compiler_params=pltpu.CompilerParams(dimension_semantics=("parallel",),
                                             vmem_limit_bytes=_vmem_limit(40 << 20)),
        name="gdn_sample",
    )(q, k, v, gcb, grep, grow_flat, s0)


def _swa_prompt_kernel(q_ref, kc_ref, kp_ref, vc_ref, vp_ref, sgb_ref, sink_ref, o_ref):
    w = WINDOW
    hd = SWA_HEAD_DIM
    nkeys = 2 * w
    ncols = 2 * w
    nwin = q_ref.shape[1] // w
    has_prev = pl.program_id(1) > 0
    kj = lax.broadcasted_iota(jnp.int32, (nkeys, ncols), 0)
    qi = lax.broadcasted_iota(jnp.int32, (nkeys, ncols), 1) % w
    mask_inner = ((kj >= w) & ((kj - w) <= qi)) | ((kj < w) & (kj > qi))
    masks = [mask_inner & ((kj >= w) | has_prev)] + [mask_inner] * (nwin - 1)
    lo_keys = lax.broadcasted_iota(jnp.int32, (nkeys, V7X_LANES), 1) < hd
    lo_q = lax.broadcasted_iota(jnp.int32, (w, V7X_LANES), 1) < hd
    scale = hd ** -0.5 * LOG2_E
    problems = [(t, h, gp) for t in range(nwin) for h in range(SWA_KV_HEADS)
                for gp in range(SWA_GROUP // 2)]

    kk, vbd = {}, {}
    for hp in range(SWA_KV_HEADS // 2):
        sl = slice(hp * V7X_LANES, (hp + 1) * V7X_LANES)
        k_all = jnp.concatenate([kp_ref[0, :, sl], kc_ref[0, :, sl]], axis=0)
        v_all = jnp.concatenate([vp_ref[0, :, sl], vc_ref[0, :, sl]], axis=0)
        kr_all = pltpu.roll(k_all, hd, 1)
        vr_all = pltpu.roll(v_all, hd, 1)
        for t in range(nwin):
            keys = slice(t * w, (t + 2) * w)
            kt, kr, vt, vr = k_all[keys], kr_all[keys], v_all[keys], vr_all[keys]
            for par, h in enumerate((2 * hp, 2 * hp + 1)):
                kk[t, h] = (jnp.where(lo_keys, kt, kr) if par == 0
                            else jnp.where(lo_keys, kr, kt)).astype(BF16)
                vv = jnp.where(lo_keys, vt, vr) if par == 0 else jnp.where(lo_keys, vr, vt)
                vbd[t, h] = jnp.concatenate([jnp.where(lo_keys, vv, 0.0), jnp.where(lo_keys, 0.0, vv)],
                                            axis=0).astype(BF16)
    s, pn = {}, {}

    def scores(t, h, gp):
        lo = (h * SWA_GROUP + 2 * gp) * hd
        qp = q_ref[0, t * w:(t + 1) * w, lo:lo + V7X_LANES]
        rq = jnp.concatenate([jnp.where(lo_q, qp, 0.0), jnp.where(lo_q, 0.0, qp)], axis=0).astype(BF16)
        sc = lax.dot_general(kk[t, h], rq, (((1,), (1,)), ((), ())), preferred_element_type=F32)
        s[t, h, gp] = jnp.where(masks[t], sc * scale, -jnp.inf)

    def softmax(t, h, gp):
        sink = sink_ref[h * (SWA_GROUP // 2) + gp] * LOG2_E
        m = jnp.maximum(jnp.max(s[t, h, gp], axis=0, keepdims=True), sink)
        p = jnp.exp2(s[t, h, gp] - m)
        denom = jnp.sum(p, axis=0, keepdims=True) + jnp.exp2(sink - m)
        p = (p * (1.0 / denom)).astype(BF16)
        pn[t, h, gp] = jnp.concatenate([p[:, :w], p[:, w:]], axis=0)

    def values(t, h, gp):
        lo = (h * SWA_GROUP + 2 * gp) * hd
        rows = slice(t * w, (t + 1) * w)
        o = lax.dot_general(pn[t, h, gp], vbd[t, h], (((0,), (0,)), ((), ())), preferred_element_type=F32)
        o_ref[0, rows, lo:lo + V7X_LANES] = o * sgb_ref[0, rows, lo:lo + V7X_LANES]

    for stage in (scores, softmax, values):
        for prob in problems:
            stage(*prob)


def _swa_prompt(q, k, v, sgb, sink_cols):
    b, l, _ = q.shape
    nwin = _token_tile(l // WINDOW, SWA_WINDOWS_PER_STEP)
    nb = l // (nwin * WINDOW)
    cur = lambda wd: pl.BlockSpec((1, nwin * WINDOW, wd), lambda i, j: (i, j, 0))
    prev = lambda wd: pl.BlockSpec((1, WINDOW, wd), lambda i, j: (i, jnp.maximum(j * nwin - 1, 0), 0))
    return pl.pallas_call(
        _swa_prompt_kernel,
        grid=(b, nb),
        in_specs=[cur(SWA_Q_DIM), cur(SWA_KV_DIM), prev(SWA_KV_DIM), cur(SWA_KV_DIM), prev(SWA_KV_DIM),
                  cur(SWA_Q_DIM), pl.BlockSpec(sink_cols.shape, lambda i, j: (0, 0, 0))],
        out_specs=cur(SWA_Q_DIM),
        out_shape=jax.ShapeDtypeStruct((b, l, SWA_Q_DIM), F32),
        compiler_params=pltpu.CompilerParams(dimension_semantics=("parallel", "parallel")),
        name="swa_prompt",
    )(q, k, k, v, v, sgb, sink_cols)


def _swa_sample_kernel(q_ref, kn_ref, vn_ref, knt_ref, vnt_ref, kb_ref, vb_ref, sgb_ref, sink_ref,
                       o_ref, ko_ref, vo_ref, *, steps):
    t = steps
    bt = q_ref.shape[0]
    nrows = kb_ref.shape[2]
    rows = SWA_GROUP * t
    qt_b = lax.broadcasted_iota(jnp.int32, (rows, nrows), 0) % t
    ki_b = lax.broadcasted_iota(jnp.int32, (rows, nrows), 1)
    mask_buf = ((nrows - ki_b + qt_b) < WINDOW)[None]
    qt_n = lax.broadcasted_iota(jnp.int32, (rows, t), 0) % t
    ki_n = lax.broadcasted_iota(jnp.int32, (rows, t), 1)
    mask_new = (ki_n <= qt_n)[None]
    scale = SWA_HEAD_DIM ** -0.5
    bdot = lambda a, b_, dims: lax.dot_general(a.astype(BF16), b_.astype(BF16), dims,
                                               preferred_element_type=F32)
    nt_dims = (((2,), (2,)), ((0,), (0,)))
    nn_dims = (((2,), (1,)), ((0,), (0,)))
    for h in range(SWA_KV_HEADS):
        klo = h * SWA_HEAD_DIM
        q_h = jnp.concatenate(
            [q_ref[:, :, (h * SWA_GROUP + g) * SWA_HEAD_DIM:(h * SWA_GROUP + g + 1) * SWA_HEAD_DIM]
             for g in range(SWA_GROUP)], axis=1)
        sink = sink_ref[h][None]
        k_b = kb_ref[:, klo:klo + SWA_HEAD_DIM, :]
        v_b = vb_ref[:, klo:klo + SWA_HEAD_DIM, :]
        k_n = kn_ref[:, :, klo:klo + SWA_HEAD_DIM]
        v_n = vn_ref[:, :, klo:klo + SWA_HEAD_DIM]
        s_b = jnp.where(mask_buf, bdot(q_h, k_b, nn_dims) * scale, -jnp.inf)
        s_n = jnp.where(mask_new, bdot(q_h, k_n, nt_dims) * scale, -jnp.inf)
        m = jnp.maximum(jnp.maximum(jnp.max(s_b, axis=-1, keepdims=True),
                                    jnp.max(s_n, axis=-1, keepdims=True)), sink)
        p_b = jnp.exp(s_b - m)
        p_n = jnp.exp(s_n - m)
        denom = (jnp.sum(p_b, axis=-1, keepdims=True) + jnp.sum(p_n, axis=-1, keepdims=True)
                 + jnp.exp(sink - m))
        o = (bdot(p_b, v_b, nt_dims) + bdot(p_n, v_n, nn_dims)) / denom
        for g in range(SWA_GROUP):
            lo = (h * SWA_GROUP + g) * SWA_HEAD_DIM
            o_ref[:, :, lo:lo + SWA_HEAD_DIM] = (o[:, g * t:(g + 1) * t]
                                                 * sgb_ref[:, :, lo:lo + SWA_HEAD_DIM])
    is_new = lax.broadcasted_iota(jnp.int32, (SWA_KV_DIM, nrows), 1) >= nrows - t
    for new_ref, buf_ref, out_ref in ((knt_ref, kb_ref, ko_ref), (vnt_ref, vb_ref, vo_ref)):
        fresh = new_ref[...]
        for i in range(bt):
            kept = pltpu.roll(buf_ref[i], nrows - t, 1)
            out_ref[i] = jnp.where(is_new, pltpu.roll(fresh, (nrows - t - i * t) % nrows, 1), kept)


def _swa_sample(q, k, v, k_t, v_t, k_buf, v_buf, sgb, sink_rows, bt):
    b, t, _ = q.shape
    nrows = k_buf.shape[2]
    assert bt * t == nrows == V7X_LANES, "one lane tile of new columns per step"
    new = lambda wd: pl.BlockSpec((bt, t, wd), lambda i: (i, 0, 0))
    new_t = pl.BlockSpec((SWA_KV_DIM, bt * t), lambda i: (0, i))
    buf = pl.BlockSpec((bt, SWA_KV_DIM, nrows), lambda i: (i, 0, 0))
    kern = functools.partial(_swa_sample_kernel, steps=t)
    return pl.pallas_call(
        kern,
        grid=(b // bt,),
        in_specs=[new(SWA_Q_DIM), new(SWA_KV_DIM), new(SWA_KV_DIM), new_t, new_t, buf, buf, new(SWA_Q_DIM),
                  pl.BlockSpec(sink_rows.shape, lambda i: (0, 0, 0))],
        out_specs=[new(SWA_Q_DIM), buf, buf],
        out_shape=[jax.ShapeDtypeStruct((b, t, SWA_Q_DIM), F32),
                   jax.ShapeDtypeStruct((b, SWA_KV_DIM, nrows), F32),
                   jax.ShapeDtypeStruct((b, SWA_KV_DIM, nrows), F32)],
        compiler_params=pltpu.CompilerParams(dimension_semantics=("parallel",)),
        name="swa_sample",
    )(q, k, v, k_t, v_t, k_buf, v_buf, sgb, sink_rows)


def _out_kernel(x_ref, og_ref, zga_ref, osg_ref, gn_ref, wo_ref, nw2_ref, win_ref, wout_ref, nf_ref, y_ref):
    gn = gn_ref[...]
    heads = []
    for h in range(GDN_HEADS):
        lo = h * GDN_DV
        heads.append(_rms(og_ref[:, lo:lo + GDN_DV], gn))
    mixed = jnp.concatenate(heads, axis=1) * zga_ref[...] + osg_ref[...]
    x2 = x_ref[...] + jnp.dot(mixed.astype(BF16), wo_ref[...], preferred_element_type=F32)
    x3 = _swiglu_half_step(x2, nw2_ref[...], win_ref, wout_ref)
    y_ref[...] = _rms(x3, nf_ref[...])


def _out(x1, og, zga, osg, gnorm, wo, nw2, win, wout, nf, tm):
    t = x1.shape[0]
    row = pl.BlockSpec((tm, D_MODEL), lambda i: (i, 0))
    wbytes = (wo.size + win.size + wout.size) * 2
    return pl.pallas_call(
        _out_kernel,
        grid=(t // tm,),
        in_specs=[row] * 4 + [_const_spec(gnorm.shape), _const_spec(wo.shape), _const_spec(nw2.shape),
                              _const_spec(win.shape), _const_spec(wout.shape), _const_spec(nf.shape)],
        out_specs=row,
        out_shape=jax.ShapeDtypeStruct((t, D_MODEL), F32),
        compiler_params=pltpu.CompilerParams(
            dimension_semantics=("parallel",),
            vmem_limit_bytes=_vmem_limit(wbytes + 18 * tm * D_MODEL * 4 + (8 << 20))),
        name="out",
    )(x1, og, zga, osg, gnorm, wo, nw2, win, wout, nf)


def _rope_tables(pos):
    inv = ROPE_THETA ** (-jnp.arange(0, ROPE_DIM, 2, dtype=F32) / ROPE_DIM)
    ang = pos.astype(F32)[:, None] * inv[None, :]
    cos, sin = jnp.cos(ang), jnp.sin(ang)
    n = pos.shape[0]
    pad = SWA_HEAD_DIM - ROPE_DIM
    one_head = lambda first, second, fill: jnp.concatenate(
        [first, second, jnp.full((n, pad), fill, F32)], axis=1)
    zeros = jnp.zeros_like(sin)
    cos_f = one_head(cos, cos, 1.0)
    sin_a = one_head(-sin, zeros, 0.0)
    sin_b = one_head(zeros, sin, 0.0)
    reps = V7X_LANES // SWA_HEAD_DIM
    return tuple(jnp.tile(t, (1, reps)) for t in (cos_f, sin_a, sin_b))


def _pad_lanes(v, width):
    return jnp.pad(v.astype(F32), (0, width - v.shape[0]))[None, :]


def _branch(x, pos, conv_buf, s0, k_buf, v_buf, p):
    b, l, _ = x.shape
    t = b * l
    is_prompt = conv_buf is None
    chunk = min(GDN_CHUNK, l)
    n = l // chunk
    tm = _token_tile(l if is_prompt else t, TOKEN_TILE)
    tables = _rope_tables(pos)
    if is_prompt:
        halo = None
        table_tiles = l // tm
    else:
        halo = jnp.pad(conv_buf, ((0, 0), (l - (CONV_W - 1), 0), (0, 0))).reshape(t, CONV_DIM)
        heads_t = lambda tb: jnp.tile(tb.T, (SWA_KV_DIM // V7X_LANES, tm // l))
        tables_t = tuple(heads_t(tb) for tb in tables)
        tables = tuple(jnp.tile(tb, (tm // l, 1)) for tb in tables)
        table_tiles = 1
    x1, qg, kg, vg, tail = _ffn_conv(x.reshape(t, D_MODEL), p['norm_ffn1'], p['w_ffn1_in'], p['w_ffn1_out'],
                                     p['norm_mix'], p['w_qkv'], p['conv_w'], halo, tm=tm, seq_len=l, group=l)
    mixed = _mix(x1, p['norm_mix'], p['w_mix'], p['a_log_pad'], p['dt_bias_pad'], tables,
                 None if is_prompt else tables_t, tm=tm, table_tiles=table_tiles, chunk=chunk)
    gcb, zga, qr, kr, vs, sgb = mixed[:6]
    seq = lambda a: a.reshape(b, l, a.shape[-1])
    qg, kg, vg, gcb, qr, kr, vs, sgb = (seq(a) for a in (qg, kg, vg, gcb, qr, kr, vs, sgb))
    new_conv = tail.reshape(b, -1, CONV_DIM)[:, -(CONV_W - 1):]

    gc = gcb[:, :, :GDN_HEADS]
    grow = gc.reshape(b, n, chunk, GDN_HEADS).transpose(0, 1, 3, 2)
    if is_prompt:
        assert chunk == PK_C and s0 is None
        u, w, qgd, kgd, qk, gl = _pk_pre(qg, kg, vg, gcb, grow.reshape(b, n, 1, PK_CAT),
                                         _token_tile(n, PK_CHUNKS_PER_STEP))
        og, new_s = _pk_scan(u, w, qgd, kgd, qk, gl)
        sink_cols = jnp.repeat(p['sinks'].reshape(SWA_Q_HEADS // 2, 2), WINDOW, axis=1)
        osg = _swa_prompt(qr, kr, vs, sgb, sink_cols.reshape(SWA_Q_HEADS // 2, 1, 2 * WINDOW))
        new_k, new_v = kr[:, l - WINDOW:], vs[:, l - WINDOW:]
    else:
        assert n == 1
        og, new_s = _gdn_sample(qg, kg, vg, gcb, jnp.repeat(gc, l, axis=-1),
                                grow.reshape(b, 1, GDN_HEADS * l), s0, _token_tile(b, 8))
        sink_rows = jnp.repeat(p['sinks'], l, axis=1).reshape(SWA_KV_HEADS, SWA_GROUP * l, 1)
        nrows = k_buf.shape[1]
        to_minor = lambda c: c.transpose(0, 2, 3, 1).reshape(b, SWA_KV_DIM, nrows)
        from_minor = lambda c: c.reshape(b, SWA_KV_HEADS, SWA_HEAD_DIM, nrows).transpose(0, 3, 1, 2)
        osg, new_k, new_v = _swa_sample(qr, kr, vs, mixed[6], mixed[7], to_minor(k_buf), to_minor(v_buf),
                                        sgb, sink_rows, nrows // l)
        new_k, new_v = from_minor(new_k), from_minor(new_v)
    y = _out(x1, og.reshape(t, GDN_V_DIM), zga, osg.reshape(t, SWA_Q_DIM), p['gdn_norm'], p['w_out'],
             p['norm_ffn2'], p['w_ffn2_in'], p['w_ffn2_out'], p['norm_final'], tm)
    kv_shape = (b, new_k.shape[1], SWA_KV_HEADS, SWA_HEAD_DIM)
    return (y.reshape(b, l, D_MODEL), new_conv, new_s, new_k.reshape(kv_shape), new_v.reshape(kv_shape))


def _layer_params(l, norm_ffn1, w_ffn1_in, w_ffn1_out, norm_mix, w_in, conv_w, gdn_a_log, gdn_dt_bias,
                  gdn_norm, swa_sinks, w_out, norm_ffn2, w_ffn2_in, w_ffn2_out, norm_final):
    bounds = [0]
    for s in IN_SPLITS:
        bounds.append(bounds[-1] + s)
    w_in_t = jnp.swapaxes(w_in[l], 0, 1)
    col = lambda i: w_in_t[bounds[i]:bounds[i + 1]].astype(BF16)
    w_ab = jnp.pad(jnp.concatenate([col(2), col(3)], axis=0), ((0, V7X_LANES - 2 * GDN_HEADS), (0, 0)))
    row = lambda v: v.astype(F32)[None, :]
    return {
        'norm_ffn1': row(norm_ffn1[l]), 'w_ffn1_in': w_ffn1_in[l].astype(BF16),
        'w_ffn1_out': w_ffn1_out[l].astype(BF16),
        'norm_mix': row(norm_mix[l]),
        'w_qkv': col(0),
        'w_mix': [w_ab, col(1), col(7), col(4), col(5), col(6), col(8)],
        'conv_w': conv_w[l].astype(F32),
        'a_log_pad': _pad_lanes(gdn_a_log[l], V7X_LANES), 'dt_bias_pad': _pad_lanes(gdn_dt_bias[l], V7X_LANES),
        'gdn_norm': row(gdn_norm[l]),
        'sinks': swa_sinks[l].astype(F32).reshape(SWA_KV_HEADS, SWA_GROUP),
        'w_out': w_out[l].astype(BF16),
        'norm_ffn2': row(norm_ffn2[l]), 'w_ffn2_in': w_ffn2_in[l].astype(BF16),
        'w_ffn2_out': w_ffn2_out[l].astype(BF16),
        'norm_final': row(norm_final),
    }


def kernel(x_prompt, x_sample, state_conv, state_gdn, cache_swa_k, cache_swa_v, norm_ffn1, w_ffn1_in, w_ffn1_out, norm_mix, w_in, conv_w, gdn_a_log, gdn_dt_bias, gdn_norm, swa_sinks, w_out, norm_ffn2, w_ffn2_in, w_ffn2_out, norm_final):
    depth = w_in.shape[0]
    assert depth == 1, "the final norm is fused into the layer's last kernel"
    l_p = x_prompt.shape[1]
    l_s = x_sample.shape[1]
    pos_p = jnp.arange(l_p, dtype=jnp.int32)
    pos_s = PAST_LEN + jnp.arange(l_s, dtype=jnp.int32)
    p = _layer_params(0, norm_ffn1, w_ffn1_in, w_ffn1_out, norm_mix, w_in, conv_w, gdn_a_log, gdn_dt_bias,
                      gdn_norm, swa_sinks, w_out, norm_ffn2, w_ffn2_in, w_ffn2_out, norm_final)
    yp, c1, s1, k1, v1 = _branch(x_prompt, pos_p, None, None, None, None, p)
    ys, c2, s2, k2, v2 = _branch(x_sample, pos_s, state_conv[0], state_gdn[0], cache_swa_k[0],
                                 cache_swa_v[0], p)
    return (yp, ys, c1[None], c2[None], s1[None], s2[None], k1[None], k2[None], v1[None], v2[None])
```

```python
import functools

import jax
import jax.numpy as jnp
from jax import lax
from jax.experimental import pallas as pl
from jax.experimental.pallas import tpu as pltpu

D_MODEL = 1024
PAST_LEN = 16384
GDN_HEADS = 8
GDN_DK = 128
GDN_DV = 128
GDN_QK_DIM = GDN_HEADS * GDN_DK
GDN_V_DIM = GDN_HEADS * GDN_DV
CONV_W = 4
GDN_CHUNK = 64
CONV_DIM = 2 * GDN_QK_DIM + GDN_V_DIM
SWA_Q_HEADS = 16
SWA_KV_HEADS = 4
SWA_GROUP = SWA_Q_HEADS // SWA_KV_HEADS
SWA_HEAD_DIM = 64
SWA_Q_DIM = SWA_Q_HEADS * SWA_HEAD_DIM
SWA_KV_DIM = SWA_KV_HEADS * SWA_HEAD_DIM
WINDOW = 128
ROPE_DIM = SWA_HEAD_DIM // 4
ROPE_HALF = ROPE_DIM // 2
ROPE_THETA = 500000.0
D_FF = 2816
EPS = 1e-6
LOG2_E = 1.4426950408889634
IN_SPLITS = (CONV_DIM, GDN_V_DIM, GDN_HEADS, GDN_HEADS, SWA_Q_DIM, SWA_KV_DIM, SWA_KV_DIM, D_MODEL, D_MODEL)

V7X_LANES = 128
V7X_SUBLANES = 8
V7X_MXU_DIM = 256
V7X_VMEM_BYTES = 64 * 1024 * 1024

TOKEN_TILE = 2 * V7X_MXU_DIM
FFN_CHUNK = V7X_MXU_DIM
CONV_PIECE = V7X_MXU_DIM
SOLVE_BASE = V7X_SUBLANES
SWA_WINDOWS_PER_STEP = 4

F32 = jnp.float32
BF16 = jnp.bfloat16


def _dot(a, b):
    return jnp.dot(a.astype(BF16), b.astype(BF16), preferred_element_type=F32)


def _dot_nt(a, b):
    return lax.dot_general(a.astype(BF16), b.astype(BF16), (((1,), (1,)), ((), ())),
                           preferred_element_type=F32)


def _dot_tn(a, b):
    return lax.dot_general(a.astype(BF16), b.astype(BF16), (((0,), (0,)), ((), ())),
                           preferred_element_type=F32)


def _proj(xn, w_t):
    return lax.dot_general(xn, w_t, (((1,), (1,)), ((), ())), preferred_element_type=F32)


def _rms(x, w):
    return x * lax.rsqrt(jnp.mean(x * x, axis=-1, keepdims=True) + EPS) * w


def _sigmoid(x):
    return 0.5 * jnp.tanh(0.5 * x) + 0.5


def _silu(x):
    h = 0.5 * x
    return h * jnp.tanh(h) + h


def _vmem_limit(nbytes):
    return int(min(V7X_VMEM_BYTES - (4 << 20), max(nbytes, 16 << 20)))


def _const_spec(shape):
    nd = len(shape)
    return pl.BlockSpec(shape, lambda *_: (0,) * nd, pipeline_mode=pl.Buffered(1))


def _token_tile(t, want):
    tm = min(want, t)
    while t % tm:
        tm //= 2
    return tm


FFN_NCHUNK = D_FF // FFN_CHUNK


def _swiglu_act(xn, win_ref, c):
    lo = c * FFN_CHUNK
    g = jnp.dot(xn, win_ref[:, lo:lo + FFN_CHUNK], preferred_element_type=F32)
    u = jnp.dot(xn, win_ref[:, D_FF + lo:D_FF + lo + FFN_CHUNK], preferred_element_type=F32)
    return (_silu(g) * u).astype(BF16)


def _swiglu_half_step(x, norm_w, win_ref, wout_ref):
    xn = _rms(x, norm_w).astype(BF16)
    act = jnp.concatenate([_swiglu_act(xn, win_ref, c) for c in range(FFN_NCHUNK)], axis=1)
    return x + 0.5 * jnp.dot(act, wout_ref[...], preferred_element_type=F32)


def _ffn_conv_kernel(*refs, tiles_per_seq, group):
    carried = tiles_per_seq > 0
    if carried:
        (x_ref, nw1_ref, win_ref, wout_ref, nw_ref, wqkv_ref, cw_ref,
         x1_ref, qg_ref, kg_ref, vg_ref, tail_ref, xn_ref, carry_ref) = refs
    else:
        (x_ref, nw1_ref, win_ref, wout_ref, nw_ref, wqkv_ref, cw_ref, halo_ref,
         x1_ref, qg_ref, kg_ref, vg_ref, tail_ref, xn_ref) = refs
    rows = x_ref.shape[0]
    step = pl.program_id(0)
    n_tiles = pl.num_programs(0) - 1
    if carried:
        seq_start = ((step - 1) % tiles_per_seq) == 0
    else:
        pos = lax.broadcasted_iota(jnp.int32, (rows, 1), 0) % group

    def conv_piece(c):
        cols = slice(c * CONV_PIECE, (c + 1) * CONV_PIECE)
        raw = _proj(xn_ref[...], wqkv_ref[cols, :])
        cw = cw_ref[:, cols]
        acc = raw * cw[CONV_W - 1:CONV_W, :]
        if carried:
            halo = jnp.where(seq_start, 0.0, carry_ref[:, cols])
            ext = jnp.concatenate([halo, raw], axis=0)
            for s in range(1, CONV_W):
                acc = acc + pltpu.roll(ext, s, 0)[V7X_SUBLANES:] * cw[CONV_W - 1 - s:CONV_W - s, :]
            carry_ref[:, cols] = raw[rows - V7X_SUBLANES:]
            tail_ref[0, :, cols] = raw[rows - V7X_SUBLANES:]
        else:
            halo = halo_ref[:, cols]
            for s in range(1, CONV_W):
                shifted = jnp.where(pos >= s, pltpu.roll(raw, s, 0),
                                    pltpu.roll(halo, (rows + s - group) % rows, 0))
                acc = acc + shifted * cw[CONV_W - 1 - s:CONV_W - s, :]
            tail_ref[:, cols] = raw
        conv = _silu(acc)
        part, lo = divmod(c * CONV_PIECE, GDN_QK_DIM)
        out_ref = (qg_ref, kg_ref, vg_ref)[part]
        if part == 2:
            out_ref[:, lo:lo + CONV_PIECE] = conv.astype(BF16)
        else:
            for h in range(CONV_PIECE // GDN_DK):
                ch = conv[:, h * GDN_DK:(h + 1) * GDN_DK]
                out_ref[:, lo + h * GDN_DK:lo + (h + 1) * GDN_DK] = (
                    ch * lax.rsqrt(jnp.sum(ch * ch, axis=-1, keepdims=True) + EPS)).astype(BF16)

    def body(do_ffn, do_conv):
        n_pieces = CONV_DIM // CONV_PIECE
        n_out = D_MODEL // FFN_CHUNK
        slots = FFN_NCHUNK + n_out
        due = [((k + 1) * n_pieces) // slots for k in range(slots)] if do_ffn else [n_pieces]
        done = 0
        if do_ffn:
            x = x_ref[...]
            xn = _rms(x, nw1_ref[...]).astype(BF16)
            acts, outs = [], []
        for k, upto in enumerate(due):
            if do_ffn and k < FFN_NCHUNK:
                acts.append(_swiglu_act(xn, win_ref, k))
            elif do_ffn:
                if k == FFN_NCHUNK:
                    act = jnp.concatenate(acts, axis=1)
                lo = (k - FFN_NCHUNK) * FFN_CHUNK
                outs.append(jnp.dot(act, wout_ref[:, lo:lo + FFN_CHUNK], preferred_element_type=F32))
            if do_conv:
                for c in range(done, upto):
                    conv_piece(c)
                done = upto
        if do_ffn:
            x1 = x + 0.5 * jnp.concatenate(outs, axis=1)
            x1_ref[...] = x1
            xn_ref[...] = _rms(x1, nw_ref[...]).astype(BF16)

    @pl.when(step == 0)
    def _():
        if carried:
            carry_ref[...] = jnp.zeros_like(carry_ref)
        body(True, False)

    @pl.when((step > 0) & (step < n_tiles))
    def _():
        body(True, True)

    @pl.when(step == n_tiles)
    def _():
        body(False, True)


def _ffn_conv(x, norm1, win, wout, norm_w, wqkv, cw, halo, *, tm, seq_len, group):
    t = x.shape[0]
    carried = halo is None
    tiles_per_seq = seq_len // tm if carried else 0
    n_tiles = t // tm
    cur = lambda n: pl.BlockSpec((tm, n), lambda i: (jnp.minimum(i, n_tiles - 1), 0))
    prev = lambda n: pl.BlockSpec((tm, n), lambda i: (jnp.maximum(i - 1, 0), 0))
    weights = [win, wout, wqkv]
    in_specs = [cur(D_MODEL), _const_spec((1, D_MODEL)), _const_spec(win.shape), _const_spec(wout.shape),
                _const_spec((1, D_MODEL)), _const_spec(wqkv.shape), _const_spec(cw.shape)]
    args = [x, norm1, win, wout, norm_w, wqkv, cw]
    out_specs = [cur(D_MODEL), prev(GDN_QK_DIM), prev(GDN_QK_DIM), prev(GDN_V_DIM)]
    out_shape = [jax.ShapeDtypeStruct((t, D_MODEL), F32)] + [
        jax.ShapeDtypeStruct((t, n), BF16) for n in (GDN_QK_DIM, GDN_QK_DIM, GDN_V_DIM)]
    scratch = [pltpu.VMEM((tm, D_MODEL), BF16)]
    if carried:
        nseq = t // seq_len
        out_specs.append(pl.BlockSpec((1, V7X_SUBLANES, CONV_DIM),
                                      lambda i: (jnp.maximum(i - 1, 0) // tiles_per_seq, 0, 0)))
        out_shape.append(jax.ShapeDtypeStruct((nseq, V7X_SUBLANES, CONV_DIM), F32))
        scratch.append(pltpu.VMEM((V7X_SUBLANES, CONV_DIM), F32))
    else:
        in_specs.append(prev(CONV_DIM))
        args.append(halo)
        out_specs.append(prev(CONV_DIM))
        out_shape.append(jax.ShapeDtypeStruct((t, CONV_DIM), F32))
    wbytes = sum(w.size for w in weights) * 2
    kern = functools.partial(_ffn_conv_kernel, tiles_per_seq=tiles_per_seq, group=group)
    return pl.pallas_call(
        kern,
        grid=(n_tiles + 1,),
        in_specs=in_specs,
        out_specs=out_specs,
        out_shape=out_shape,
        scratch_shapes=scratch,
        compiler_params=pltpu.CompilerParams(
            dimension_semantics=("arbitrary",),
            vmem_limit_bytes=_vmem_limit(wbytes + 32 * tm * D_MODEL * 4 + (4 << 20))),
        name="ffn_conv",
    )(*args)


def _rope(x, cos_f, sin_a, sin_b):
    width = x.shape[1]
    reps = width // V7X_LANES
    tile = lambda t: jnp.concatenate([t] * reps, axis=1) if reps > 1 else t
    return (x * tile(cos_f) + pltpu.roll(x, width - ROPE_HALF, 1) * tile(sin_a)
            + pltpu.roll(x, ROPE_HALF, 1) * tile(sin_b))


def _mix_kernel(*refs, chunk, kv_t):
    (x_ref, nw_ref, wab_ref, wz_ref, wga_ref, wqs_ref, wks_ref, wvs_ref, wgb_ref,
     alog_ref, dtb_ref, cos_ref, sina_ref, sinb_ref) = refs[:14]
    if kv_t:
        cost_ref, sinat_ref, sinbt_ref = refs[14:17]
        gcb_ref, zga_ref, qr_ref, kr_ref, vs_ref, sgb_ref, krt_ref, vst_ref = refs[17:]
    else:
        gcb_ref, zga_ref, qr_ref, kr_ref, vs_ref, sgb_ref = refs[14:]
    rows = x_ref.shape[0]
    xn = _rms(x_ref[...], nw_ref[...]).astype(BF16)

    ab = _proj(xn, wab_ref[...])
    sp_in = ab + dtb_ref[...]
    softplus = jnp.maximum(sp_in, 0.0) + jnp.log1p(jnp.exp(-jnp.abs(sp_in)))
    gc = -jnp.exp(alog_ref[...]) * softplus
    cpos = lax.broadcasted_iota(jnp.int32, (rows, 1), 0) % chunk
    s = 1
    while s < chunk:
        gc = gc + jnp.where(cpos >= s, pltpu.roll(gc, s, 0), 0.0)
        s *= 2
    lane = lax.broadcasted_iota(jnp.int32, (1, V7X_LANES), 1)
    gcb_ref[...] = jnp.where(lane < GDN_HEADS, gc, jax.nn.sigmoid(ab))

    zga_ref[...] = _silu(_proj(xn, wz_ref[...])) * _sigmoid(_proj(xn, wga_ref[...]))
    cos_f, sin_a, sin_b = cos_ref[...], sina_ref[...], sinb_ref[...]
    qr_ref[...] = _rope(_proj(xn, wqs_ref[...]), cos_f, sin_a, sin_b)
    kr_ref[...] = _rope(_proj(xn, wks_ref[...]), cos_f, sin_a, sin_b)
    vs_ref[...] = _proj(xn, wvs_ref[...])
    sgb_ref[...] = _sigmoid(_proj(xn, wgb_ref[...]))
    if kv_t:
        ks_t = lax.dot_general(wks_ref[...], xn, (((1,), (1,)), ((), ())), preferred_element_type=F32)
        krt_ref[...] = (ks_t * cost_ref[...] + pltpu.roll(ks_t, SWA_KV_DIM - ROPE_HALF, 0) * sinat_ref[...]
                        + pltpu.roll(ks_t, ROPE_HALF, 0) * sinbt_ref[...])
        vst_ref[...] = lax.dot_general(wvs_ref[...], xn, (((1,), (1,)), ((), ())),
                                       preferred_element_type=F32)


def _mix(x, norm_w, weights, alog, dtb, tables, tables_t, *, tm, table_tiles, chunk):
    t = x.shape[0]
    kv_t = tables_t is not None
    row = lambda n: pl.BlockSpec((tm, n), lambda i: (i, 0))
    tab = pl.BlockSpec((tm, V7X_LANES), lambda i: (i % table_tiles, 0))
    widths = (V7X_LANES, GDN_V_DIM, SWA_Q_DIM, SWA_KV_DIM, SWA_KV_DIM, D_MODEL)
    in_specs = ([row(D_MODEL), _const_spec((1, D_MODEL))] + [_const_spec(w.shape) for w in weights]
                + [_const_spec(alog.shape), _const_spec(dtb.shape), tab, tab, tab])
    args = [x, norm_w, *weights, alog, dtb, *tables]
    out_specs = [row(n) for n in widths]
    out_shape = [jax.ShapeDtypeStruct((t, n), F32) for n in widths]
    if kv_t:
        in_specs += [_const_spec((SWA_KV_DIM, tm))] * 3
        args += list(tables_t)
        out_specs += [pl.BlockSpec((SWA_KV_DIM, tm), lambda i: (0, i))] * 2
        out_shape += [jax.ShapeDtypeStruct((SWA_KV_DIM, t), F32)] * 2
    wbytes = sum(w.size for w in weights) * 2
    return pl.pallas_call(
        functools.partial(_mix_kernel, chunk=chunk, kv_t=kv_t),
        grid=(t // tm,),
        in_specs=in_specs,
        out_specs=out_specs,
        out_shape=out_shape,
        compiler_params=pltpu.CompilerParams(
            dimension_semantics=("parallel",),
            vmem_limit_bytes=_vmem_limit(wbytes + 36 * tm * D_MODEL * 4 + (4 << 20))),
        name="mix",
    )(*args)


PK_C = GDN_CHUNK
PK_CAT = GDN_HEADS * PK_C
PK_PAIR = 2 * GDN_DK
PK_NPAIR = GDN_HEADS // 2
PK_GROUP = V7X_MXU_DIM // PK_C
PK_CHUNKS_PER_STEP = 4


def _pk_masks():
    c = PK_C
    i = lax.broadcasted_iota(jnp.int32, (c, PK_CAT), 0)
    j = lax.broadcasted_iota(jnp.int32, (c, PK_CAT), 1) % c
    lo = (lax.broadcasted_iota(jnp.int32, (c, V7X_LANES), 1) < c).astype(BF16)
    return i, j, (lo, 1 - lo)


def _pk_prod(x, y, bd_mask):
    lo, hi = bd_mask
    zero = jnp.zeros((PK_C, V7X_LANES), BF16)
    outs = []
    for g in range(GDN_HEADS // PK_GROUP):
        sl = slice(g * V7X_MXU_DIM, (g + 1) * V7X_MXU_DIM)
        yb = y[:, sl].astype(BF16)
        t0, t1 = yb[:, :V7X_LANES], yb[:, V7X_LANES:]
        bd = jnp.concatenate([jnp.concatenate([t0 * lo, zero], axis=1),
                              jnp.concatenate([t0 * hi, zero], axis=1),
                              jnp.concatenate([zero, t1 * lo], axis=1),
                              jnp.concatenate([zero, t1 * hi], axis=1)], axis=0)
        outs.append(jnp.dot(x[:, sl].astype(BF16), bd, preferred_element_type=F32))
    return jnp.concatenate(outs, axis=1)


def _pk_solve(a_list, i, j, bd_mask):
    prod_all = lambda xs, ys: [_pk_prod(x, y, bd_mask) for x, y in zip(xs, ys)]
    eye = (i == j).astype(F32)
    b = SOLVE_BASE
    same = (i // b) == (j // b)
    p = [jnp.where(same, a, 0.0) for a in a_list]
    d = [eye - x for x in p]
    span = 2
    while span < b:
        p = prod_all(p, p)
        d = [x + y for x, y in zip(d, prod_all(d, p))]
        span *= 2
    while b < PK_C:
        same2 = (i // (2 * b)) == (j // (2 * b))
        sel = same2 & jnp.logical_not(same)
        off = [jnp.where(sel, a, 0.0) for a in a_list]
        d = [x - y for x, y in zip(d, prod_all(d, prod_all(off, d)))]
        same = same2
        b *= 2
    return d


def _pk_pre_body(q_ref, k_ref, v_ref, gcb_ref, grow_ref, u_ref, w_ref, qg_ref, kg_ref, qk_ref, gl_ref):
    c = PK_C
    i, j, bd_mask = _pk_masks()
    lane_lo = lax.broadcasted_iota(jnp.int32, (c, V7X_LANES), 1) < c
    zeros_k = jnp.zeros((c, GDN_DK), BF16)
    chunks = [(b, ch) for b in range(q_ref.shape[0]) for ch in range(q_ref.shape[1] // c)]
    a_list, vb_list, kbg_list = [], [], []
    for b, ch in chunks:
        rows = slice(ch * c, (ch + 1) * c)
        gcb = gcb_ref[b, rows, :]
        g_end = gcb[c - 1:c, :]
        eg_s = jnp.exp(gcb)
        ek_s = jnp.exp(g_end - gcb)
        col = lambda arr, n: jnp.broadcast_to(arr[:, n:n + 1], (c, V7X_LANES))
        full = lambda arr, base: jnp.concatenate([col(arr, base + h) for h in range(GDN_HEADS)], axis=1)
        gc_cols = [col(gcb, h) for h in range(GDN_HEADS)]
        beta_f = full(gcb, GDN_HEADS)
        eg_f = full(eg_s, 0)
        ek_f = full(ek_s, 0)
        gc_cat = jnp.concatenate([jnp.where(lane_lo, gc_cols[2 * m], gc_cols[2 * m + 1])
                                  for m in range(PK_NPAIR)], axis=1)
        decay = jnp.exp(jnp.where(i >= j, gc_cat - grow_ref[b, ch], -jnp.inf))
        q = q_ref[b, rows, :].astype(F32) * (GDN_DK ** -0.5)
        k = k_ref[b, rows, :].astype(F32)
        kb = k * beta_f
        vb_list.append((v_ref[b, rows, :].astype(F32) * beta_f).astype(BF16))
        kbg_list.append((kb * eg_f).astype(BF16))
        qg_ref[b, rows, :] = (q * eg_f).astype(BF16)
        kg_ref[b, rows, :] = (k * ek_f).astype(BF16)
        gl_ref[b, ch] = eg_f[c - 1:c, :]
        kk = []
        for m in range(PK_NPAIR):
            sl = slice(m * PK_PAIR, (m + 1) * PK_PAIR)
            lhs = jnp.concatenate([kb[:, sl], q[:, sl]], axis=0).astype(BF16)
            k2 = k[:, sl].astype(BF16)
            rhs_nt = jnp.concatenate(
                [jnp.concatenate([k2[:, :GDN_DK], zeros_k], axis=1),
                 jnp.concatenate([zeros_k, k2[:, GDN_DK:]], axis=1)], axis=0)
            kk.append(lax.dot_general(lhs, rhs_nt, (((1,), (1,)), ((), ())), preferred_element_type=F32))
        a_raw = jnp.concatenate([x[:c] for x in kk], axis=1)
        qk_raw = jnp.concatenate([x[c:] for x in kk], axis=1)
        a_list.append(jnp.where(i > j, a_raw * decay, 0.0))
        qk_ref[b, rows, :] = (qk_raw * decay).astype(BF16)
    t_list = _pk_solve(a_list, i, j, bd_mask)
    for n, (b, ch) in enumerate(chunks):
        rows = slice(ch * c, (ch + 1) * c)
        t = t_list[n]
        for m in range(PK_NPAIR):
            tp = t[:, m * V7X_LANES:(m + 1) * V7X_LANES]
            lhs = jnp.concatenate([jnp.where(lane_lo, tp, 0.0), jnp.where(lane_lo, 0.0, tp)],
                                  axis=0).astype(BF16)
            h0, h1 = 2 * m, 2 * m + 1
            hs = lambda h: slice(h * GDN_DK, (h + 1) * GDN_DK)
            rhs = jnp.concatenate(
                [jnp.concatenate([vb_list[n][:, hs(h0)], kbg_list[n][:, hs(h0)]], axis=1),
                 jnp.concatenate([vb_list[n][:, hs(h1)], kbg_list[n][:, hs(h1)]], axis=1)], axis=0)
            uw = jnp.dot(lhs, rhs, preferred_element_type=F32)
            u_ref[b, rows, hs(h0)] = uw[:c, :GDN_DV]
            w_ref[b, rows, hs(h0)] = uw[:c, GDN_DV:].astype(BF16)
            u_ref[b, rows, hs(h1)] = uw[c:, :GDN_DV]
            w_ref[b, rows, hs(h1)] = uw[c:, GDN_DV:].astype(BF16)


def _pk_kernel(q_ref, k_ref, v_ref, gcb_ref, grow_ref, o_ref, sout_ref,
               u_ref, w_ref, qg_ref, kg_ref, qk_ref, gl_ref, s_ref):
    c = PK_C
    nb = u_ref.shape[0]
    step = pl.program_id(0)

    @pl.when(step == 0)
    def _():
        s_ref[...] = jnp.zeros_like(s_ref)

    _pk_pre_body(q_ref, k_ref, v_ref, gcb_ref, grow_ref, u_ref, w_ref, qg_ref, kg_ref, qk_ref, gl_ref)

    zeros_s = jnp.zeros((GDN_DK, GDN_DV), BF16)
    zeros_v = jnp.zeros((c, GDN_DV), BF16)
    chains = [(b, m) for b in range(nb) for m in range(PK_NPAIR)]
    pair = lambda m: slice(m * PK_PAIR, (m + 1) * PK_PAIR)
    state = {bm: s_ref[bm] for bm in chains}

    for ch in range(u_ref.shape[1] // c):
        rows = slice(ch * c, (ch + 1) * c)
        wq, bd_v = {}, {}

        def state_products(b, m):
            sb = state[b, m].astype(BF16)
            bd_s = jnp.concatenate([jnp.concatenate([sb[:, :GDN_DV], zeros_s], axis=1),
                                    jnp.concatenate([zeros_s, sb[:, GDN_DV:]], axis=1)], axis=0)
            lhs = jnp.concatenate([w_ref[b, rows, pair(m)], qg_ref[b, rows, pair(m)]], axis=0)
            wq[b, m] = jnp.dot(lhs, bd_s, preferred_element_type=F32)

        def new_values(b, m):
            v_new = (u_ref[b, rows, pair(m)] - wq[b, m][:c]).astype(BF16)
            bd_v[b, m] = jnp.concatenate([jnp.concatenate([v_new[:, :GDN_DV], zeros_v], axis=1),
                                          jnp.concatenate([zeros_v, v_new[:, GDN_DV:]], axis=1)], axis=0)

        def outputs_and_update(b, m):
            qk_p = qk_ref[b, rows, m * V7X_LANES:(m + 1) * V7X_LANES]
            o_ref[b, rows, pair(m)] = wq[b, m][c:] + jnp.dot(qk_p, bd_v[b, m], preferred_element_type=F32)
            kg_p = kg_ref[b, rows, pair(m)]
            kg_st = jnp.concatenate([kg_p[:, :GDN_DK], kg_p[:, GDN_DK:]], axis=0)
            upd = lax.dot_general(kg_st, bd_v[b, m], (((0,), (0,)), ((), ())),
                                  preferred_element_type=F32)
            state[b, m] = state[b, m] * gl_ref[b, ch, :, pair(m)] + upd

        for stage in (state_products, new_values, outputs_and_update):
            for b, m in chains:
                stage(b, m)

    for bm in chains:
        s_ref[bm] = state[bm]

    @pl.when(step == pl.num_programs(0) - 1)
    def _():
        for b, m in chains:
            s_c = s_ref[b, m]
            sout_ref[b, 2 * m] = s_c[:, :GDN_DV]
            sout_ref[b, 2 * m + 1] = s_c[:, GDN_DV:]


def _pk_gdn(q, k, v, gcb, grow_flat):
    b, l, _ = q.shape
    n = l // PK_C
    per = _token_tile(n, PK_CHUNKS_PER_STEP)
    rows = per * PK_C
    tok = lambda wd: pl.BlockSpec((b, rows, wd), lambda s: (0, s, 0))
    return pl.pallas_call(
        _pk_kernel,
        grid=(n // per,),
        in_specs=[tok(GDN_QK_DIM), tok(GDN_QK_DIM), tok(GDN_V_DIM), tok(V7X_LANES),
                  pl.BlockSpec((b, per, 1, PK_CAT), lambda s: (0, s, 0, 0))],
        out_specs=[tok(GDN_V_DIM),
                   pl.BlockSpec((b, GDN_HEADS, GDN_DK, GDN_DV), lambda s: (0, 0, 0, 0))],
        out_shape=[jax.ShapeDtypeStruct((b, l, GDN_V_DIM), F32),
                   jax.ShapeDtypeStruct((b, GDN_HEADS, GDN_DK, GDN_DV), F32)],
        scratch_shapes=[pltpu.VMEM((b, rows, GDN_V_DIM), F32),
                        pltpu.VMEM((b, rows, GDN_QK_DIM), BF16),
                        pltpu.VMEM((b, rows, GDN_QK_DIM), BF16),
                        pltpu.VMEM((b, rows, GDN_QK_DIM), BF16),
                        pltpu.VMEM((b, rows, PK_CAT), BF16),
                        pltpu.VMEM((b, per, 1, GDN_V_DIM), F32),
                        pltpu.VMEM((b, PK_NPAIR, GDN_DK, 2 * GDN_DV), F32)],
        compiler_params=pltpu.CompilerParams(dimension_semantics=("arbitrary",),
                                             vmem_limit_bytes=_vmem_limit(V7X_VMEM_BYTES)),
        name="gdn_prompt",
    )(q, k, v, gcb, grow_flat)


def _head_block_rows(x, t):
    width = x.shape[1] // GDN_HEADS
    z = jnp.zeros((t, width), x.dtype)
    rows = []
    for h in range(GDN_HEADS):
        rows.append(jnp.concatenate([z] * h + [x[:, h * width:(h + 1) * width]]
                                    + [z] * (GDN_HEADS - 1 - h), axis=1))
    return jnp.concatenate(rows, axis=0)


def _gdn_sample_kernel(q_ref, k_ref, v_ref, gcb_ref, grep_ref, grow_ref, s0_ref, o_ref, sout_ref):
    bt, t, _ = q_ref.shape
    cat = GDN_HEADS * t
    i = lax.broadcasted_iota(jnp.int32, (t, cat), 0)
    j = lax.broadcasted_iota(jnp.int32, (t, cat), 1) % t
    r = lax.broadcasted_iota(jnp.int32, (cat, cat), 0) // t
    l = lax.broadcasted_iota(jnp.int32, (cat, cat), 1) // t
    same_head = r == l
    eye = (i == j).astype(F32)
    seqs = range(bt)
    pair = lambda m: slice(m * PK_PAIR, (m + 1) * PK_PAIR)

    vb, kbg, qg, kg, gl, a, qk = {}, {}, {}, {}, {}, {}, {}
    for b in seqs:
        gcb = gcb_ref[b]
        g_end = gcb[t - 1:t, :]
        eg_s = jnp.exp(gcb)
        ek_s = jnp.exp(g_end - gcb)
        col = lambda arr, n: jnp.broadcast_to(arr[:, n:n + 1], (t, V7X_LANES))
        full = lambda arr, base: jnp.concatenate([col(arr, base + h) for h in range(GDN_HEADS)], axis=1)
        beta_f, eg_f, ek_f = full(gcb, GDN_HEADS), full(eg_s, 0), full(ek_s, 0)
        decay = jnp.exp(jnp.where(i >= j, grep_ref[b] - grow_ref[b], -jnp.inf))
        q = q_ref[b].astype(F32) * (GDN_DK ** -0.5)
        k = k_ref[b].astype(F32)
        kb = k * beta_f
        vb[b] = v_ref[b].astype(F32) * beta_f
        kbg[b] = kb * eg_f
        qg[b] = q * eg_f
        kg[b] = k * ek_f
        gl[b] = eg_f[t - 1:t, :]
        kk = _dot_nt(jnp.concatenate([kb, q], axis=0), _head_block_rows(k, t))
        a[b] = jnp.where(i > j, kk[:t] * decay, 0.0)
        qk[b] = kk[t:] * decay

    def prod_all(xs, ys):
        return {b: _dot(xs[b], jnp.where(same_head, jnp.concatenate([ys[b]] * GDN_HEADS, axis=0), 0.0))
                for b in seqs}

    d = {b: eye - a[b] for b in seqs}
    p = a
    span = 2
    while span < t:
        p = prod_all(p, p)
        dp = prod_all(d, p)
        d = {b: d[b] + dp[b] for b in seqs}
        span *= 2

    u, w = {}, {}
    for b in seqs:
        rhs = jnp.concatenate([_head_block_rows(vb[b], t), _head_block_rows(kbg[b], t)], axis=1)
        uw = _dot(d[b], rhs)
        u[b], w[b] = uw[:, :GDN_V_DIM], uw[:, GDN_V_DIM:]

    zeros_s = jnp.zeros((GDN_DK, GDN_DV), BF16)
    wq = {}
    for b in seqs:
        for m in range(PK_NPAIR):
            s0 = s0_ref[b, 2 * m].astype(BF16)
            s1 = s0_ref[b, 2 * m + 1].astype(BF16)
            bd_s = jnp.concatenate([jnp.concatenate([s0, zeros_s], axis=1),
                                    jnp.concatenate([zeros_s, s1], axis=1)], axis=0)
            lhs = jnp.concatenate([w[b][:, pair(m)], qg[b][:, pair(m)]], axis=0).astype(BF16)
            wq[b, m] = jnp.dot(lhs, bd_s, preferred_element_type=F32)
    v_new = {b: u[b] - jnp.concatenate([wq[b, m][:t] for m in range(PK_NPAIR)], axis=1) for b in seqs}
    zeros_v = jnp.zeros((t, GDN_DV), F32)
    for b in seqs:
        o_ref[b] = (jnp.concatenate([wq[b, m][t:] for m in range(PK_NPAIR)], axis=1)
                    + _dot(qk[b], _head_block_rows(v_new[b], t)))
        for m in range(PK_NPAIR):
            vp = v_new[b][:, pair(m)]
            bd_v = jnp.concatenate([jnp.concatenate([vp[:, :GDN_DV], zeros_v], axis=1),
                                    jnp.concatenate([zeros_v, vp[:, GDN_DV:]], axis=1)], axis=0)
            kp = kg[b][:, pair(m)]
            kg_st = jnp.concatenate([kp[:, :GDN_DK], kp[:, GDN_DK:]], axis=0)
            upd = _dot_tn(kg_st, bd_v)
            glp = gl[b][:, pair(m)]
            sout_ref[b, 2 * m] = s0_ref[b, 2 * m] * glp[:, :GDN_DV] + upd[:, :GDN_DV]
            sout_ref[b, 2 * m + 1] = s0_ref[b, 2 * m + 1] * glp[:, GDN_DV:] + upd[:, GDN_DV:]


def _gdn_sample(q, k, v, gcb, grep, grow_flat, s0, bt):
    b, t, _ = q.shape
    assert t == V7X_SUBLANES, "one sublane tile of steps per sequence"
    cat = GDN_HEADS * t
    tok = lambda wd: pl.BlockSpec((bt, t, wd), lambda i: (i, 0, 0))
    state = pl.BlockSpec((bt, GDN_HEADS, GDN_DK, GDN_DV), lambda i: (i, 0, 0, 0))
    return pl.pallas_call(
        _gdn_sample_kernel,
        grid=(b // bt,),
        in_specs=[tok(GDN_QK_DIM), tok(GDN_QK_DIM), tok(GDN_V_DIM), tok(V7X_LANES), tok(cat),
                  pl.BlockSpec((bt, 1, cat), lambda i: (i, 0, 0)), state],
        out_specs=[tok(GDN_V_DIM), state],
        out_shape=[jax.ShapeDtypeStruct((b, t, GDN_V_DIM), F32),
                   jax.ShapeDtypeStruct((b, GDN_HEADS, GDN_DK, GDN_DV), F32)],
        compiler_params=pltpu.CompilerParams(dimension_semantics=("parallel",),
                                             vmem_limit_bytes=_vmem_limit(40 << 20)),
        name="gdn_sample",
    )(q, k, v, gcb, grep, grow_flat, s0)


def _swa_prompt_kernel(q_ref, kc_ref, kp_ref, vc_ref, vp_ref, sgb_ref, sink_ref, o_ref):
    w = WINDOW
    hd = SWA_HEAD_DIM
    nkeys = 2 * w
    ncols = 2 * w
    nwin = q_ref.shape[1] // w
    has_prev = pl.program_id(1) > 0
    kj = lax.broadcasted_iota(jnp.int32, (nkeys, ncols), 0)
    qi = lax.broadcasted_iota(jnp.int32, (nkeys, ncols), 1) % w
    mask_inner = ((kj >= w) & ((kj - w) <= qi)) | ((kj < w) & (kj > qi))
    masks = [mask_inner & ((kj >= w) | has_prev)] + [mask_inner] * (nwin - 1)
    lo_keys = lax.broadcasted_iota(jnp.int32, (nkeys, V7X_LANES), 1) < hd
    lo_q = lax.broadcasted_iota(jnp.int32, (w, V7X_LANES), 1) < hd
    scale = hd ** -0.5 * LOG2_E
    problems = [(t, h, gp) for t in range(nwin) for h in range(SWA_KV_HEADS)
                for gp in range(SWA_GROUP // 2)]

    kk, vbd = {}, {}
    for hp in range(SWA_KV_HEADS // 2):
        sl = slice(hp * V7X_LANES, (hp + 1) * V7X_LANES)
        k_all = jnp.concatenate([kp_ref[0, :, sl], kc_ref[0, :, sl]], axis=0)
        v_all = jnp.concatenate([vp_ref[0, :, sl], vc_ref[0, :, sl]], axis=0)
        kr_all = pltpu.roll(k_all, hd, 1)
        vr_all = pltpu.roll(v_all, hd, 1)
        for t in range(nwin):
            keys = slice(t * w, (t + 2) * w)
            kt, kr, vt, vr = k_all[keys], kr_all[keys], v_all[keys], vr_all[keys]
            for par, h in enumerate((2 * hp, 2 * hp + 1)):
                kk[t, h] = (jnp.where(lo_keys, kt, kr) if par == 0
                            else jnp.where(lo_keys, kr, kt)).astype(BF16)
                vv = jnp.where(lo_keys, vt, vr) if par == 0 else jnp.where(lo_keys, vr, vt)
                vbd[t, h] = jnp.concatenate([jnp.where(lo_keys, vv, 0.0), jnp.where(lo_keys, 0.0, vv)],
                                            axis=0).astype(BF16)
    s, pn = {}, {}

    def scores(t, h, gp):
        lo = (h * SWA_GROUP + 2 * gp) * hd
        qp = q_ref[0, t * w:(t + 1) * w, lo:lo + V7X_LANES]
        rq = jnp.concatenate([jnp.where(lo_q, qp, 0.0), jnp.where(lo_q, 0.0, qp)], axis=0).astype(BF16)
        sc = lax.dot_general(kk[t, h], rq, (((1,), (1,)), ((), ())), preferred_element_type=F32)
        s[t, h, gp] = jnp.where(masks[t], sc * scale, -jnp.inf)

    def softmax(t, h, gp):
        sink = sink_ref[h * (SWA_GROUP // 2) + gp] * LOG2_E
        m = jnp.maximum(jnp.max(s[t, h, gp], axis=0, keepdims=True), sink)
        p = jnp.exp2(s[t, h, gp] - m)
        denom = jnp.sum(p, axis=0, keepdims=True) + jnp.exp2(sink - m)
        p = (p * (1.0 / denom)).astype(BF16)
        pn[t, h, gp] = jnp.concatenate([p[:, :w], p[:, w:]], axis=0)

    def values(t, h, gp):
        lo = (h * SWA_GROUP + 2 * gp) * hd
        rows = slice(t * w, (t + 1) * w)
        o = lax.dot_general(pn[t, h, gp], vbd[t, h], (((0,), (0,)), ((), ())), preferred_element_type=F32)
        o_ref[0, rows, lo:lo + V7X_LANES] = o * sgb_ref[0, rows, lo:lo + V7X_LANES]

    for stage in (scores, softmax, values):
        for prob in problems:
            stage(*prob)


def _swa_prompt(q, k, v, sgb, sink_cols):
    b, l, _ = q.shape
    nwin = _token_tile(l // WINDOW, SWA_WINDOWS_PER_STEP)
    nb = l // (nwin * WINDOW)
    cur = lambda wd: pl.BlockSpec((1, nwin * WINDOW, wd), lambda i, j: (i, j, 0))
    prev = lambda wd: pl.BlockSpec((1, WINDOW, wd), lambda i, j: (i, jnp.maximum(j * nwin - 1, 0), 0))
    return pl.pallas_call(
        _swa_prompt_kernel,
        grid=(b, nb),
        in_specs=[cur(SWA_Q_DIM), cur(SWA_KV_DIM), prev(SWA_KV_DIM), cur(SWA_KV_DIM), prev(SWA_KV_DIM),
                  cur(SWA_Q_DIM), pl.BlockSpec(sink_cols.shape, lambda i, j: (0, 0, 0))],
        out_specs=cur(SWA_Q_DIM),
        out_shape=jax.ShapeDtypeStruct((b, l, SWA_Q_DIM), F32),
        compiler_params=pltpu.CompilerParams(dimension_semantics=("parallel", "parallel")),
        name="swa_prompt",
    )(q, k, k, v, v, sgb, sink_cols)


def _swa_sample_kernel(q_ref, kn_ref, vn_ref, knt_ref, vnt_ref, kb_ref, vb_ref, sgb_ref, sink_ref,
                       o_ref, ko_ref, vo_ref, *, steps):
    t = steps
    bt = q_ref.shape[0]
    nrows = kb_ref.shape[2]
    rows = SWA_GROUP * t
    qt_b = lax.broadcasted_iota(jnp.int32, (rows, nrows), 0) % t
    ki_b = lax.broadcasted_iota(jnp.int32, (rows, nrows), 1)
    mask_buf = ((nrows - ki_b + qt_b) < WINDOW)[None]
    qt_n = lax.broadcasted_iota(jnp.int32, (rows, t), 0) % t
    ki_n = lax.broadcasted_iota(jnp.int32, (rows, t), 1)
    mask_new = (ki_n <= qt_n)[None]
    scale = SWA_HEAD_DIM ** -0.5
    bdot = lambda a, b_, dims: lax.dot_general(a.astype(BF16), b_.astype(BF16), dims,
                                               preferred_element_type=F32)
    nt_dims = (((2,), (2,)), ((0,), (0,)))
    nn_dims = (((2,), (1,)), ((0,), (0,)))
    for h in range(SWA_KV_HEADS):
        klo = h * SWA_HEAD_DIM
        q_h = jnp.concatenate(
            [q_ref[:, :, (h * SWA_GROUP + g) * SWA_HEAD_DIM:(h * SWA_GROUP + g + 1) * SWA_HEAD_DIM]
             for g in range(SWA_GROUP)], axis=1)
        sink = sink_ref[h][None]
        k_b = kb_ref[:, klo:klo + SWA_HEAD_DIM, :]
        v_b = vb_ref[:, klo:klo + SWA_HEAD_DIM, :]
        k_n = kn_ref[:, :, klo:klo + SWA_HEAD_DIM]
        v_n = vn_ref[:, :, klo:klo + SWA_HEAD_DIM]
        s_b = jnp.where(mask_buf, bdot(q_h, k_b, nn_dims) * scale, -jnp.inf)
        s_n = jnp.where(mask_new, bdot(q_h, k_n, nt_dims) * scale, -jnp.inf)
        m = jnp.maximum(jnp.maximum(jnp.max(s_b, axis=-1, keepdims=True),
                                    jnp.max(s_n, axis=-1, keepdims=True)), sink)
        p_b = jnp.exp(s_b - m)
        p_n = jnp.exp(s_n - m)
        denom = (jnp.sum(p_b, axis=-1, keepdims=True) + jnp.sum(p_n, axis=-1, keepdims=True)
                 + jnp.exp(sink - m))
        o = (bdot(p_b, v_b, nt_dims) + bdot(p_n, v_n, nn_dims)) / denom
        for g in range(SWA_GROUP):
            lo = (h * SWA_GROUP + g) * SWA_HEAD_DIM
            o_ref[:, :, lo:lo + SWA_HEAD_DIM] = (o[:, g * t:(g + 1) * t]
                                                 * sgb_ref[:, :, lo:lo + SWA_HEAD_DIM])
    is_new = lax.broadcasted_iota(jnp.int32, (SWA_KV_DIM, nrows), 1) >= nrows - t
    for new_ref, buf_ref, out_ref in ((knt_ref, kb_ref, ko_ref), (vnt_ref, vb_ref, vo_ref)):
        fresh = new_ref[...]
        for i in range(bt):
            kept = pltpu.roll(buf_ref[i], nrows - t, 1)
            out_ref[i] = jnp.where(is_new, pltpu.roll(fresh, (nrows - t - i * t) % nrows, 1), kept)


def _swa_sample(q, k, v, k_t, v_t, k_buf, v_buf, sgb, sink_rows, bt):
    b, t, _ = q.shape
    nrows = k_buf.shape[2]
    assert bt * t == nrows == V7X_LANES, "one lane tile of new columns per step"
    new = lambda wd: pl.BlockSpec((bt, t, wd), lambda i: (i, 0, 0))
    new_t = pl.BlockSpec((SWA_KV_DIM, bt * t), lambda i: (0, i))
    buf = pl.BlockSpec((bt, SWA_KV_DIM, nrows), lambda i: (i, 0, 0))
    kern = functools.partial(_swa_sample_kernel, steps=t)
    return pl.pallas_call(
        kern,
        grid=(b // bt,),
        in_specs=[new(SWA_Q_DIM), new(SWA_KV_DIM), new(SWA_KV_DIM), new_t, new_t, buf, buf, new(SWA_Q_DIM),
                  pl.BlockSpec(sink_rows.shape, lambda i: (0, 0, 0))],
        out_specs=[new(SWA_Q_DIM), buf, buf],
        out_shape=[jax.ShapeDtypeStruct((b, t, SWA_Q_DIM), F32),
                   jax.ShapeDtypeStruct((b, SWA_KV_DIM, nrows), F32),
                   jax.ShapeDtypeStruct((b, SWA_KV_DIM, nrows), F32)],
        compiler_params=pltpu.CompilerParams(dimension_semantics=("parallel",)),
        name="swa_sample",
    )(q, k, v, k_t, v_t, k_buf, v_buf, sgb, sink_rows)


def _out_kernel(x_ref, og_ref, zga_ref, osg_ref, gn_ref, wo_ref, nw2_ref, win_ref, wout_ref, nf_ref, y_ref):
    gn = gn_ref[...]
    heads = []
    for h in range(GDN_HEADS):
        lo = h * GDN_DV
        heads.append(_rms(og_ref[:, lo:lo + GDN_DV], gn))
    mixed = jnp.concatenate(heads, axis=1) * zga_ref[...] + osg_ref[...]
    x2 = x_ref[...] + jnp.dot(mixed.astype(BF16), wo_ref[...], preferred_element_type=F32)
    x3 = _swiglu_half_step(x2, nw2_ref[...], win_ref, wout_ref)
    y_ref[...] = _rms(x3, nf_ref[...])


def _out(x1, og, zga, osg, gnorm, wo, nw2, win, wout, nf, tm):
    t = x1.shape[0]
    row = pl.BlockSpec((tm, D_MODEL), lambda i: (i, 0))
    wbytes = (wo.size + win.size + wout.size) * 2
    return pl.pallas_call(
        _out_kernel,
        grid=(t // tm,),
        in_specs=[row] * 4 + [_const_spec(gnorm.shape), _const_spec(wo.shape), _const_spec(nw2.shape),
                              _const_spec(win.shape), _const_spec(wout.shape), _const_spec(nf.shape)],
        out_specs=row,
        out_shape=jax.ShapeDtypeStruct((t, D_MODEL), F32),
        compiler_params=pltpu.CompilerParams(
            dimension_semantics=("parallel",),
            vmem_limit_bytes=_vmem_limit(wbytes + 18 * tm * D_MODEL * 4 + (8 << 20))),
        name="out",
    )(x1, og, zga, osg, gnorm, wo, nw2, win, wout, nf)


def _rope_tables(pos):
    inv = ROPE_THETA ** (-jnp.arange(0, ROPE_DIM, 2, dtype=F32) / ROPE_DIM)
    ang = pos.astype(F32)[:, None] * inv[None, :]
    cos, sin = jnp.cos(ang), jnp.sin(ang)
    n = pos.shape[0]
    pad = SWA_HEAD_DIM - ROPE_DIM
    one_head = lambda first, second, fill: jnp.concatenate(
        [first, second, jnp.full((n, pad), fill, F32)], axis=1)
    zeros = jnp.zeros_like(sin)
    cos_f = one_head(cos, cos, 1.0)
    sin_a = one_head(-sin, zeros, 0.0)
    sin_b = one_head(zeros, sin, 0.0)
    reps = V7X_LANES // SWA_HEAD_DIM
    return tuple(jnp.tile(t, (1, reps)) for t in (cos_f, sin_a, sin_b))


def _pad_lanes(v, width):
    return jnp.pad(v.astype(F32), (0, width - v.shape[0]))[None, :]


def _branch(x, pos, conv_buf, s0, k_buf, v_buf, p):
    b, l, _ = x.shape
    t = b * l
    is_prompt = conv_buf is None
    chunk = min(GDN_CHUNK, l)
    n = l // chunk
    tm = _token_tile(l if is_prompt else t, TOKEN_TILE)
    tables = _rope_tables(pos)
    if is_prompt:
        halo = None
        table_tiles = l // tm
    else:
        halo = jnp.pad(conv_buf, ((0, 0), (l - (CONV_W - 1), 0), (0, 0))).reshape(t, CONV_DIM)
        heads_t = lambda tb: jnp.tile(tb.T, (SWA_KV_DIM // V7X_LANES, tm // l))
        tables_t = tuple(heads_t(tb) for tb in tables)
        tables = tuple(jnp.tile(tb, (tm // l, 1)) for tb in tables)
        table_tiles = 1
    tm_conv = tm if is_prompt else _token_tile(t, TOKEN_TILE // 2)
    x1, qg, kg, vg, tail = _ffn_conv(x.reshape(t, D_MODEL), p['norm_ffn1'], p['w_ffn1_in'], p['w_ffn1_out'],
                                     p['norm_mix'], p['w_qkv'], p['conv_w'], halo,
                                     tm=tm_conv, seq_len=l, group=l)
    mixed = _mix(x1, p['norm_mix'], p['w_mix'], p['a_log_pad'], p['dt_bias_pad'], tables,
                 None if is_prompt else tables_t, tm=tm, table_tiles=table_tiles, chunk=chunk)
    gcb, zga, qr, kr, vs, sgb = mixed[:6]
    seq = lambda a: a.reshape(b, l, a.shape[-1])
    qg, kg, vg, gcb, qr, kr, vs, sgb = (seq(a) for a in (qg, kg, vg, gcb, qr, kr, vs, sgb))
    new_conv = tail.reshape(b, -1, CONV_DIM)[:, -(CONV_W - 1):]

    gc = gcb[:, :, :GDN_HEADS]
    grow = gc.reshape(b, n, chunk, GDN_HEADS).transpose(0, 1, 3, 2)
    if is_prompt:
        assert chunk == PK_C and s0 is None
        og, new_s = _pk_gdn(qg, kg, vg, gcb, grow.reshape(b, n, 1, PK_CAT))
        sink_cols = jnp.repeat(p['sinks'].reshape(SWA_Q_HEADS // 2, 2), WINDOW, axis=1)
        osg = _swa_prompt(qr, kr, vs, sgb, sink_cols.reshape(SWA_Q_HEADS // 2, 1, 2 * WINDOW))
        new_k, new_v = kr[:, l - WINDOW:], vs[:, l - WINDOW:]
    else:
        assert n == 1
        og, new_s = _gdn_sample(qg, kg, vg, gcb, jnp.repeat(gc, l, axis=-1),
                                grow.reshape(b, 1, GDN_HEADS * l), s0, _token_tile(b, 8))
        sink_rows = jnp.repeat(p['sinks'], l, axis=1).reshape(SWA_KV_HEADS, SWA_GROUP * l, 1)
        nrows = k_buf.shape[1]
        to_minor = lambda c: c.transpose(0, 2, 3, 1).reshape(b, SWA_KV_DIM, nrows)
        from_minor = lambda c: c.reshape(b, SWA_KV_HEADS, SWA_HEAD_DIM, nrows).transpose(0, 3, 1, 2)
        osg, new_k, new_v = _swa_sample(qr, kr, vs, mixed[6], mixed[7], to_minor(k_buf), to_minor(v_buf),
                                        sgb, sink_rows, nrows // l)
        new_k, new_v = from_minor(new_k), from_minor(new_v)
    y = _out(x1, og.reshape(t, GDN_V_DIM), zga, osg.reshape(t, SWA_Q_DIM), p['gdn_norm'], p['w_out'],
             p['norm_ffn2'], p['w_ffn2_in'], p['w_ffn2_out'], p['norm_final'], tm)
    kv_shape = (b, new_k.shape[1], SWA_KV_HEADS, SWA_HEAD_DIM)
    return (y.reshape(b, l, D_MODEL), new_conv, new_s, new_k.reshape(kv_shape), new_v.reshape(kv_shape))


def _layer_params(l, norm_ffn1, w_ffn1_in, w_ffn1_out, norm_mix, w_in, conv_w, gdn_a_log, gdn_dt_bias,
                  gdn_norm, swa_sinks, w_out, norm_ffn2, w_ffn2_in, w_ffn2_out, norm_final):
    bounds = [0]
    for s in IN_SPLITS:
        bounds.append(bounds[-1] + s)
    w_in_t = jnp.swapaxes(w_in[l], 0, 1)
    col = lambda i: w_in_t[bounds[i]:bounds[i + 1]].astype(BF16)
    w_ab = jnp.pad(jnp.concatenate([col(2), col(3)], axis=0), ((0, V7X_LANES - 2 * GDN_HEADS), (0, 0)))
    row = lambda v: v.astype(F32)[None, :]
    return {
        'norm_ffn1': row(norm_ffn1[l]), 'w_ffn1_in': w_ffn1_in[l].astype(BF16),
        'w_ffn1_out': w_ffn1_out[l].astype(BF16),
        'norm_mix': row(norm_mix[l]),
        'w_qkv': col(0),
        'w_mix': [w_ab, col(1), col(7), col(4), col(5), col(6), col(8)],
        'conv_w': conv_w[l].astype(F32),
        'a_log_pad': _pad_lanes(gdn_a_log[l], V7X_LANES), 'dt_bias_pad': _pad_lanes(gdn_dt_bias[l], V7X_LANES),
        'gdn_norm': row(gdn_norm[l]),
        'sinks': swa_sinks[l].astype(F32).reshape(SWA_KV_HEADS, SWA_GROUP),
        'w_out': w_out[l].astype(BF16),
        'norm_ffn2': row(norm_ffn2[l]), 'w_ffn2_in': w_ffn2_in[l].astype(BF16),
        'w_ffn2_out': w_ffn2_out[l].astype(BF16),
        'norm_final': row(norm_final),
    }


def kernel(x_prompt, x_sample, state_conv, state_gdn, cache_swa_k, cache_swa_v, norm_ffn1, w_ffn1_in, w_ffn1_out, norm_mix, w_in, conv_w, gdn_a_log, gdn_dt_bias, gdn_norm, swa_sinks, w_out, norm_ffn2, w_ffn2_in, w_ffn2_out, norm_final):
    depth = w_in.shape[0]
    assert depth == 1, "the final norm is fused into the layer's last kernel"
    l_p = x_prompt.shape[1]
    l_s = x_sample.shape[1]
    pos_p = jnp.arange(l_p, dtype=jnp.int32)
    pos_s = PAST_LEN + jnp.arange(l_s, dtype=jnp.int32)
    p = _layer_params(0, norm_ffn1, w_ffn1_in, w_ffn1_out, norm_mix, w_in, conv_w, gdn_a_log, gdn_dt_bias,
                      gdn_norm, swa_sinks, w_out, norm_ffn2, w_ffn2_in, w_ffn2_out, norm_final)
    yp, c1, s1, k1, v1 = _branch(x_prompt, pos_p, None, None, None, None, p)
    ys, c2, s2, k2, v2 = _branch(x_sample, pos_s, state_conv[0], state_gdn[0], cache_swa_k[0],
                                 cache_swa_v[0], p)
    return (yp, ys, c1[None], c2[None], s1[None], s2[None], k1[None], k2[None], v1[None], v2[None])
```

```python
import functools

import jax
import jax.numpy as jnp
from jax import lax
from jax.experimental import pallas as pl
from jax.experimental.pallas import tpu as pltpu

D_MODEL = 1024
PAST_LEN = 16384
GDN_HEADS = 8
GDN_DK = 128
GDN_DV = 128
GDN_QK_DIM = GDN_HEADS * GDN_DK
GDN_V_DIM = GDN_HEADS * GDN_DV
CONV_W = 4
GDN_CHUNK = 64
CONV_DIM = 2 * GDN_QK_DIM + GDN_V_DIM
SWA_Q_HEADS = 16
SWA_KV_HEADS = 4
SWA_GROUP = SWA_Q_HEADS // SWA_KV_HEADS
SWA_HEAD_DIM = 64
SWA_Q_DIM = SWA_Q_HEADS * SWA_HEAD_DIM
SWA_KV_DIM = SWA_KV_HEADS * SWA_HEAD_DIM
WINDOW = 128
ROPE_DIM = SWA_HEAD_DIM // 4
ROPE_HALF = ROPE_DIM // 2
ROPE_THETA = 500000.0
D_FF = 2816
EPS = 1e-6
LOG2_E = 1.4426950408889634
IN_SPLITS = (CONV_DIM, GDN_V_DIM, GDN_HEADS, GDN_HEADS, SWA_Q_DIM, SWA_KV_DIM, SWA_KV_DIM, D_MODEL, D_MODEL)

V7X_LANES = 128
V7X_SUBLANES = 8
V7X_MXU_DIM = 256
V7X_VMEM_BYTES = 64 * 1024 * 1024

TOKEN_TILE = 2 * V7X_MXU_DIM
MIX_TILE = 4 * V7X_MXU_DIM
FFN_CHUNK = V7X_MXU_DIM
CONV_PIECE = V7X_MXU_DIM
SOLVE_BASE = V7X_SUBLANES
SWA_WINDOWS_PER_STEP = 4

F32 = jnp.float32
BF16 = jnp.bfloat16


def _dot(a, b):
    return jnp.dot(a.astype(BF16), b.astype(BF16), preferred_element_type=F32)


def _dot_nt(a, b):
    return lax.dot_general(a.astype(BF16), b.astype(BF16), (((1,), (1,)), ((), ())),
                           preferred_element_type=F32)


def _dot_tn(a, b):
    return lax.dot_general(a.astype(BF16), b.astype(BF16), (((0,), (0,)), ((), ())),
                           preferred_element_type=F32)


def _proj(xn, w_t):
    return lax.dot_general(xn, w_t, (((1,), (1,)), ((), ())), preferred_element_type=F32)


def _rms(x, w):
    return x * lax.rsqrt(jnp.mean(x * x, axis=-1, keepdims=True) + EPS) * w


def _sigmoid(x):
    return 0.5 * jnp.tanh(0.5 * x) + 0.5


def _silu(x):
    h = 0.5 * x
    return h * jnp.tanh(h) + h


def _vmem_limit(nbytes):
    return int(min(V7X_VMEM_BYTES - (4 << 20), max(nbytes, 16 << 20)))


def _const_spec(shape):
    nd = len(shape)
    return pl.BlockSpec(shape, lambda *_: (0,) * nd, pipeline_mode=pl.Buffered(1))


def _token_tile(t, want):
    tm = min(want, t)
    while t % tm:
        tm //= 2
    return tm


FFN_NCHUNK = D_FF // FFN_CHUNK


def _swiglu_act(xn, win_ref, c):
    lo = c * FFN_CHUNK
    g = jnp.dot(xn, win_ref[:, lo:lo + FFN_CHUNK], preferred_element_type=F32)
    u = jnp.dot(xn, win_ref[:, D_FF + lo:D_FF + lo + FFN_CHUNK], preferred_element_type=F32)
    return (_silu(g) * u).astype(BF16)


def _swiglu_half_step(x, norm_w, win_ref, wout_ref):
    xn = _rms(x, norm_w).astype(BF16)
    act = jnp.concatenate([_swiglu_act(xn, win_ref, c) for c in range(FFN_NCHUNK)], axis=1)
    return x + 0.5 * jnp.dot(act, wout_ref[...], preferred_element_type=F32)


def _ffn_conv_kernel(*refs, tiles_per_seq, group):
    carried = tiles_per_seq > 0
    if carried:
        (x_ref, nw1_ref, win_ref, wout_ref, nw_ref, wqkv_ref, cw_ref,
         x1_ref, qg_ref, kg_ref, vg_ref, tail_ref, xn_ref, carry_ref) = refs
    else:
        (x_ref, nw1_ref, win_ref, wout_ref, nw_ref, wqkv_ref, cw_ref, halo_ref,
         x1_ref, qg_ref, kg_ref, vg_ref, tail_ref, xn_ref) = refs
    rows = x_ref.shape[0]
    step = pl.program_id(0)
    n_tiles = pl.num_programs(0) - 1
    if carried:
        seq_start = ((step - 1) % tiles_per_seq) == 0
    else:
        pos = lax.broadcasted_iota(jnp.int32, (rows, 1), 0) % group

    def conv_piece(c):
        cols = slice(c * CONV_PIECE, (c + 1) * CONV_PIECE)
        raw = _proj(xn_ref[...], wqkv_ref[cols, :])
        cw = cw_ref[:, cols]
        acc = raw * cw[CONV_W - 1:CONV_W, :]
        if carried:
            halo = jnp.where(seq_start, 0.0, carry_ref[:, cols])
            ext = jnp.concatenate([halo, raw], axis=0)
            for s in range(1, CONV_W):
                acc = acc + pltpu.roll(ext, s, 0)[V7X_SUBLANES:] * cw[CONV_W - 1 - s:CONV_W - s, :]
            carry_ref[:, cols] = raw[rows - V7X_SUBLANES:]
            tail_ref[0, :, cols] = raw[rows - V7X_SUBLANES:]
        else:
            halo = halo_ref[:, cols]
            for s in range(1, CONV_W):
                shifted = jnp.where(pos >= s, pltpu.roll(raw, s, 0),
                                    pltpu.roll(halo, (rows + s - group) % rows, 0))
                acc = acc + shifted * cw[CONV_W - 1 - s:CONV_W - s, :]
            tail_ref[:, cols] = raw
        conv = _silu(acc)
        part, lo = divmod(c * CONV_PIECE, GDN_QK_DIM)
        out_ref = (qg_ref, kg_ref, vg_ref)[part]
        if part == 2:
            out_ref[:, lo:lo + CONV_PIECE] = conv.astype(BF16)
        else:
            for h in range(CONV_PIECE // GDN_DK):
                ch = conv[:, h * GDN_DK:(h + 1) * GDN_DK]
                out_ref[:, lo + h * GDN_DK:lo + (h + 1) * GDN_DK] = (
                    ch * lax.rsqrt(jnp.sum(ch * ch, axis=-1, keepdims=True) + EPS)).astype(BF16)

    def body(do_ffn, do_conv):
        n_pieces = CONV_DIM // CONV_PIECE
        n_out = D_MODEL // FFN_CHUNK
        slots = FFN_NCHUNK + n_out
        due = [((k + 1) * n_pieces) // slots for k in range(slots)] if do_ffn else [n_pieces]
        done = 0
        if do_ffn:
            x = x_ref[...]
            xn = _rms(x, nw1_ref[...]).astype(BF16)
            acts, outs = [], []
        for k, upto in enumerate(due):
            if do_ffn and k < FFN_NCHUNK:
                acts.append(_swiglu_act(xn, win_ref, k))
            elif do_ffn:
                if k == FFN_NCHUNK:
                    act = jnp.concatenate(acts, axis=1)
                lo = (k - FFN_NCHUNK) * FFN_CHUNK
                outs.append(jnp.dot(act, wout_ref[:, lo:lo + FFN_CHUNK], preferred_element_type=F32))
            if do_conv:
                for c in range(done, upto):
                    conv_piece(c)
                done = upto
        if do_ffn:
            x1 = x + 0.5 * jnp.concatenate(outs, axis=1)
            x1_ref[...] = x1
            xn_ref[...] = _rms(x1, nw_ref[...]).astype(BF16)

    @pl.when(step == 0)
    def _():
        if carried:
            carry_ref[...] = jnp.zeros_like(carry_ref)
        body(True, False)

    @pl.when((step > 0) & (step < n_tiles))
    def _():
        body(True, True)

    @pl.when(step == n_tiles)
    def _():
        body(False, True)


def _ffn_conv(x, norm1, win, wout, norm_w, wqkv, cw, halo, *, tm, seq_len, group):
    t = x.shape[0]
    carried = halo is None
    tiles_per_seq = seq_len // tm if carried else 0
    n_tiles = t // tm
    cur = lambda n: pl.BlockSpec((tm, n), lambda i: (jnp.minimum(i, n_tiles - 1), 0))
    prev = lambda n: pl.BlockSpec((tm, n), lambda i: (jnp.maximum(i - 1, 0), 0))
    weights = [win, wout, wqkv]
    in_specs = [cur(D_MODEL), _const_spec((1, D_MODEL)), _const_spec(win.shape), _const_spec(wout.shape),
                _const_spec((1, D_MODEL)), _const_spec(wqkv.shape), _const_spec(cw.shape)]
    args = [x, norm1, win, wout, norm_w, wqkv, cw]
    out_specs = [cur(D_MODEL), prev(GDN_QK_DIM), prev(GDN_QK_DIM), prev(GDN_V_DIM)]
    out_shape = [jax.ShapeDtypeStruct((t, D_MODEL), F32)] + [
        jax.ShapeDtypeStruct((t, n), BF16) for n in (GDN_QK_DIM, GDN_QK_DIM, GDN_V_DIM)]
    scratch = [pltpu.VMEM((tm, D_MODEL), BF16)]
    if carried:
        nseq = t // seq_len
        out_specs.append(pl.BlockSpec((1, V7X_SUBLANES, CONV_DIM),
                                      lambda i: (jnp.maximum(i - 1, 0) // tiles_per_seq, 0, 0)))
        out_shape.append(jax.ShapeDtypeStruct((nseq, V7X_SUBLANES, CONV_DIM), F32))
        scratch.append(pltpu.VMEM((V7X_SUBLANES, CONV_DIM), F32))
    else:
        in_specs.append(prev(CONV_DIM))
        args.append(halo)
        out_specs.append(prev(CONV_DIM))
        out_shape.append(jax.ShapeDtypeStruct((t, CONV_DIM), F32))
    wbytes = sum(w.size for w in weights) * 2
    kern = functools.partial(_ffn_conv_kernel, tiles_per_seq=tiles_per_seq, group=group)
    return pl.pallas_call(
        kern,
        grid=(n_tiles + 1,),
        in_specs=in_specs,
        out_specs=out_specs,
        out_shape=out_shape,
        scratch_shapes=scratch,
        compiler_params=pltpu.CompilerParams(
            dimension_semantics=("arbitrary",),
            vmem_limit_bytes=_vmem_limit(wbytes + 32 * tm * D_MODEL * 4 + (4 << 20))),
        name="ffn_conv",
    )(*args)


def _rope(x, cos_f, sin_a, sin_b):
    width = x.shape[1]
    reps = width // V7X_LANES
    tile = lambda t: jnp.concatenate([t] * reps, axis=1) if reps > 1 else t
    return (x * tile(cos_f) + pltpu.roll(x, width - ROPE_HALF, 1) * tile(sin_a)
            + pltpu.roll(x, ROPE_HALF, 1) * tile(sin_b))


def _mix_kernel(*refs, chunk, kv_t):
    (x_ref, nw_ref, wab_ref, wz_ref, wga_ref, wqs_ref, wks_ref, wvs_ref, wgb_ref,
     alog_ref, dtb_ref, cos_ref, sina_ref, sinb_ref) = refs[:14]
    if kv_t:
        cost_ref, sinat_ref, sinbt_ref = refs[14:17]
        gcb_ref, zga_ref, qr_ref, kr_ref, vs_ref, sgb_ref, krt_ref, vst_ref = refs[17:]
    else:
        gcb_ref, zga_ref, qr_ref, kr_ref, vs_ref, sgb_ref = refs[14:]
    rows = x_ref.shape[0]
    xn = _rms(x_ref[...], nw_ref[...]).astype(BF16)

    ab = _proj(xn, wab_ref[...])
    sp_in = ab + dtb_ref[...]
    softplus = jnp.maximum(sp_in, 0.0) + jnp.log1p(jnp.exp(-jnp.abs(sp_in)))
    gc = -jnp.exp(alog_ref[...]) * softplus
    cpos = lax.broadcasted_iota(jnp.int32, (rows, 1), 0) % chunk
    s = 1
    while s < chunk:
        gc = gc + jnp.where(cpos >= s, pltpu.roll(gc, s, 0), 0.0)
        s *= 2
    lane = lax.broadcasted_iota(jnp.int32, (1, V7X_LANES), 1)
    gcb_ref[...] = jnp.where(lane < GDN_HEADS, gc, jax.nn.sigmoid(ab))

    zga_ref[...] = _silu(_proj(xn, wz_ref[...])) * _sigmoid(_proj(xn, wga_ref[...]))
    cos_f, sin_a, sin_b = cos_ref[...], sina_ref[...], sinb_ref[...]
    qr_ref[...] = _rope(_proj(xn, wqs_ref[...]), cos_f, sin_a, sin_b)
    kr_ref[...] = _rope(_proj(xn, wks_ref[...]), cos_f, sin_a, sin_b)
    vs_ref[...] = _proj(xn, wvs_ref[...])
    sgb_ref[...] = _sigmoid(_proj(xn, wgb_ref[...]))
    if kv_t:
        ks_t = lax.dot_general(wks_ref[...], xn, (((1,), (1,)), ((), ())), preferred_element_type=F32)
        krt_ref[...] = (ks_t * cost_ref[...] + pltpu.roll(ks_t, SWA_KV_DIM - ROPE_HALF, 0) * sinat_ref[...]
                        + pltpu.roll(ks_t, ROPE_HALF, 0) * sinbt_ref[...])
        vst_ref[...] = lax.dot_general(wvs_ref[...], xn, (((1,), (1,)), ((), ())),
                                       preferred_element_type=F32)


def _mix(x, norm_w, weights, alog, dtb, tables, tables_t, *, tm, table_tiles, chunk):
    t = x.shape[0]
    kv_t = tables_t is not None
    row = lambda n: pl.BlockSpec((tm, n), lambda i: (i, 0))
    tab = pl.BlockSpec((tm, V7X_LANES), lambda i: (i % table_tiles, 0))
    widths = (V7X_LANES, GDN_V_DIM, SWA_Q_DIM, SWA_KV_DIM, SWA_KV_DIM, D_MODEL)
    in_specs = ([row(D_MODEL), _const_spec((1, D_MODEL))] + [_const_spec(w.shape) for w in weights]
                + [_const_spec(alog.shape), _const_spec(dtb.shape), tab, tab, tab])
    args = [x, norm_w, *weights, alog, dtb, *tables]
    out_specs = [row(n) for n in widths]
    out_shape = [jax.ShapeDtypeStruct((t, n), F32) for n in widths]
    if kv_t:
        in_specs += [_const_spec((SWA_KV_DIM, tm))] * 3
        args += list(tables_t)
        out_specs += [pl.BlockSpec((SWA_KV_DIM, tm), lambda i: (0, i))] * 2
        out_shape += [jax.ShapeDtypeStruct((SWA_KV_DIM, t), F32)] * 2
    wbytes = sum(w.size for w in weights) * 2
    return pl.pallas_call(
        functools.partial(_mix_kernel, chunk=chunk, kv_t=kv_t),
        grid=(t // tm,),
        in_specs=in_specs,
        out_specs=out_specs,
        out_shape=out_shape,
        compiler_params=pltpu.CompilerParams(
            dimension_semantics=("parallel",),
            vmem_limit_bytes=_vmem_limit(wbytes + 36 * tm * D_MODEL * 4 + (4 << 20))),
        name="mix",
    )(*args)


PK_C = GDN_CHUNK
PK_CAT = GDN_HEADS * PK_C
PK_PAIR = 2 * GDN_DK
PK_NPAIR = GDN_HEADS // 2
PK_GROUP = V7X_MXU_DIM // PK_C
PK_CHUNKS_PER_STEP = 4


def _pk_masks():
    c = PK_C
    i = lax.broadcasted_iota(jnp.int32, (c, PK_CAT), 0)
    j = lax.broadcasted_iota(jnp.int32, (c, PK_CAT), 1) % c
    lo = (lax.broadcasted_iota(jnp.int32, (c, V7X_LANES), 1) < c).astype(BF16)
    return i, j, (lo, 1 - lo)


def _pk_prod(x, y, bd_mask):
    lo, hi = bd_mask
    zero = jnp.zeros((PK_C, V7X_LANES), BF16)
    outs = []
    for g in range(GDN_HEADS // PK_GROUP):
        sl = slice(g * V7X_MXU_DIM, (g + 1) * V7X_MXU_DIM)
        yb = y[:, sl].astype(BF16)
        t0, t1 = yb[:, :V7X_LANES], yb[:, V7X_LANES:]
        bd = jnp.concatenate([jnp.concatenate([t0 * lo, zero], axis=1),
                              jnp.concatenate([t0 * hi, zero], axis=1),
                              jnp.concatenate([zero, t1 * lo], axis=1),
                              jnp.concatenate([zero, t1 * hi], axis=1)], axis=0)
        outs.append(jnp.dot(x[:, sl].astype(BF16), bd, preferred_element_type=F32))
    return jnp.concatenate(outs, axis=1)


def _pk_solve(a_list, i, j, bd_mask):
    prod_all = lambda xs, ys: [_pk_prod(x, y, bd_mask) for x, y in zip(xs, ys)]
    eye = (i == j).astype(F32)
    b = SOLVE_BASE
    same = (i // b) == (j // b)
    p = [jnp.where(same, a, 0.0) for a in a_list]
    d = [eye - x for x in p]
    span = 2
    while span < b:
        p = prod_all(p, p)
        d = [x + y for x, y in zip(d, prod_all(d, p))]
        span *= 2
    while b < PK_C:
        same2 = (i // (2 * b)) == (j // (2 * b))
        sel = same2 & jnp.logical_not(same)
        off = [jnp.where(sel, a, 0.0) for a in a_list]
        d = [x - y for x, y in zip(d, prod_all(d, prod_all(off, d)))]
        same = same2
        b *= 2
    return d


def _pk_pre_body(q_ref, k_ref, v_ref, gcb_ref, grow_ref, u_ref, w_ref, qg_ref, kg_ref, qk_ref, gl_ref):
    c = PK_C
    i, j, bd_mask = _pk_masks()
    lane_lo = lax.broadcasted_iota(jnp.int32, (c, V7X_LANES), 1) < c
    zeros_k = jnp.zeros((c, GDN_DK), BF16)
    chunks = [(b, ch) for b in range(q_ref.shape[0]) for ch in range(q_ref.shape[1] // c)]
    a_list, vb_list, kbg_list = [], [], []
    for b, ch in chunks:
        rows = slice(ch * c, (ch + 1) * c)
        gcb = gcb_ref[b, rows, :]
        g_end = gcb[c - 1:c, :]
        eg_s = jnp.exp(gcb)
        ek_s = jnp.exp(g_end - gcb)
        col = lambda arr, n: jnp.broadcast_to(arr[:, n:n + 1], (c, V7X_LANES))
        full = lambda arr, base: jnp.concatenate([col(arr, base + h) for h in range(GDN_HEADS)], axis=1)
        gc_cols = [col(gcb, h) for h in range(GDN_HEADS)]
        beta_f = full(gcb, GDN_HEADS)
        eg_f = full(eg_s, 0)
        ek_f = full(ek_s, 0)
        gc_cat = jnp.concatenate([jnp.where(lane_lo, gc_cols[2 * m], gc_cols[2 * m + 1])
                                  for m in range(PK_NPAIR)], axis=1)
        decay = jnp.exp(jnp.where(i >= j, gc_cat - grow_ref[b, ch], -jnp.inf))
        q = q_ref[b, rows, :].astype(F32) * (GDN_DK ** -0.5)
        k = k_ref[b, rows, :].astype(F32)
        kb = k * beta_f
        vb_list.append((v_ref[b, rows, :].astype(F32) * beta_f).astype(BF16))
        kbg_list.append((kb * eg_f).astype(BF16))
        qg_ref[b, rows, :] = (q * eg_f).astype(BF16)
        kg_ref[b, rows, :] = (k * ek_f).astype(BF16)
        gl_ref[b, ch] = eg_f[c - 1:c, :]
        kk = []
        for m in range(PK_NPAIR):
            sl = slice(m * PK_PAIR, (m + 1) * PK_PAIR)
            lhs = jnp.concatenate([kb[:, sl], q[:, sl]], axis=0).astype(BF16)
            k2 = k[:, sl].astype(BF16)
            rhs_nt = jnp.concatenate(
                [jnp.concatenate([k2[:, :GDN_DK], zeros_k], axis=1),
                 jnp.concatenate([zeros_k, k2[:, GDN_DK:]], axis=1)], axis=0)
            kk.append(lax.dot_general(lhs, rhs_nt, (((1,), (1,)), ((), ())), preferred_element_type=F32))
        a_raw = jnp.concatenate([x[:c] for x in kk], axis=1)
        qk_raw = jnp.concatenate([x[c:] for x in kk], axis=1)
        a_list.append(jnp.where(i > j, a_raw * decay, 0.0))
        qk_ref[b, rows, :] = (qk_raw * decay).astype(BF16)
    t_list = _pk_solve(a_list, i, j, bd_mask)
    for n, (b, ch) in enumerate(chunks):
        rows = slice(ch * c, (ch + 1) * c)
        t = t_list[n]
        for m in range(PK_NPAIR):
            tp = t[:, m * V7X_LANES:(m + 1) * V7X_LANES]
            lhs = jnp.concatenate([jnp.where(lane_lo, tp, 0.0), jnp.where(lane_lo, 0.0, tp)],
                                  axis=0).astype(BF16)
            h0, h1 = 2 * m, 2 * m + 1
            hs = lambda h: slice(h * GDN_DK, (h + 1) * GDN_DK)
            rhs = jnp.concatenate(
                [jnp.concatenate([vb_list[n][:, hs(h0)], kbg_list[n][:, hs(h0)]], axis=1),
                 jnp.concatenate([vb_list[n][:, hs(h1)], kbg_list[n][:, hs(h1)]], axis=1)], axis=0)
            uw = jnp.dot(lhs, rhs, preferred_element_type=F32)
            u_ref[b, rows, hs(h0)] = uw[:c, :GDN_DV]
            w_ref[b, rows, hs(h0)] = uw[:c, GDN_DV:].astype(BF16)
            u_ref[b, rows, hs(h1)] = uw[c:, :GDN_DV]
            w_ref[b, rows, hs(h1)] = uw[c:, GDN_DV:].astype(BF16)


def _pk_kernel(q_ref, k_ref, v_ref, gcb_ref, grow_ref, o_ref, sout_ref,
               u_ref, w_ref, qg_ref, kg_ref, qk_ref, gl_ref, s_ref):
    c = PK_C
    nb = u_ref.shape[0]
    step = pl.program_id(0)

    @pl.when(step == 0)
    def _():
        s_ref[...] = jnp.zeros_like(s_ref)

    _pk_pre_body(q_ref, k_ref, v_ref, gcb_ref, grow_ref, u_ref, w_ref, qg_ref, kg_ref, qk_ref, gl_ref)

    zeros_s = jnp.zeros((GDN_DK, GDN_DV), BF16)
    zeros_v = jnp.zeros((c, GDN_DV), BF16)
    chains = [(b, m) for b in range(nb) for m in range(PK_NPAIR)]
    pair = lambda m: slice(m * PK_PAIR, (m + 1) * PK_PAIR)
    state = {bm: s_ref[bm] for bm in chains}

    for ch in range(u_ref.shape[1] // c):
        rows = slice(ch * c, (ch + 1) * c)
        wq, bd_v = {}, {}

        def state_products(b, m):
            sb = state[b, m].astype(BF16)
            bd_s = jnp.concatenate([jnp.concatenate([sb[:, :GDN_DV], zeros_s], axis=1),
                                    jnp.concatenate([zeros_s, sb[:, GDN_DV:]], axis=1)], axis=0)
            lhs = jnp.concatenate([w_ref[b, rows, pair(m)], qg_ref[b, rows, pair(m)]], axis=0)
            wq[b, m] = jnp.dot(lhs, bd_s, preferred_element_type=F32)

        def new_values(b, m):
            v_new = (u_ref[b, rows, pair(m)] - wq[b, m][:c]).astype(BF16)
            bd_v[b, m] = jnp.concatenate([jnp.concatenate([v_new[:, :GDN_DV], zeros_v], axis=1),
                                          jnp.concatenate([zeros_v, v_new[:, GDN_DV:]], axis=1)], axis=0)

        def outputs_and_update(b, m):
            qk_p = qk_ref[b, rows, m * V7X_LANES:(m + 1) * V7X_LANES]
            o_ref[b, rows, pair(m)] = wq[b, m][c:] + jnp.dot(qk_p, bd_v[b, m], preferred_element_type=F32)
            kg_p = kg_ref[b, rows, pair(m)]
            kg_st = jnp.concatenate([kg_p[:, :GDN_DK], kg_p[:, GDN_DK:]], axis=0)
            upd = lax.dot_general(kg_st, bd_v[b, m], (((0,), (0,)), ((), ())),
                                  preferred_element_type=F32)
            state[b, m] = state[b, m] * gl_ref[b, ch, :, pair(m)] + upd

        for stage in (state_products, new_values, outputs_and_update):
            for b, m in chains:
                stage(b, m)

    for bm in chains:
        s_ref[bm] = state[bm]

    @pl.when(step == pl.num_programs(0) - 1)
    def _():
        for b, m in chains:
            s_c = s_ref[b, m]
            sout_ref[b, 2 * m] = s_c[:, :GDN_DV]
            sout_ref[b, 2 * m + 1] = s_c[:, GDN_DV:]


def _pk_gdn(q, k, v, gcb, grow_flat):
    b, l, _ = q.shape
    n = l // PK_C
    per = _token_tile(n, PK_CHUNKS_PER_STEP)
    rows = per * PK_C
    tok = lambda wd: pl.BlockSpec((b, rows, wd), lambda s: (0, s, 0))
    return pl.pallas_call(
        _pk_kernel,
        grid=(n // per,),
        in_specs=[tok(GDN_QK_DIM), tok(GDN_QK_DIM), tok(GDN_V_DIM), tok(V7X_LANES),
                  pl.BlockSpec((b, per, 1, PK_CAT), lambda s: (0, s, 0, 0))],
        out_specs=[tok(GDN_V_DIM),
                   pl.BlockSpec((b, GDN_HEADS, GDN_DK, GDN_DV), lambda s: (0, 0, 0, 0))],
        out_shape=[jax.ShapeDtypeStruct((b, l, GDN_V_DIM), F32),
                   jax.ShapeDtypeStruct((b, GDN_HEADS, GDN_DK, GDN_DV), F32)],
        scratch_shapes=[pltpu.VMEM((b, rows, GDN_V_DIM), F32),
                        pltpu.VMEM((b, rows, GDN_QK_DIM), BF16),
                        pltpu.VMEM((b, rows, GDN_QK_DIM), BF16),
                        pltpu.VMEM((b, rows, GDN_QK_DIM), BF16),
                        pltpu.VMEM((b, rows, PK_CAT), BF16),
                        pltpu.VMEM((b, per, 1, GDN_V_DIM), F32),
                        pltpu.VMEM((b, PK_NPAIR, GDN_DK, 2 * GDN_DV), F32)],
        compiler_params=pltpu.CompilerParams(dimension_semantics=("arbitrary",),
                                             vmem_limit_bytes=_vmem_limit(V7X_VMEM_BYTES)),
        name="gdn_prompt",
    )(q, k, v, gcb, grow_flat)


def _head_block_rows(x, t):
    width = x.shape[1] // GDN_HEADS
    z = jnp.zeros((t, width), x.dtype)
    rows = []
    for h in range(GDN_HEADS):
        rows.append(jnp.concatenate([z] * h + [x[:, h * width:(h + 1) * width]]
                                    + [z] * (GDN_HEADS - 1 - h), axis=1))
    return jnp.concatenate(rows, axis=0)


def _gdn_sample_kernel(q_ref, k_ref, v_ref, gcb_ref, grep_ref, grow_ref, s0_ref, o_ref, sout_ref):
    bt, t, _ = q_ref.shape
    cat = GDN_HEADS * t
    i = lax.broadcasted_iota(jnp.int32, (t, cat), 0)
    j = lax.broadcasted_iota(jnp.int32, (t, cat), 1) % t
    r = lax.broadcasted_iota(jnp.int32, (cat, cat), 0) // t
    l = lax.broadcasted_iota(jnp.int32, (cat, cat), 1) // t
    same_head = r == l
    eye = (i == j).astype(F32)
    seqs = range(bt)
    pair = lambda m: slice(m * PK_PAIR, (m + 1) * PK_PAIR)

    vb, kbg, qg, kg, gl, a, qk = {}, {}, {}, {}, {}, {}, {}
    for b in seqs:
        gcb = gcb_ref[b]
        g_end = gcb[t - 1:t, :]
        eg_s = jnp.exp(gcb)
        ek_s = jnp.exp(g_end - gcb)
        col = lambda arr, n: jnp.broadcast_to(arr[:, n:n + 1], (t, V7X_LANES))
        full = lambda arr, base: jnp.concatenate([col(arr, base + h) for h in range(GDN_HEADS)], axis=1)
        beta_f, eg_f, ek_f = full(gcb, GDN_HEADS), full(eg_s, 0), full(ek_s, 0)
        decay = jnp.exp(jnp.where(i >= j, grep_ref[b] - grow_ref[b], -jnp.inf))
        q = q_ref[b].astype(F32) * (GDN_DK ** -0.5)
        k = k_ref[b].astype(F32)
        kb = k * beta_f
        vb[b] = v_ref[b].astype(F32) * beta_f
        kbg[b] = kb * eg_f
        qg[b] = q * eg_f
        kg[b] = k * ek_f
        gl[b] = eg_f[t - 1:t, :]
        kk = _dot_nt(jnp.concatenate([kb, q], axis=0), _head_block_rows(k, t))
        a[b] = jnp.where(i > j, kk[:t] * decay, 0.0)
        qk[b] = kk[t:] * decay

    def prod_all(xs, ys):
        return {b: _dot(xs[b], jnp.where(same_head, jnp.concatenate([ys[b]] * GDN_HEADS, axis=0), 0.0))
                for b in seqs}

    d = {b: eye - a[b] for b in seqs}
    p = a
    span = 2
    while span < t:
        p = prod_all(p, p)
        dp = prod_all(d, p)
        d = {b: d[b] + dp[b] for b in seqs}
        span *= 2

    u, w = {}, {}
    for b in seqs:
        rhs = jnp.concatenate([_head_block_rows(vb[b], t), _head_block_rows(kbg[b], t)], axis=1)
        uw = _dot(d[b], rhs)
        u[b], w[b] = uw[:, :GDN_V_DIM], uw[:, GDN_V_DIM:]

    zeros_s = jnp.zeros((GDN_DK, GDN_DV), BF16)
    wq = {}
    for b in seqs:
        for m in range(PK_NPAIR):
            s0 = s0_ref[b, 2 * m].astype(BF16)
            s1 = s0_ref[b, 2 * m + 1].astype(BF16)
            bd_s = jnp.concatenate([jnp.concatenate([s0, zeros_s], axis=1),
                                    jnp.concatenate([zeros_s, s1], axis=1)], axis=0)
            lhs = jnp.concatenate([w[b][:, pair(m)], qg[b][:, pair(m)]], axis=0).astype(BF16)
            wq[b, m] = jnp.dot(lhs, bd_s, preferred_element_type=F32)
    v_new = {b: u[b] - jnp.concatenate([wq[b, m][:t] for m in range(PK_NPAIR)], axis=1) for b in seqs}
    zeros_v = jnp.zeros((t, GDN_DV), F32)
    for b in seqs:
        o_ref[b] = (jnp.concatenate([wq[b, m][t:] for m in range(PK_NPAIR)], axis=1)
                    + _dot(qk[b], _head_block_rows(v_new[b], t)))
        for m in range(PK_NPAIR):
            vp = v_new[b][:, pair(m)]
            bd_v = jnp.concatenate([jnp.concatenate([vp[:, :GDN_DV], zeros_v], axis=1),
                                    jnp.concatenate([zeros_v, vp[:, GDN_DV:]], axis=1)], axis=0)
            kp = kg[b][:, pair(m)]
            kg_st = jnp.concatenate([kp[:, :GDN_DK], kp[:, GDN_DK:]], axis=0)
            upd = _dot_tn(kg_st, bd_v)
            glp = gl[b][:, pair(m)]
            sout_ref[b, 2 * m] = s0_ref[b, 2 * m] * glp[:, :GDN_DV] + upd[:, :GDN_DV]
            sout_ref[b, 2 * m + 1] = s0_ref[b, 2 * m + 1] * glp[:, GDN_DV:] + upd[:, GDN_DV:]


def _gdn_sample(q, k, v, gcb, grep, grow_flat, s0, bt):
    b, t, _ = q.shape
    assert t == V7X_SUBLANES, "one sublane tile of steps per sequence"
    cat = GDN_HEADS * t
    tok = lambda wd: pl.BlockSpec((bt, t, wd), lambda i: (i, 0, 0))
    state = pl.BlockSpec((bt, GDN_HEADS, GDN_DK, GDN_DV), lambda i: (i, 0, 0, 0))
    return pl.pallas_call(
        _gdn_sample_kernel,
        grid=(b // bt,),
        in_specs=[tok(GDN_QK_DIM), tok(GDN_QK_DIM), tok(GDN_V_DIM), tok(V7X_LANES), tok(cat),
                  pl.BlockSpec((bt, 1, cat), lambda i: (i, 0, 0)), state],
        out_specs=[tok(GDN_V_DIM), state],
        out_shape=[jax.ShapeDtypeStruct((b, t, GDN_V_DIM), F32),
                   jax.ShapeDtypeStruct((b, GDN_HEADS, GDN_DK, GDN_DV), F32)],
        compiler_params=pltpu.CompilerParams(dimension_semantics=("parallel",),
                                             vmem_limit_bytes=_vmem_limit(40 << 20)),
        name="gdn_sample",
    )(q, k, v, gcb, grep, grow_flat, s0)


def _swa_prompt_kernel(q_ref, kc_ref, kp_ref, vc_ref, vp_ref, sgb_ref, sink_ref, o_ref):
    w = WINDOW
    hd = SWA_HEAD_DIM
    nkeys = 2 * w
    ncols = 2 * w
    nwin = q_ref.shape[1] // w
    has_prev = pl.program_id(1) > 0
    kj = lax.broadcasted_iota(jnp.int32, (nkeys, ncols), 0)
    qi = lax.broadcasted_iota(jnp.int32, (nkeys, ncols), 1) % w
    mask_inner = ((kj >= w) & ((kj - w) <= qi)) | ((kj < w) & (kj > qi))
    masks = [mask_inner & ((kj >= w) | has_prev)] + [mask_inner] * (nwin - 1)
    lo_keys = lax.broadcasted_iota(jnp.int32, (nkeys, V7X_LANES), 1) < hd
    lo_q = lax.broadcasted_iota(jnp.int32, (w, V7X_LANES), 1) < hd
    scale = hd ** -0.5 * LOG2_E
    problems = [(t, h, gp) for t in range(nwin) for h in range(SWA_KV_HEADS)
                for gp in range(SWA_GROUP // 2)]

    kk, vbd = {}, {}
    for hp in range(SWA_KV_HEADS // 2):
        sl = slice(hp * V7X_LANES, (hp + 1) * V7X_LANES)
        k_all = jnp.concatenate([kp_ref[0, :, sl], kc_ref[0, :, sl]], axis=0)
        v_all = jnp.concatenate([vp_ref[0, :, sl], vc_ref[0, :, sl]], axis=0)
        kr_all = pltpu.roll(k_all, hd, 1)
        vr_all = pltpu.roll(v_all, hd, 1)
        for t in range(nwin):
            keys = slice(t * w, (t + 2) * w)
            kt, kr, vt, vr = k_all[keys], kr_all[keys], v_all[keys], vr_all[keys]
            for par, h in enumerate((2 * hp, 2 * hp + 1)):
                kk[t, h] = (jnp.where(lo_keys, kt, kr) if par == 0
                            else jnp.where(lo_keys, kr, kt)).astype(BF16)
                vv = jnp.where(lo_keys, vt, vr) if par == 0 else jnp.where(lo_keys, vr, vt)
                vbd[t, h] = jnp.concatenate([jnp.where(lo_keys, vv, 0.0), jnp.where(lo_keys, 0.0, vv)],
                                            axis=0).astype(BF16)
    s, pn = {}, {}

    def scores(t, h, gp):
        lo = (h * SWA_GROUP + 2 * gp) * hd
        qp = q_ref[0, t * w:(t + 1) * w, lo:lo + V7X_LANES]
        rq = jnp.concatenate([jnp.where(lo_q, qp, 0.0), jnp.where(lo_q, 0.0, qp)], axis=0).astype(BF16)
        sc = lax.dot_general(kk[t, h], rq, (((1,), (1,)), ((), ())), preferred_element_type=F32)
        s[t, h, gp] = jnp.where(masks[t], sc * scale, -jnp.inf)

    def softmax(t, h, gp):
        sink = sink_ref[h * (SWA_GROUP // 2) + gp] * LOG2_E
        m = jnp.maximum(jnp.max(s[t, h, gp], axis=0, keepdims=True), sink)
        p = jnp.exp2(s[t, h, gp] - m)
        denom = jnp.sum(p, axis=0, keepdims=True) + jnp.exp2(sink - m)
        p = (p * (1.0 / denom)).astype(BF16)
        pn[t, h, gp] = jnp.concatenate([p[:, :w], p[:, w:]], axis=0)

    def values(t, h, gp):
        lo = (h * SWA_GROUP + 2 * gp) * hd
        rows = slice(t * w, (t + 1) * w)
        o = lax.dot_general(pn[t, h, gp], vbd[t, h], (((0,), (0,)), ((), ())), preferred_element_type=F32)
        o_ref[0, rows, lo:lo + V7X_LANES] = o * sgb_ref[0, rows, lo:lo + V7X_LANES]

    for stage in (scores, softmax, values):
        for prob in problems:
            stage(*prob)


def _swa_prompt(q, k, v, sgb, sink_cols):
    b, l, _ = q.shape
    nwin = _token_tile(l // WINDOW, SWA_WINDOWS_PER_STEP)
    nb = l // (nwin * WINDOW)
    cur = lambda wd: pl.BlockSpec((1, nwin * WINDOW, wd), lambda i, j: (i, j, 0))
    prev = lambda wd: pl.BlockSpec((1, WINDOW, wd), lambda i, j: (i, jnp.maximum(j * nwin - 1, 0), 0))
    return pl.pallas_call(
        _swa_prompt_kernel,
        grid=(b, nb),
        in_specs=[cur(SWA_Q_DIM), cur(SWA_KV_DIM), prev(SWA_KV_DIM), cur(SWA_KV_DIM), prev(SWA_KV_DIM),
                  cur(SWA_Q_DIM), pl.BlockSpec(sink_cols.shape, lambda i, j: (0, 0, 0))],
        out_specs=cur(SWA_Q_DIM),
        out_shape=jax.ShapeDtypeStruct((b, l, SWA_Q_DIM), F32),
        compiler_params=pltpu.CompilerParams(dimension_semantics=("parallel", "parallel")),
        name="swa_prompt",
    )(q, k, k, v, v, sgb, sink_cols)


def _swa_sample_kernel(q_ref, kn_ref, vn_ref, knt_ref, vnt_ref, kb_ref, vb_ref, sgb_ref, sink_ref,
                       o_ref, ko_ref, vo_ref, *, steps):
    t = steps
    bt = q_ref.shape[0]
    nrows = kb_ref.shape[2]
    rows = SWA_GROUP * t
    qt_b = lax.broadcasted_iota(jnp.int32, (rows, nrows), 0) % t
    ki_b = lax.broadcasted_iota(jnp.int32, (rows, nrows), 1)
    mask_buf = ((nrows - ki_b + qt_b) < WINDOW)[None]
    qt_n = lax.broadcasted_iota(jnp.int32, (rows, t), 0) % t
    ki_n = lax.broadcasted_iota(jnp.int32, (rows, t), 1)
    mask_new = (ki_n <= qt_n)[None]
    scale = SWA_HEAD_DIM ** -0.5
    bdot = lambda a, b_, dims: lax.dot_general(a.astype(BF16), b_.astype(BF16), dims,
                                               preferred_element_type=F32)
    nt_dims = (((2,), (2,)), ((0,), (0,)))
    nn_dims = (((2,), (1,)), ((0,), (0,)))
    s_b, s_n, p_b, p_n, denom = {}, {}, {}, {}, {}

    def scores(h):
        klo = h * SWA_HEAD_DIM
        q_h = jnp.concatenate(
            [q_ref[:, :, (h * SWA_GROUP + g) * SWA_HEAD_DIM:(h * SWA_GROUP + g + 1) * SWA_HEAD_DIM]
             for g in range(SWA_GROUP)], axis=1)
        k_b = kb_ref[:, klo:klo + SWA_HEAD_DIM, :]
        k_n = kn_ref[:, :, klo:klo + SWA_HEAD_DIM]
        s_b[h] = jnp.where(mask_buf, bdot(q_h, k_b, nn_dims) * scale, -jnp.inf)
        s_n[h] = jnp.where(mask_new, bdot(q_h, k_n, nt_dims) * scale, -jnp.inf)

    def softmax(h):
        sink = sink_ref[h][None]
        m = jnp.maximum(jnp.maximum(jnp.max(s_b[h], axis=-1, keepdims=True),
                                    jnp.max(s_n[h], axis=-1, keepdims=True)), sink)
        p_b[h] = jnp.exp(s_b[h] - m)
        p_n[h] = jnp.exp(s_n[h] - m)
        denom[h] = (jnp.sum(p_b[h], axis=-1, keepdims=True) + jnp.sum(p_n[h], axis=-1, keepdims=True)
                    + jnp.exp(sink - m))

    def values(h):
        klo = h * SWA_HEAD_DIM
        v_b = vb_ref[:, klo:klo + SWA_HEAD_DIM, :]
        v_n = vn_ref[:, :, klo:klo + SWA_HEAD_DIM]
        o = (bdot(p_b[h], v_b, nt_dims) + bdot(p_n[h], v_n, nn_dims)) / denom[h]
        for g in range(SWA_GROUP):
            lo = (h * SWA_GROUP + g) * SWA_HEAD_DIM
            o_ref[:, :, lo:lo + SWA_HEAD_DIM] = (o[:, g * t:(g + 1) * t]
                                                 * sgb_ref[:, :, lo:lo + SWA_HEAD_DIM])

    for stage in (scores, softmax, values):
        for h in range(SWA_KV_HEADS):
            stage(h)
    is_new = lax.broadcasted_iota(jnp.int32, (SWA_KV_DIM, nrows), 1) >= nrows - t
    for new_ref, buf_ref, out_ref in ((knt_ref, kb_ref, ko_ref), (vnt_ref, vb_ref, vo_ref)):
        fresh = new_ref[...]
        for i in range(bt):
            kept = pltpu.roll(buf_ref[i], nrows - t, 1)
            out_ref[i] = jnp.where(is_new, pltpu.roll(fresh, (nrows - t - i * t) % nrows, 1), kept)


def _swa_sample(q, k, v, k_t, v_t, k_buf, v_buf, sgb, sink_rows, bt):
    b, t, _ = q.shape
    nrows = k_buf.shape[2]
    assert bt * t == nrows == V7X_LANES, "one lane tile of new columns per step"
    new = lambda wd: pl.BlockSpec((bt, t, wd), lambda i: (i, 0, 0))
    new_t = pl.BlockSpec((SWA_KV_DIM, bt * t), lambda i: (0, i))
    buf = pl.BlockSpec((bt, SWA_KV_DIM, nrows), lambda i: (i, 0, 0))
    kern = functools.partial(_swa_sample_kernel, steps=t)
    return pl.pallas_call(
        kern,
        grid=(b // bt,),
        in_specs=[new(SWA_Q_DIM), new(SWA_KV_DIM), new(SWA_KV_DIM), new_t, new_t, buf, buf, new(SWA_Q_DIM),
                  pl.BlockSpec(sink_rows.shape, lambda i: (0, 0, 0))],
        out_specs=[new(SWA_Q_DIM), buf, buf],
        out_shape=[jax.ShapeDtypeStruct((b, t, SWA_Q_DIM), F32),
                   jax.ShapeDtypeStruct((b, SWA_KV_DIM, nrows), F32),
                   jax.ShapeDtypeStruct((b, SWA_KV_DIM, nrows), F32)],
        compiler_params=pltpu.CompilerParams(dimension_semantics=("parallel",)),
        name="swa_sample",
    )(q, k, v, k_t, v_t, k_buf, v_buf, sgb, sink_rows)


def _out_kernel(x_ref, og_ref, zga_ref, osg_ref, gn_ref, wo_ref, nw2_ref, win_ref, wout_ref, nf_ref, y_ref):
    gn = gn_ref[...]
    heads = []
    for h in range(GDN_HEADS):
        lo = h * GDN_DV
        heads.append(_rms(og_ref[:, lo:lo + GDN_DV], gn))
    mixed = jnp.concatenate(heads, axis=1) * zga_ref[...] + osg_ref[...]
    x2 = x_ref[...] + jnp.dot(mixed.astype(BF16), wo_ref[...], preferred_element_type=F32)
    x3 = _swiglu_half_step(x2, nw2_ref[...], win_ref, wout_ref)
    y_ref[...] = _rms(x3, nf_ref[...])


def _out(x1, og, zga, osg, gnorm, wo, nw2, win, wout, nf, tm):
    t = x1.shape[0]
    row = pl.BlockSpec((tm, D_MODEL), lambda i: (i, 0))
    wbytes = (wo.size + win.size + wout.size) * 2
    return pl.pallas_call(
        _out_kernel,
        grid=(t // tm,),
        in_specs=[row] * 4 + [_const_spec(gnorm.shape), _const_spec(wo.shape), _const_spec(nw2.shape),
                              _const_spec(win.shape), _const_spec(wout.shape), _const_spec(nf.shape)],
        out_specs=row,
        out_shape=jax.ShapeDtypeStruct((t, D_MODEL), F32),
        compiler_params=pltpu.CompilerParams(
            dimension_semantics=("parallel",),
            vmem_limit_bytes=_vmem_limit(wbytes + 18 * tm * D_MODEL * 4 + (8 << 20))),
        name="out",
    )(x1, og, zga, osg, gnorm, wo, nw2, win, wout, nf)


def _rope_tables(pos):
    inv = ROPE_THETA ** (-jnp.arange(0, ROPE_DIM, 2, dtype=F32) / ROPE_DIM)
    ang = pos.astype(F32)[:, None] * inv[None, :]
    cos, sin = jnp.cos(ang), jnp.sin(ang)
    n = pos.shape[0]
    pad = SWA_HEAD_DIM - ROPE_DIM
    one_head = lambda first, second, fill: jnp.concatenate(
        [first, second, jnp.full((n, pad), fill, F32)], axis=1)
    zeros = jnp.zeros_like(sin)
    cos_f = one_head(cos, cos, 1.0)
    sin_a = one_head(-sin, zeros, 0.0)
    sin_b = one_head(zeros, sin, 0.0)
    reps = V7X_LANES // SWA_HEAD_DIM
    return tuple(jnp.tile(t, (1, reps)) for t in (cos_f, sin_a, sin_b))


def _pad_lanes(v, width):
    return jnp.pad(v.astype(F32), (0, width - v.shape[0]))[None, :]


def _branch(x, pos, conv_buf, s0, k_buf, v_buf, p):
    b, l, _ = x.shape
    t = b * l
    is_prompt = conv_buf is None
    chunk = min(GDN_CHUNK, l)
    n = l // chunk
    tm = _token_tile(l if is_prompt else t, TOKEN_TILE)
    tm_mix = _token_tile(l if is_prompt else t, MIX_TILE)
    tables = _rope_tables(pos)
    if is_prompt:
        halo = None
        table_tiles = l // tm_mix
    else:
        halo = jnp.pad(conv_buf, ((0, 0), (l - (CONV_W - 1), 0), (0, 0))).reshape(t, CONV_DIM)
        heads_t = lambda tb: jnp.tile(tb.T, (SWA_KV_DIM // V7X_LANES, tm_mix // l))
        tables_t = tuple(heads_t(tb) for tb in tables)
        tables = tuple(jnp.tile(tb, (tm_mix // l, 1)) for tb in tables)
        table_tiles = 1
    tm_conv = tm if is_prompt else _token_tile(t, TOKEN_TILE // 2)
    x1, qg, kg, vg, tail = _ffn_conv(x.reshape(t, D_MODEL), p['norm_ffn1'], p['w_ffn1_in'], p['w_ffn1_out'],
                                     p['norm_mix'], p['w_qkv'], p['conv_w'], halo,
                                     tm=tm_conv, seq_len=l, group=l)
    mixed = _mix(x1, p['norm_mix'], p['w_mix'], p['a_log_pad'], p['dt_bias_pad'], tables,
                 None if is_prompt else tables_t, tm=tm_mix, table_tiles=table_tiles, chunk=chunk)
    gcb, zga, qr, kr, vs, sgb = mixed[:6]
    seq = lambda a: a.reshape(b, l, a.shape[-1])
    qg, kg, vg, gcb, qr, kr, vs, sgb = (seq(a) for a in (qg, kg, vg, gcb, qr, kr, vs, sgb))
    new_conv = tail.reshape(b, -1, CONV_DIM)[:, -(CONV_W - 1):]

    gc = gcb[:, :, :GDN_HEADS]
    grow = gc.reshape(b, n, chunk, GDN_HEADS).transpose(0, 1, 3, 2)
    if is_prompt:
        assert chunk == PK_C and s0 is None
        og, new_s = _pk_gdn(qg, kg, vg, gcb, grow.reshape(b, n, 1, PK_CAT))
        sink_cols = jnp.repeat(p['sinks'].reshape(SWA_Q_HEADS // 2, 2), WINDOW, axis=1)
        osg = _swa_prompt(qr, kr, vs, sgb, sink_cols.reshape(SWA_Q_HEADS // 2, 1, 2 * WINDOW))
        new_k, new_v = kr[:, l - WINDOW:], vs[:, l - WINDOW:]
    else:
        assert n == 1
        og, new_s = _gdn_sample(qg, kg, vg, gcb, jnp.repeat(gc, l, axis=-1),
                                grow.reshape(b, 1, GDN_HEADS * l), s0, _token_tile(b, 8))
        sink_rows = jnp.repeat(p['sinks'], l, axis=1).reshape(SWA_KV_HEADS, SWA_GROUP * l, 1)
        nrows = k_buf.shape[1]
        to_minor = lambda c: c.transpose(0, 2, 3, 1).reshape(b, SWA_KV_DIM, nrows)
        from_minor = lambda c: c.reshape(b, SWA_KV_HEADS, SWA_HEAD_DIM, nrows).transpose(0, 3, 1, 2)
        osg, new_k, new_v = _swa_sample(qr, kr, vs, mixed[6], mixed[7], to_minor(k_buf), to_minor(v_buf),
                                        sgb, sink_rows, nrows // l)
        new_k, new_v = from_minor(new_k), from_minor(new_v)
    y = _out(x1, og.reshape(t, GDN_V_DIM), zga, osg.reshape(t, SWA_Q_DIM), p['gdn_norm'], p['w_out'],
             p['norm_ffn2'], p['w_ffn2_in'], p['w_ffn2_out'], p['norm_final'], tm)
    kv_shape = (b, new_k.shape[1], SWA_KV_HEADS, SWA_HEAD_DIM)
    return (y.reshape(b, l, D_MODEL), new_conv, new_s, new_k.reshape(kv_shape), new_v.reshape(kv_shape))


def _layer_params(l, norm_ffn1, w_ffn1_in, w_ffn1_out, norm_mix, w_in, conv_w, gdn_a_log, gdn_dt_bias,
                  gdn_norm, swa_sinks, w_out, norm_ffn2, w_ffn2_in, w_ffn2_out, norm_final):
    bounds = [0]
    for s in IN_SPLITS:
        bounds.append(bounds[-1] + s)
    w_in_t = jnp.swapaxes(w_in[l], 0, 1)
    col = lambda i: w_in_t[bounds[i]:bounds[i + 1]].astype(BF16)
    w_ab = jnp.pad(jnp.concatenate([col(2), col(3)], axis=0), ((0, V7X_LANES - 2 * GDN_HEADS), (0, 0)))
    row = lambda v: v.astype(F32)[None, :]
    return {
        'norm_ffn1': row(norm_ffn1[l]), 'w_ffn1_in': w_ffn1_in[l].astype(BF16),
        'w_ffn1_out': w_ffn1_out[l].astype(BF16),
        'norm_mix': row(norm_mix[l]),
        'w_qkv': col(0),
        'w_mix': [w_ab, col(1), col(7), col(4), col(5), col(6), col(8)],
        'conv_w': conv_w[l].astype(F32),
        'a_log_pad': _pad_lanes(gdn_a_log[l], V7X_LANES), 'dt_bias_pad': _pad_lanes(gdn_dt_bias[l], V7X_LANES),
        'gdn_norm': row(gdn_norm[l]),
        'sinks': swa_sinks[l].astype(F32).reshape(SWA_KV_HEADS, SWA_GROUP),
        'w_out': w_out[l].astype(BF16),
        'norm_ffn2': row(norm_ffn2[l]), 'w_ffn2_in': w_ffn2_in[l].astype(BF16),
        'w_ffn2_out': w_ffn2_out[l].astype(BF16),
        'norm_final': row(norm_final),
    }


def kernel(x_prompt, x_sample, state_conv, state_gdn, cache_swa_k, cache_swa_v, norm_ffn1, w_ffn1_in, w_ffn1_out, norm_mix, w_in, conv_w, gdn_a_log, gdn_dt_bias, gdn_norm, swa_sinks, w_out, norm_ffn2, w_ffn2_in, w_ffn2_out, norm_final):
    depth = w_in.shape[0]
    assert depth == 1, "the final norm is fused into the layer's last kernel"
    l_p = x_prompt.shape[1]
    l_s = x_sample.shape[1]
    pos_p = jnp.arange(l_p, dtype=jnp.int32)
    pos_s = PAST_LEN + jnp.arange(l_s, dtype=jnp.int32)
    p = _layer_params(0, norm_ffn1, w_ffn1_in, w_ffn1_out, norm_mix, w_in, conv_w, gdn_a_log, gdn_dt_bias,
                      gdn_norm, swa_sinks, w_out, norm_ffn2, w_ffn2_in, w_ffn2_out, norm_final)
    yp, c1, s1, k1, v1 = _branch(x_prompt, pos_p, None, None, None, None, p)
    ys, c2, s2, k2, v2 = _branch(x_sample, pos_s, state_conv[0], state_gdn[0], cache_swa_k[0],
                                 cache_swa_v[0], p)
    return (yp, ys, c1[None], c2[None], s1[None], s2[None], k1[None], k2[None], v1[None], v2[None])
```

```python
import functools

import jax
import jax.numpy as jnp
from jax import lax
from jax.experimental import pallas as pl
from jax.experimental.pallas import tpu as pltpu

D_MODEL = 1024
PAST_LEN = 16384
GDN_HEADS = 8
GDN_DK = 128
GDN_DV = 128
GDN_QK_DIM = GDN_HEADS * GDN_DK
GDN_V_DIM = GDN_HEADS * GDN_DV
CONV_W = 4
GDN_CHUNK = 64
CONV_DIM = 2 * GDN_QK_DIM + GDN_V_DIM
SWA_Q_HEADS = 16
SWA_KV_HEADS = 4
SWA_GROUP = SWA_Q_HEADS // SWA_KV_HEADS
SWA_HEAD_DIM = 64
SWA_Q_DIM = SWA_Q_HEADS * SWA_HEAD_DIM
SWA_KV_DIM = SWA_KV_HEADS * SWA_HEAD_DIM
WINDOW = 128
ROPE_DIM = SWA_HEAD_DIM // 4
ROPE_HALF = ROPE_DIM // 2
ROPE_THETA = 500000.0
D_FF = 2816
EPS = 1e-6
LOG2_E = 1.4426950408889634
IN_SPLITS = (CONV_DIM, GDN_V_DIM, GDN_HEADS, GDN_HEADS, SWA_Q_DIM, SWA_KV_DIM, SWA_KV_DIM, D_MODEL, D_MODEL)

V7X_LANES = 128
V7X_SUBLANES = 8
V7X_MXU_DIM = 256
V7X_VMEM_BYTES = 64 * 1024 * 1024

TOKEN_TILE = 2 * V7X_MXU_DIM
MIX_TILE = 2 * V7X_MXU_DIM
FFN_CHUNK = V7X_MXU_DIM
FFN_OUT_TILE = V7X_MXU_DIM
CONV_PIECE = V7X_MXU_DIM
SOLVE_BASE = V7X_SUBLANES
SWA_WINDOWS_PER_STEP = 4

F32 = jnp.float32
BF16 = jnp.bfloat16


def _dot(a, b):
    return jnp.dot(a.astype(BF16), b.astype(BF16), preferred_element_type=F32)


def _dot_nt(a, b):
    return lax.dot_general(a.astype(BF16), b.astype(BF16), (((1,), (1,)), ((), ())),
                           preferred_element_type=F32)


def _dot_tn(a, b):
    return lax.dot_general(a.astype(BF16), b.astype(BF16), (((0,), (0,)), ((), ())),
                           preferred_element_type=F32)


def _proj(xn, w_t):
    return lax.dot_general(xn, w_t, (((1,), (1,)), ((), ())), preferred_element_type=F32)


def _rms(x, w):
    return x * lax.rsqrt(jnp.mean(x * x, axis=-1, keepdims=True) + EPS) * w


def _sigmoid(x):
    return 0.5 * jnp.tanh(0.5 * x) + 0.5


def _silu(x):
    h = 0.5 * x
    return h * jnp.tanh(h) + h


def _vmem_limit(nbytes):
    return int(min(V7X_VMEM_BYTES - (4 << 20), max(nbytes, 16 << 20)))


def _const_spec(shape):
    nd = len(shape)
    return pl.BlockSpec(shape, lambda *_: (0,) * nd, pipeline_mode=pl.Buffered(1))


def _token_tile(t, want):
    tm = min(want, t)
    while t % tm:
        tm //= 2
    return tm


FFN_NCHUNK = D_FF // FFN_CHUNK


def _swiglu_act(xn, win_ref, c):
    lo = c * FFN_CHUNK
    g = jnp.dot(xn, win_ref[:, lo:lo + FFN_CHUNK], preferred_element_type=F32)
    u = jnp.dot(xn, win_ref[:, D_FF + lo:D_FF + lo + FFN_CHUNK], preferred_element_type=F32)
    return (_silu(g) * u).astype(BF16)


def _swiglu_half_step(x, norm_w, win_ref, wout_ref):
    xn = _rms(x, norm_w).astype(BF16)
    act = jnp.concatenate([_swiglu_act(xn, win_ref, c) for c in range(FFN_NCHUNK)], axis=1)
    return x + 0.5 * jnp.dot(act, wout_ref[...], preferred_element_type=F32)


def _ffn_conv_kernel(*refs, tiles_per_seq, group):
    carried = tiles_per_seq > 0
    if carried:
        (x_ref, nw1_ref, win_ref, wout_ref, nw_ref, wqkv_ref, cw_ref,
         x1_ref, qg_ref, kg_ref, vg_ref, tail_ref, xn_ref, carry_ref) = refs
    else:
        (x_ref, nw1_ref, win_ref, wout_ref, nw_ref, wqkv_ref, cw_ref, halo_ref,
         x1_ref, qg_ref, kg_ref, vg_ref, tail_ref, xn_ref) = refs
    rows = x_ref.shape[0]
    step = pl.program_id(0)
    n_tiles = pl.num_programs(0) - 1
    if carried:
        seq_start = ((step - 1) % tiles_per_seq) == 0
    else:
        pos = lax.broadcasted_iota(jnp.int32, (rows, 1), 0) % group

    def conv_piece(c):
        cols = slice(c * CONV_PIECE, (c + 1) * CONV_PIECE)
        raw = _proj(xn_ref[...], wqkv_ref[cols, :])
        cw = cw_ref[:, cols]
        acc = raw * cw[CONV_W - 1:CONV_W, :]
        if carried:
            halo = jnp.where(seq_start, 0.0, carry_ref[:, cols])
            ext = jnp.concatenate([halo, raw], axis=0)
            for s in range(1, CONV_W):
                acc = acc + pltpu.roll(ext, s, 0)[V7X_SUBLANES:] * cw[CONV_W - 1 - s:CONV_W - s, :]
            carry_ref[:, cols] = raw[rows - V7X_SUBLANES:]
            tail_ref[0, :, cols] = raw[rows - V7X_SUBLANES:]
        else:
            halo = halo_ref[:, cols]
            for s in range(1, CONV_W):
                shifted = jnp.where(pos >= s, pltpu.roll(raw, s, 0),
                                    pltpu.roll(halo, (rows + s - group) % rows, 0))
                acc = acc + shifted * cw[CONV_W - 1 - s:CONV_W - s, :]
            tail_ref[:, cols] = raw
        conv = _silu(acc)
        part, lo = divmod(c * CONV_PIECE, GDN_QK_DIM)
        out_ref = (qg_ref, kg_ref, vg_ref)[part]
        if part == 2:
            out_ref[:, lo:lo + CONV_PIECE] = conv.astype(BF16)
        else:
            for h in range(CONV_PIECE // GDN_DK):
                ch = conv[:, h * GDN_DK:(h + 1) * GDN_DK]
                out_ref[:, lo + h * GDN_DK:lo + (h + 1) * GDN_DK] = (
                    ch * lax.rsqrt(jnp.sum(ch * ch, axis=-1, keepdims=True) + EPS)).astype(BF16)

    def body(do_ffn, do_conv):
        n_pieces = CONV_DIM // CONV_PIECE
        n_out = D_MODEL // FFN_OUT_TILE
        slots = FFN_NCHUNK + n_out
        due = [((k + 1) * n_pieces) // slots for k in range(slots)] if do_ffn else [n_pieces]
        done = 0
        if do_ffn:
            x = x_ref[...]
            xn = _rms(x, nw1_ref[...]).astype(BF16)
            acts, outs = [], []
        for k, upto in enumerate(due):
            if do_ffn and k < FFN_NCHUNK:
                acts.append(_swiglu_act(xn, win_ref, k))
            elif do_ffn:
                if k == FFN_NCHUNK:
                    act = jnp.concatenate(acts, axis=1)
                lo = (k - FFN_NCHUNK) * FFN_OUT_TILE
                outs.append(jnp.dot(act, wout_ref[:, lo:lo + FFN_OUT_TILE], preferred_element_type=F32))
            if do_conv:
                for c in range(done, upto):
                    conv_piece(c)
                done = upto
        if do_ffn:
            x1 = x + 0.5 * jnp.concatenate(outs, axis=1)
            x1_ref[...] = x1
            xn_ref[...] = _rms(x1, nw_ref[...]).astype(BF16)

    @pl.when(step == 0)
    def _():
        if carried:
            carry_ref[...] = jnp.zeros_like(carry_ref)
        body(True, False)

    @pl.when((step > 0) & (step < n_tiles))
    def _():
        body(True, True)

    @pl.when(step == n_tiles)
    def _():
        body(False, True)


def _ffn_conv(x, norm1, win, wout, norm_w, wqkv, cw, halo, *, tm, seq_len, group):
    t = x.shape[0]
    carried = halo is None
    tiles_per_seq = seq_len // tm if carried else 0
    n_tiles = t // tm
    cur = lambda n: pl.BlockSpec((tm, n), lambda i: (jnp.minimum(i, n_tiles - 1), 0))
    prev = lambda n: pl.BlockSpec((tm, n), lambda i: (jnp.maximum(i - 1, 0), 0))
    weights = [win, wout, wqkv]
    in_specs = [cur(D_MODEL), _const_spec((1, D_MODEL)), _const_spec(win.shape), _const_spec(wout.shape),
                _const_spec((1, D_MODEL)), _const_spec(wqkv.shape), _const_spec(cw.shape)]
    args = [x, norm1, win, wout, norm_w, wqkv, cw]
    out_specs = [cur(D_MODEL), prev(GDN_QK_DIM), prev(GDN_QK_DIM), prev(GDN_V_DIM)]
    out_shape = [jax.ShapeDtypeStruct((t, D_MODEL), F32)] + [
        jax.ShapeDtypeStruct((t, n), BF16) for n in (GDN_QK_DIM, GDN_QK_DIM, GDN_V_DIM)]
    scratch = [pltpu.VMEM((tm, D_MODEL), BF16)]
    if carried:
        nseq = t // seq_len
        out_specs.append(pl.BlockSpec((1, V7X_SUBLANES, CONV_DIM),
                                      lambda i: (jnp.maximum(i - 1, 0) // tiles_per_seq, 0, 0)))
        out_shape.append(jax.ShapeDtypeStruct((nseq, V7X_SUBLANES, CONV_DIM), F32))
        scratch.append(pltpu.VMEM((V7X_SUBLANES, CONV_DIM), F32))
    else:
        in_specs.append(prev(CONV_DIM))
        args.append(halo)
        out_specs.append(prev(CONV_DIM))
        out_shape.append(jax.ShapeDtypeStruct((t, CONV_DIM), F32))
    wbytes = sum(w.size for w in weights) * 2
    kern = functools.partial(_ffn_conv_kernel, tiles_per_seq=tiles_per_seq, group=group)
    return pl.pallas_call(
        kern,
        grid=(n_tiles + 1,),
        in_specs=in_specs,
        out_specs=out_specs,
        out_shape=out_shape,
        scratch_shapes=scratch,
        compiler_params=pltpu.CompilerParams(
            dimension_semantics=("arbitrary",),
            vmem_limit_bytes=_vmem_limit(wbytes + 32 * tm * D_MODEL * 4 + (4 << 20))),
        name="ffn_conv",
    )(*args)


def _rope(x, cos_f, sin_a, sin_b):
    width = x.shape[1]
    reps = width // V7X_LANES
    tile = lambda t: jnp.concatenate([t] * reps, axis=1) if reps > 1 else t
    return (x * tile(cos_f) + pltpu.roll(x, width - ROPE_HALF, 1) * tile(sin_a)
            + pltpu.roll(x, ROPE_HALF, 1) * tile(sin_b))


def _mix_kernel(*refs, chunk, kv_t, n_cast):
    n_in = 14 + (3 if kv_t else 0) + n_cast
    ins, outs = refs[:n_in], refs[n_in:]
    (x_ref, nw_ref, wab_ref, wz_ref, wga_ref, wqs_ref, wks_ref, wvs_ref, wgb_ref,
     alog_ref, dtb_ref, cos_ref, sina_ref, sinb_ref) = ins[:14]
    gcb_ref, zga_ref, qr_ref, kr_ref, vs_ref, sgb_ref = outs[:6]
    if kv_t:
        cost_ref, sinat_ref, sinbt_ref = ins[14:17]
        krt_ref, vst_ref = outs[6:8]
    for src_ref, dst_ref in zip(ins[n_in - n_cast:], outs[len(outs) - n_cast:] if n_cast else ()):
        dst_ref[...] = src_ref[...].astype(BF16)
    rows = x_ref.shape[0]
    xn = _rms(x_ref[...], nw_ref[...]).astype(BF16)

    ab = _proj(xn, wab_ref[...])
    sp_in = ab + dtb_ref[...]
    softplus = jnp.maximum(sp_in, 0.0) + jnp.log1p(jnp.exp(-jnp.abs(sp_in)))
    gc = -jnp.exp(alog_ref[...]) * softplus
    cpos = lax.broadcasted_iota(jnp.int32, (rows, 1), 0) % chunk
    s = 1
    while s < chunk:
        gc = gc + jnp.where(cpos >= s, pltpu.roll(gc, s, 0), 0.0)
        s *= 2
    lane = lax.broadcasted_iota(jnp.int32, (1, V7X_LANES), 1)
    gcb_ref[...] = jnp.where(lane < GDN_HEADS, gc, jax.nn.sigmoid(ab))

    zga_ref[...] = _silu(_proj(xn, wz_ref[...])) * _sigmoid(_proj(xn, wga_ref[...]))
    cos_f, sin_a, sin_b = cos_ref[...], sina_ref[...], sinb_ref[...]
    qr_ref[...] = _rope(_proj(xn, wqs_ref[...]), cos_f, sin_a, sin_b)
    kr_ref[...] = _rope(_proj(xn, wks_ref[...]), cos_f, sin_a, sin_b)
    vs_ref[...] = _proj(xn, wvs_ref[...])
    sgb_ref[...] = _sigmoid(_proj(xn, wgb_ref[...]))
    if kv_t:
        ks_t = lax.dot_general(wks_ref[...], xn, (((1,), (1,)), ((), ())), preferred_element_type=F32)
        krt_ref[...] = (ks_t * cost_ref[...] + pltpu.roll(ks_t, SWA_KV_DIM - ROPE_HALF, 0) * sinat_ref[...]
                        + pltpu.roll(ks_t, ROPE_HALF, 0) * sinbt_ref[...])
        vst_ref[...] = lax.dot_general(wvs_ref[...], xn, (((1,), (1,)), ((), ())),
                                       preferred_element_type=F32)


def _mix(x, norm_w, weights, alog, dtb, tables, tables_t, to_cast=(), *, tm, table_tiles, chunk):
    t = x.shape[0]
    kv_t = tables_t is not None
    row = lambda n: pl.BlockSpec((tm, n), lambda i: (i, 0))
    tab = pl.BlockSpec((tm, V7X_LANES), lambda i: (i % table_tiles, 0))
    widths = (V7X_LANES, GDN_V_DIM, SWA_Q_DIM, SWA_KV_DIM, SWA_KV_DIM, D_MODEL)
    in_specs = ([row(D_MODEL), _const_spec((1, D_MODEL))] + [_const_spec(w.shape) for w in weights]
                + [_const_spec(alog.shape), _const_spec(dtb.shape), tab, tab, tab])
    args = [x, norm_w, *weights, alog, dtb, *tables]
    out_specs = [row(n) for n in widths]
    out_shape = [jax.ShapeDtypeStruct((t, n), F32) for n in widths]
    if kv_t:
        in_specs += [_const_spec((SWA_KV_DIM, tm))] * 3
        args += list(tables_t)
        out_specs += [pl.BlockSpec((SWA_KV_DIM, tm), lambda i: (0, i))] * 2
        out_shape += [jax.ShapeDtypeStruct((SWA_KV_DIM, t), F32)] * 2
    steps = t // tm
    for w in to_cast:
        parts = steps
        while w.shape[0] % parts or (w.shape[0] // parts) % (2 * V7X_SUBLANES):
            parts //= 2
        blk = pl.BlockSpec((w.shape[0] // parts, w.shape[1]),
                           functools.partial(lambda i, parts: (i * parts // steps, 0), parts=parts))
        in_specs.append(blk)
        args.append(w)
        out_specs.append(blk)
        out_shape.append(jax.ShapeDtypeStruct(w.shape, BF16))
    wbytes = sum(w.size for w in weights) * 2
    return pl.pallas_call(
        functools.partial(_mix_kernel, chunk=chunk, kv_t=kv_t, n_cast=len(to_cast)),
        grid=(t // tm,),
        in_specs=in_specs,
        out_specs=out_specs,
        out_shape=out_shape,
        compiler_params=pltpu.CompilerParams(
            dimension_semantics=("arbitrary",),
            vmem_limit_bytes=_vmem_limit(wbytes + 36 * tm * D_MODEL * 4 + (8 << 20))),
        name="mix",
    )(*args)


PK_C = GDN_CHUNK
PK_CAT = GDN_HEADS * PK_C
PK_PAIR = 2 * GDN_DK
PK_NPAIR = GDN_HEADS // 2
PK_GROUP = V7X_MXU_DIM // PK_C
PK_CHUNKS_PER_STEP = 4


def _pk_masks():
    c = PK_C
    i = lax.broadcasted_iota(jnp.int32, (c, PK_CAT), 0)
    j = lax.broadcasted_iota(jnp.int32, (c, PK_CAT), 1) % c
    lo = (lax.broadcasted_iota(jnp.int32, (c, V7X_LANES), 1) < c).astype(BF16)
    return i, j, (lo, 1 - lo)


def _pk_prod(x, y, bd_mask):
    lo, hi = bd_mask
    zero = jnp.zeros((PK_C, V7X_LANES), BF16)
    outs = []
    for g in range(GDN_HEADS // PK_GROUP):
        sl = slice(g * V7X_MXU_DIM, (g + 1) * V7X_MXU_DIM)
        yb = y[:, sl].astype(BF16)
        t0, t1 = yb[:, :V7X_LANES], yb[:, V7X_LANES:]
        bd = jnp.concatenate([jnp.concatenate([t0 * lo, zero], axis=1),
                              jnp.concatenate([t0 * hi, zero], axis=1),
                              jnp.concatenate([zero, t1 * lo], axis=1),
                              jnp.concatenate([zero, t1 * hi], axis=1)], axis=0)
        outs.append(jnp.dot(x[:, sl].astype(BF16), bd, preferred_element_type=F32))
    return jnp.concatenate(outs, axis=1)


def _pk_solve(a_list, i, j, bd_mask):
    prod_all = lambda xs, ys: [_pk_prod(x, y, bd_mask) for x, y in zip(xs, ys)]
    eye = (i == j).astype(F32)
    b = SOLVE_BASE
    same = (i // b) == (j // b)
    p = [jnp.where(same, a, 0.0) for a in a_list]
    d = [eye - x for x in p]
    span = 2
    while span < b:
        p = prod_all(p, p)
        d = [x + y for x, y in zip(d, prod_all(d, p))]
        span *= 2
    while b < PK_C:
        same2 = (i // (2 * b)) == (j // (2 * b))
        sel = same2 & jnp.logical_not(same)
        off = [jnp.where(sel, a, 0.0) for a in a_list]
        d = [x - y for x, y in zip(d, prod_all(d, prod_all(off, d)))]
        same = same2
        b *= 2
    return d


def _pk_pre_body(q_ref, k_ref, v_ref, gcb_ref, grow_ref, u_ref, w_ref, qg_ref, kg_ref, qk_ref, gl_ref):
    c = PK_C
    i, j, bd_mask = _pk_masks()
    lane_lo = lax.broadcasted_iota(jnp.int32, (c, V7X_LANES), 1) < c
    zeros_k = jnp.zeros((c, GDN_DK), BF16)
    chunks = [(b, ch) for b in range(q_ref.shape[0]) for ch in range(q_ref.shape[1] // c)]
    a_list, vb_list, kbg_list = [], [], []
    for b, ch in chunks:
        rows = slice(ch * c, (ch + 1) * c)
        gcb = gcb_ref[b, rows, :]
        g_end = gcb[c - 1:c, :]
        eg_s = jnp.exp(gcb)
        ek_s = jnp.exp(g_end - gcb)
        col = lambda arr, n: jnp.broadcast_to(arr[:, n:n + 1], (c, V7X_LANES))
        full = lambda arr, base: jnp.concatenate([col(arr, base + h) for h in range(GDN_HEADS)], axis=1)
        gc_cols = [col(gcb, h) for h in range(GDN_HEADS)]
        beta_f = full(gcb, GDN_HEADS)
        eg_f = full(eg_s, 0)
        ek_f = full(ek_s, 0)
        gc_cat = jnp.concatenate([jnp.where(lane_lo, gc_cols[2 * m], gc_cols[2 * m + 1])
                                  for m in range(PK_NPAIR)], axis=1)
        decay = jnp.exp(jnp.where(i >= j, gc_cat - grow_ref[b, ch], -jnp.inf))
        q = q_ref[b, rows, :].astype(F32) * (GDN_DK ** -0.5)
        k = k_ref[b, rows, :].astype(F32)
        kb = k * beta_f
        vb_list.append((v_ref[b, rows, :].astype(F32) * beta_f).astype(BF16))
        kbg_list.append((kb * eg_f).astype(BF16))
        qg_ref[b, rows, :] = (q * eg_f).astype(BF16)
        kg_ref[b, rows, :] = (k * ek_f).astype(BF16)
        gl_ref[b, ch] = eg_f[c - 1:c, :]
        kk = []
        for m in range(PK_NPAIR):
            sl = slice(m * PK_PAIR, (m + 1) * PK_PAIR)
            lhs = jnp.concatenate([kb[:, sl], q[:, sl]], axis=0).astype(BF16)
            k2 = k[:, sl].astype(BF16)
            rhs_nt = jnp.concatenate(
                [jnp.concatenate([k2[:, :GDN_DK], zeros_k], axis=1),
                 jnp.concatenate([zeros_k, k2[:, GDN_DK:]], axis=1)], axis=0)
            kk.append(lax.dot_general(lhs, rhs_nt, (((1,), (1,)), ((), ())), preferred_element_type=F32))
        a_raw = jnp.concatenate([x[:c] for x in kk], axis=1)
        qk_raw = jnp.concatenate([x[c:] for x in kk], axis=1)
        a_list.append(jnp.where(i > j, a_raw * decay, 0.0))
        qk_ref[b, rows, :] = (qk_raw * decay).astype(BF16)
    t_list = _pk_solve(a_list, i, j, bd_mask)
    for n, (b, ch) in enumerate(chunks):
        rows = slice(ch * c, (ch + 1) * c)
        t = t_list[n]
        for m in range(PK_NPAIR):
            tp = t[:, m * V7X_LANES:(m + 1) * V7X_LANES]
            lhs = jnp.concatenate([jnp.where(lane_lo, tp, 0.0), jnp.where(lane_lo, 0.0, tp)],
                                  axis=0).astype(BF16)
            h0, h1 = 2 * m, 2 * m + 1
            hs = lambda h: slice(h * GDN_DK, (h + 1) * GDN_DK)
            rhs = jnp.concatenate(
                [jnp.concatenate([vb_list[n][:, hs(h0)], kbg_list[n][:, hs(h0)]], axis=1),
                 jnp.concatenate([vb_list[n][:, hs(h1)], kbg_list[n][:, hs(h1)]], axis=1)], axis=0)
            uw = jnp.dot(lhs, rhs, preferred_element_type=F32)
            u_ref[b, rows, hs(h0)] = uw[:c, :GDN_DV]
            w_ref[b, rows, hs(h0)] = uw[:c, GDN_DV:].astype(BF16)
            u_ref[b, rows, hs(h1)] = uw[c:, :GDN_DV]
            w_ref[b, rows, hs(h1)] = uw[c:, GDN_DV:].astype(BF16)


def _pk_kernel(q_ref, k_ref, v_ref, gcb_ref, grow_ref, o_ref, sout_ref,
               u_ref, w_ref, qg_ref, kg_ref, qk_ref, gl_ref, s_ref):
    c = PK_C
    nb = u_ref.shape[0]
    step = pl.program_id(0)

    @pl.when(step == 0)
    def _():
        s_ref[...] = jnp.zeros_like(s_ref)

    _pk_pre_body(q_ref, k_ref, v_ref, gcb_ref, grow_ref, u_ref, w_ref, qg_ref, kg_ref, qk_ref, gl_ref)

    zeros_s = jnp.zeros((GDN_DK, GDN_DV), BF16)
    zeros_v = jnp.zeros((c, GDN_DV), BF16)
    chains = [(b, m) for b in range(nb) for m in range(PK_NPAIR)]
    pair = lambda m: slice(m * PK_PAIR, (m + 1) * PK_PAIR)
    state = {bm: s_ref[bm] for bm in chains}

    for ch in range(u_ref.shape[1] // c):
        rows = slice(ch * c, (ch + 1) * c)
        wq, bd_v = {}, {}

        def state_products(b, m):
            sb = state[b, m].astype(BF16)
            bd_s = jnp.concatenate([jnp.concatenate([sb[:, :GDN_DV], zeros_s], axis=1),
                                    jnp.concatenate([zeros_s, sb[:, GDN_DV:]], axis=1)], axis=0)
            lhs = jnp.concatenate([w_ref[b, rows, pair(m)], qg_ref[b, rows, pair(m)]], axis=0)
            wq[b, m] = jnp.dot(lhs, bd_s, preferred_element_type=F32)

        def new_values(b, m):
            v_new = (u_ref[b, rows, pair(m)] - wq[b, m][:c]).astype(BF16)
            bd_v[b, m] = jnp.concatenate([jnp.concatenate([v_new[:, :GDN_DV], zeros_v], axis=1),
                                          jnp.concatenate([zeros_v, v_new[:, GDN_DV:]], axis=1)], axis=0)

        def outputs_and_update(b, m):
            qk_p = qk_ref[b, rows, m * V7X_LANES:(m + 1) * V7X_LANES]
            o_ref[b, rows, pair(m)] = wq[b, m][c:] + jnp.dot(qk_p, bd_v[b, m], preferred_element_type=F32)
            kg_p = kg_ref[b, rows, pair(m)]
            kg_st = jnp.concatenate([kg_p[:, :GDN_DK], kg_p[:, GDN_DK:]], axis=0)
            upd = lax.dot_general(kg_st, bd_v[b, m], (((0,), (0,)), ((), ())),
                                  preferred_element_type=F32)
            state[b, m] = state[b, m] * gl_ref[b, ch, :, pair(m)] + upd

        for stage in (state_products, new_values, outputs_and_update):
            for b, m in chains:
                stage(b, m)

    for bm in chains:
        s_ref[bm] = state[bm]

    @pl.when(step == pl.num_programs(0) - 1)
    def _():
        for b, m in chains:
            s_c = s_ref[b, m]
            sout_ref[b, 2 * m] = s_c[:, :GDN_DV]
            sout_ref[b, 2 * m + 1] = s_c[:, GDN_DV:]


def _pk_gdn(q, k, v, gcb, grow_flat):
    b, l, _ = q.shape
    n = l // PK_C
    per = _token_tile(n, PK_CHUNKS_PER_STEP)
    rows = per * PK_C
    tok = lambda wd: pl.BlockSpec((b, rows, wd), lambda s: (0, s, 0))
    return pl.pallas_call(
        _pk_kernel,
        grid=(n // per,),
        in_specs=[tok(GDN_QK_DIM), tok(GDN_QK_DIM), tok(GDN_V_DIM), tok(V7X_LANES),
                  pl.BlockSpec((b, per, 1, PK_CAT), lambda s: (0, s, 0, 0))],
        out_specs=[tok(GDN_V_DIM),
                   pl.BlockSpec((b, GDN_HEADS, GDN_DK, GDN_DV), lambda s: (0, 0, 0, 0))],
        out_shape=[jax.ShapeDtypeStruct((b, l, GDN_V_DIM), F32),
                   jax.ShapeDtypeStruct((b, GDN_HEADS, GDN_DK, GDN_DV), F32)],
        scratch_shapes=[pltpu.VMEM((b, rows, GDN_V_DIM), F32),
                        pltpu.VMEM((b, rows, GDN_QK_DIM), BF16),
                        pltpu.VMEM((b, rows, GDN_QK_DIM), BF16),
                        pltpu.VMEM((b, rows, GDN_QK_DIM), BF16),
                        pltpu.VMEM((b, rows, PK_CAT), BF16),
                        pltpu.VMEM((b, per, 1, GDN_V_DIM), F32),
                        pltpu.VMEM((b, PK_NPAIR, GDN_DK, 2 * GDN_DV), F32)],
        compiler_params=pltpu.CompilerParams(dimension_semantics=("arbitrary",),
                                             vmem_limit_bytes=_vmem_limit(V7X_VMEM_BYTES)),
        name="gdn_prompt",
    )(q, k, v, gcb, grow_flat)


def _head_block_rows(x, t):
    width = x.shape[1] // GDN_HEADS
    z = jnp.zeros((t, width), x.dtype)
    rows = []
    for h in range(GDN_HEADS):
        rows.append(jnp.concatenate([z] * h + [x[:, h * width:(h + 1) * width]]
                                    + [z] * (GDN_HEADS - 1 - h), axis=1))
    return jnp.concatenate(rows, axis=0)


def _gdn_sample_kernel(q_ref, k_ref, v_ref, gcb_ref, grep_ref, grow_ref, s0_ref, o_ref, sout_ref):
    bt, t, _ = q_ref.shape
    cat = GDN_HEADS * t
    i = lax.broadcasted_iota(jnp.int32, (t, cat), 0)
    j = lax.broadcasted_iota(jnp.int32, (t, cat), 1) % t
    r = lax.broadcasted_iota(jnp.int32, (cat, cat), 0) // t
    l = lax.broadcasted_iota(jnp.int32, (cat, cat), 1) // t
    same_head = r == l
    eye = (i == j).astype(F32)
    seqs = range(bt)
    pair = lambda m: slice(m * PK_PAIR, (m + 1) * PK_PAIR)

    vb, kbg, qg, kg, gl, a, qk = {}, {}, {}, {}, {}, {}, {}
    for b in seqs:
        gcb = gcb_ref[b]
        g_end = gcb[t - 1:t, :]
        eg_s = jnp.exp(gcb)
        ek_s = jnp.exp(g_end - gcb)
        col = lambda arr, n: jnp.broadcast_to(arr[:, n:n + 1], (t, V7X_LANES))
        full = lambda arr, base: jnp.concatenate([col(arr, base + h) for h in range(GDN_HEADS)], axis=1)
        beta_f, eg_f, ek_f = full(gcb, GDN_HEADS), full(eg_s, 0), full(ek_s, 0)
        decay = jnp.exp(jnp.where(i >= j, grep_ref[b] - grow_ref[b], -jnp.inf))
        q = q_ref[b].astype(F32) * (GDN_DK ** -0.5)
        k = k_ref[b].astype(F32)
        kb = k * beta_f
        vb[b] = v_ref[b].astype(F32) * beta_f
        kbg[b] = kb * eg_f
        qg[b] = q * eg_f
        kg[b] = k * ek_f
        gl[b] = eg_f[t - 1:t, :]
        kk = _dot_nt(jnp.concatenate([kb, q], axis=0), _head_block_rows(k, t))
        a[b] = jnp.where(i > j, kk[:t] * decay, 0.0)
        qk[b] = kk[t:] * decay

    def prod_all(xs, ys):
        return {b: _dot(xs[b], jnp.where(same_head, jnp.concatenate([ys[b]] * GDN_HEADS, axis=0), 0.0))
                for b in seqs}

    d = {b: eye - a[b] for b in seqs}
    p = a
    span = 2
    while span < t:
        p = prod_all(p, p)
        dp = prod_all(d, p)
        d = {b: d[b] + dp[b] for b in seqs}
        span *= 2

    u, w = {}, {}
    for b in seqs:
        rhs = jnp.concatenate([_head_block_rows(vb[b], t), _head_block_rows(kbg[b], t)], axis=1)
        uw = _dot(d[b], rhs)
        u[b], w[b] = uw[:, :GDN_V_DIM], uw[:, GDN_V_DIM:]

    zeros_s = jnp.zeros((GDN_DK, GDN_DV), BF16)
    wq = {}
    for b in seqs:
        for m in range(PK_NPAIR):
            s0 = s0_ref[b, 2 * m].astype(BF16)
            s1 = s0_ref[b, 2 * m + 1].astype(BF16)
            bd_s = jnp.concatenate([jnp.concatenate([s0, zeros_s], axis=1),
                                    jnp.concatenate([zeros_s, s1], axis=1)], axis=0)
            lhs = jnp.concatenate([w[b][:, pair(m)], qg[b][:, pair(m)]], axis=0).astype(BF16)
            wq[b, m] = jnp.dot(lhs, bd_s, preferred_element_type=F32)
    v_new = {b: u[b] - jnp.concatenate([wq[b, m][:t] for m in range(PK_NPAIR)], axis=1) for b in seqs}
    zeros_v = jnp.zeros((t, GDN_DV), F32)
    for b in seqs:
        o_ref[b] = (jnp.concatenate([wq[b, m][t:] for m in range(PK_NPAIR)], axis=1)
                    + _dot(qk[b], _head_block_rows(v_new[b], t)))
        for m in range(PK_NPAIR):
            vp = v_new[b][:, pair(m)]
            bd_v = jnp.concatenate([jnp.concatenate([vp[:, :GDN_DV], zeros_v], axis=1),
                                    jnp.concatenate([zeros_v, vp[:, GDN_DV:]], axis=1)], axis=0)
            kp = kg[b][:, pair(m)]
            kg_st = jnp.concatenate([kp[:, :GDN_DK], kp[:, GDN_DK:]], axis=0)
            upd = _dot_tn(kg_st, bd_v)
            glp = gl[b][:, pair(m)]
            sout_ref[b, 2 * m] = s0_ref[b, 2 * m] * glp[:, :GDN_DV] + upd[:, :GDN_DV]
            sout_ref[b, 2 * m + 1] = s0_ref[b, 2 * m + 1] * glp[:, GDN_DV:] + upd[:, GDN_DV:]


def _gdn_sample(q, k, v, gcb, grep, grow_flat, s0, bt):
    b, t, _ = q.shape
    assert t == V7X_SUBLANES, "one sublane tile of steps per sequence"
    cat = GDN_HEADS * t
    tok = lambda wd: pl.BlockSpec((bt, t, wd), lambda i: (i, 0, 0))
    state = pl.BlockSpec((bt, GDN_HEADS, GDN_DK, GDN_DV), lambda i: (i, 0, 0, 0))
    return pl.pallas_call(
        _gdn_sample_kernel,
        grid=(b // bt,),
        in_specs=[tok(GDN_QK_DIM), tok(GDN_QK_DIM), tok(GDN_V_DIM), tok(V7X_LANES), tok(cat),
                  pl.BlockSpec((bt, 1, cat), lambda i: (i, 0, 0)), state],
        out_specs=[tok(GDN_V_DIM), state],
        out_shape=[jax.ShapeDtypeStruct((b, t, GDN_V_DIM), F32),
                   jax.ShapeDtypeStruct((b, GDN_HEADS, GDN_DK, GDN_DV), F32)],
        compiler_params=pltpu.CompilerParams(dimension_semantics=("parallel",),
                                             vmem_limit_bytes=_vmem_limit(40 << 20)),
        name="gdn_sample",
    )(q, k, v, gcb, grep, grow_flat, s0)


def _swa_prompt_kernel(q_ref, kc_ref, kp_ref, vc_ref, vp_ref, sgb_ref, sink_ref, o_ref):
    w = WINDOW
    hd = SWA_HEAD_DIM
    nkeys = 2 * w
    ncols = 2 * w
    nwin = q_ref.shape[1] // w
    has_prev = pl.program_id(1) > 0
    kj = lax.broadcasted_iota(jnp.int32, (nkeys, ncols), 0)
    qi = lax.broadcasted_iota(jnp.int32, (nkeys, ncols), 1) % w
    mask_inner = ((kj >= w) & ((kj - w) <= qi)) | ((kj < w) & (kj > qi))
    masks = [mask_inner & ((kj >= w) | has_prev)] + [mask_inner] * (nwin - 1)
    lo_keys = lax.broadcasted_iota(jnp.int32, (nkeys, V7X_LANES), 1) < hd
    lo_q = lax.broadcasted_iota(jnp.int32, (w, V7X_LANES), 1) < hd
    scale = hd ** -0.5 * LOG2_E
    problems = [(t, h, gp) for t in range(nwin) for h in range(SWA_KV_HEADS)
                for gp in range(SWA_GROUP // 2)]

    kk, vbd = {}, {}
    for hp in range(SWA_KV_HEADS // 2):
        sl = slice(hp * V7X_LANES, (hp + 1) * V7X_LANES)
        k_all = jnp.concatenate([kp_ref[0, :, sl], kc_ref[0, :, sl]], axis=0)
        v_all = jnp.concatenate([vp_ref[0, :, sl], vc_ref[0, :, sl]], axis=0)
        kr_all = pltpu.roll(k_all, hd, 1)
        vr_all = pltpu.roll(v_all, hd, 1)
        for t in range(nwin):
            keys = slice(t * w, (t + 2) * w)
            kt, kr, vt, vr = k_all[keys], kr_all[keys], v_all[keys], vr_all[keys]
            for par, h in enumerate((2 * hp, 2 * hp + 1)):
                kk[t, h] = (jnp.where(lo_keys, kt, kr) if par == 0
                            else jnp.where(lo_keys, kr, kt)).astype(BF16)
                vv = jnp.where(lo_keys, vt, vr) if par == 0 else jnp.where(lo_keys, vr, vt)
                vbd[t, h] = jnp.concatenate([jnp.where(lo_keys, vv, 0.0), jnp.where(lo_keys, 0.0, vv)],
                                            axis=0).astype(BF16)
    s, pn = {}, {}

    def scores(t, h, gp):
        lo = (h * SWA_GROUP + 2 * gp) * hd
        qp = q_ref[0, t * w:(t + 1) * w, lo:lo + V7X_LANES]
        rq = jnp.concatenate([jnp.where(lo_q, qp, 0.0), jnp.where(lo_q, 0.0, qp)], axis=0).astype(BF16)
        sc = lax.dot_general(kk[t, h], rq, (((1,), (1,)), ((), ())), preferred_element_type=F32)
        s[t, h, gp] = jnp.where(masks[t], sc * scale, -jnp.inf)

    def softmax(t, h, gp):
        sink = sink_ref[h * (SWA_GROUP // 2) + gp] * LOG2_E
        m = jnp.maximum(jnp.max(s[t, h, gp], axis=0, keepdims=True), sink)
        p = jnp.exp2(s[t, h, gp] - m)
        denom = jnp.sum(p, axis=0, keepdims=True) + jnp.exp2(sink - m)
        p = (p * (1.0 / denom)).astype(BF16)
        pn[t, h, gp] = jnp.concatenate([p[:, :w], p[:, w:]], axis=0)

    def values(t, h, gp):
        lo = (h * SWA_GROUP + 2 * gp) * hd
        rows = slice(t * w, (t + 1) * w)
        o = lax.dot_general(pn[t, h, gp], vbd[t, h], (((0,), (0,)), ((), ())), preferred_element_type=F32)
        o_ref[0, rows, lo:lo + V7X_LANES] = o * sgb_ref[0, rows, lo:lo + V7X_LANES]

    for stage in (scores, softmax, values):
        for prob in problems:
            stage(*prob)


def _swa_prompt(q, k, v, sgb, sink_cols):
    b, l, _ = q.shape
    nwin = _token_tile(l // WINDOW, SWA_WINDOWS_PER_STEP)
    nb = l // (nwin * WINDOW)
    cur = lambda wd: pl.BlockSpec((1, nwin * WINDOW, wd), lambda i, j: (i, j, 0))
    prev = lambda wd: pl.BlockSpec((1, WINDOW, wd), lambda i, j: (i, jnp.maximum(j * nwin - 1, 0), 0))
    return pl.pallas_call(
        _swa_prompt_kernel,
        grid=(b, nb),
        in_specs=[cur(SWA_Q_DIM), cur(SWA_KV_DIM), prev(SWA_KV_DIM), cur(SWA_KV_DIM), prev(SWA_KV_DIM),
                  cur(SWA_Q_DIM), pl.BlockSpec(sink_cols.shape, lambda i, j: (0, 0, 0))],
        out_specs=cur(SWA_Q_DIM),
        out_shape=jax.ShapeDtypeStruct((b, l, SWA_Q_DIM), F32),
        compiler_params=pltpu.CompilerParams(dimension_semantics=("parallel", "parallel")),
        name="swa_prompt",
    )(q, k, k, v, v, sgb, sink_cols)


def _swa_sample_kernel(q_ref, kn_ref, vn_ref, knt_ref, vnt_ref, kb_ref, vb_ref, sgb_ref, sink_ref,
                       o_ref, ko_ref, vo_ref, *, steps):
    t = steps
    bt = q_ref.shape[0]
    nrows = kb_ref.shape[2]
    rows = SWA_GROUP * t
    qt_b = lax.broadcasted_iota(jnp.int32, (rows, nrows), 0) % t
    ki_b = lax.broadcasted_iota(jnp.int32, (rows, nrows), 1)
    mask_buf = ((nrows - ki_b + qt_b) < WINDOW)[None]
    qt_n = lax.broadcasted_iota(jnp.int32, (rows, t), 0) % t
    ki_n = lax.broadcasted_iota(jnp.int32, (rows, t), 1)
    mask_new = (ki_n <= qt_n)[None]
    scale = SWA_HEAD_DIM ** -0.5
    bdot = lambda a, b_, dims: lax.dot_general(a.astype(BF16), b_.astype(BF16), dims,
                                               preferred_element_type=F32)
    nt_dims = (((2,), (2,)), ((0,), (0,)))
    nn_dims = (((2,), (1,)), ((0,), (0,)))
    s_b, s_n, p_b, p_n, denom = {}, {}, {}, {}, {}

    def scores(h):
        klo = h * SWA_HEAD_DIM
        q_h = jnp.concatenate(
            [q_ref[:, :, (h * SWA_GROUP + g) * SWA_HEAD_DIM:(h * SWA_GROUP + g + 1) * SWA_HEAD_DIM]
             for g in range(SWA_GROUP)], axis=1)
        k_b = kb_ref[:, klo:klo + SWA_HEAD_DIM, :]
        k_n = kn_ref[:, :, klo:klo + SWA_HEAD_DIM]
        s_b[h] = jnp.where(mask_buf, bdot(q_h, k_b, nn_dims) * scale, -jnp.inf)
        s_n[h] = jnp.where(mask_new, bdot(q_h, k_n, nt_dims) * scale, -jnp.inf)

    def softmax(h):
        sink = sink_ref[h][None]
        m = jnp.maximum(jnp.maximum(jnp.max(s_b[h], axis=-1, keepdims=True),
                                    jnp.max(s_n[h], axis=-1, keepdims=True)), sink)
        p_b[h] = jnp.exp(s_b[h] - m)
        p_n[h] = jnp.exp(s_n[h] - m)
        denom[h] = (jnp.sum(p_b[h], axis=-1, keepdims=True) + jnp.sum(p_n[h], axis=-1, keepdims=True)
                    + jnp.exp(sink - m))

    def values(h):
        klo = h * SWA_HEAD_DIM
        v_b = vb_ref[:, klo:klo + SWA_HEAD_DIM, :]
        v_n = vn_ref[:, :, klo:klo + SWA_HEAD_DIM]
        o = (bdot(p_b[h], v_b, nt_dims) + bdot(p_n[h], v_n, nn_dims)) / denom[h]
        for g in range(SWA_GROUP):
            lo = (h * SWA_GROUP + g) * SWA_HEAD_DIM
            o_ref[:, :, lo:lo + SWA_HEAD_DIM] = (o[:, g * t:(g + 1) * t]
                                                 * sgb_ref[:, :, lo:lo + SWA_HEAD_DIM])

    for stage in (scores, softmax, values):
        for h in range(SWA_KV_HEADS):
            stage(h)
    is_new = lax.broadcasted_iota(jnp.int32, (SWA_KV_DIM, nrows), 1) >= nrows - t
    for new_ref, buf_ref, out_ref in ((knt_ref, kb_ref, ko_ref), (vnt_ref, vb_ref, vo_ref)):
        fresh = new_ref[...]
        for i in range(bt):
            kept = pltpu.roll(buf_ref[i], nrows - t, 1)
            out_ref[i] = jnp.where(is_new, pltpu.roll(fresh, (nrows - t - i * t) % nrows, 1), kept)


def _swa_sample(q, k, v, k_t, v_t, k_buf, v_buf, sgb, sink_rows, bt):
    b, t, _ = q.shape
    nrows = k_buf.shape[2]
    assert bt * t == nrows == V7X_LANES, "one lane tile of new columns per step"
    new = lambda wd: pl.BlockSpec((bt, t, wd), lambda i: (i, 0, 0))
    new_t = pl.BlockSpec((SWA_KV_DIM, bt * t), lambda i: (0, i))
    buf = pl.BlockSpec((bt, SWA_KV_DIM, nrows), lambda i: (i, 0, 0))
    kern = functools.partial(_swa_sample_kernel, steps=t)
    return pl.pallas_call(
        kern,
        grid=(b // bt,),
        in_specs=[new(SWA_Q_DIM), new(SWA_KV_DIM), new(SWA_KV_DIM), new_t, new_t, buf, buf, new(SWA_Q_DIM),
                  pl.BlockSpec(sink_rows.shape, lambda i: (0, 0, 0))],
        out_specs=[new(SWA_Q_DIM), buf, buf],
        out_shape=[jax.ShapeDtypeStruct((b, t, SWA_Q_DIM), F32),
                   jax.ShapeDtypeStruct((b, SWA_KV_DIM, nrows), F32),
                   jax.ShapeDtypeStruct((b, SWA_KV_DIM, nrows), F32)],
        compiler_params=pltpu.CompilerParams(dimension_semantics=("parallel",)),
        name="swa_sample",
    )(q, k, v, k_t, v_t, k_buf, v_buf, sgb, sink_rows)


def _out_kernel(x_ref, og_ref, zga_ref, osg_ref, gn_ref, wo_ref, nw2_ref, win_ref, wout_ref, nf_ref, y_ref):
    gn = gn_ref[...]
    heads = []
    for h in range(GDN_HEADS):
        lo = h * GDN_DV
        heads.append(_rms(og_ref[:, lo:lo + GDN_DV], gn))
    mixed = jnp.concatenate(heads, axis=1) * zga_ref[...] + osg_ref[...]
    x2 = x_ref[...] + jnp.dot(mixed.astype(BF16), wo_ref[...], preferred_element_type=F32)
    x3 = _swiglu_half_step(x2, nw2_ref[...], win_ref, wout_ref)
    y_ref[...] = _rms(x3, nf_ref[...])


def _out(x1, og, zga, osg, gnorm, wo, nw2, win, wout, nf, tm):
    t = x1.shape[0]
    row = pl.BlockSpec((tm, D_MODEL), lambda i: (i, 0))
    wbytes = (wo.size + win.size + wout.size) * 2
    return pl.pallas_call(
        _out_kernel,
        grid=(t // tm,),
        in_specs=[row] * 4 + [_const_spec(gnorm.shape), _const_spec(wo.shape), _const_spec(nw2.shape),
                              _const_spec(win.shape), _const_spec(wout.shape), _const_spec(nf.shape)],
        out_specs=row,
        out_shape=jax.ShapeDtypeStruct((t, D_MODEL), F32),
        compiler_params=pltpu.CompilerParams(
            dimension_semantics=("parallel",),
            vmem_limit_bytes=_vmem_limit(wbytes + 18 * tm * D_MODEL * 4 + (8 << 20))),
        name="out",
    )(x1, og, zga, osg, gnorm, wo, nw2, win, wout, nf)


def _rope_tables(pos):
    inv = ROPE_THETA ** (-jnp.arange(0, ROPE_DIM, 2, dtype=F32) / ROPE_DIM)
    ang = pos.astype(F32)[:, None] * inv[None, :]
    cos, sin = jnp.cos(ang), jnp.sin(ang)
    n = pos.shape[0]
    pad = SWA_HEAD_DIM - ROPE_DIM
    one_head = lambda first, second, fill: jnp.concatenate(
        [first, second, jnp.full((n, pad), fill, F32)], axis=1)
    zeros = jnp.zeros_like(sin)
    cos_f = one_head(cos, cos, 1.0)
    sin_a = one_head(-sin, zeros, 0.0)
    sin_b = one_head(zeros, sin, 0.0)
    reps = V7X_LANES // SWA_HEAD_DIM
    return tuple(jnp.tile(t, (1, reps)) for t in (cos_f, sin_a, sin_b))


def _pad_lanes(v, width):
    return jnp.pad(v.astype(F32), (0, width - v.shape[0]))[None, :]


def _branch(x, pos, conv_buf, s0, k_buf, v_buf, p, late_weights=None):
    b, l, _ = x.shape
    t = b * l
    is_prompt = conv_buf is None
    chunk = min(GDN_CHUNK, l)
    n = l // chunk
    tm = _token_tile(l if is_prompt else t, TOKEN_TILE)
    tm_mix = _token_tile(l if is_prompt else t, MIX_TILE)
    tables = _rope_tables(pos)
    if is_prompt:
        halo = None
        table_tiles = l // tm_mix
    else:
        halo = jnp.pad(conv_buf, ((0, 0), (l - (CONV_W - 1), 0), (0, 0))).reshape(t, CONV_DIM)
        heads_t = lambda tb: jnp.tile(tb.T, (SWA_KV_DIM // V7X_LANES, tm_mix // l))
        tables_t = tuple(heads_t(tb) for tb in tables)
        tables = tuple(jnp.tile(tb, (tm_mix // l, 1)) for tb in tables)
        table_tiles = 1
    tm_conv = tm if is_prompt else _token_tile(t, TOKEN_TILE // 2)
    x1, qg, kg, vg, tail = _ffn_conv(x.reshape(t, D_MODEL), p['norm_ffn1'], p['w_ffn1_in'], p['w_ffn1_out'],
                                     p['norm_mix'], p['w_qkv'], p['conv_w'], halo,
                                     tm=tm_conv, seq_len=l, group=l)
    to_cast = p['late_f32'] if late_weights is None else ()
    mixed = _mix(x1, p['norm_mix'], p['w_mix'], p['a_log_pad'], p['dt_bias_pad'], tables,
                 None if is_prompt else tables_t, to_cast,
                 tm=tm_mix, table_tiles=table_tiles, chunk=chunk)
    if late_weights is None:
        late_weights = tuple(mixed[len(mixed) - len(to_cast):])
    w_out_b, w_ffn2_in_b, w_ffn2_out_b = late_weights
    gcb, zga, qr, kr, vs, sgb = mixed[:6]
    seq = lambda a: a.reshape(b, l, a.shape[-1])
    qg, kg, vg, gcb, qr, kr, vs, sgb = (seq(a) for a in (qg, kg, vg, gcb, qr, kr, vs, sgb))
    new_conv = tail.reshape(b, -1, CONV_DIM)[:, -(CONV_W - 1):]

    gc = gcb[:, :, :GDN_HEADS]
    grow = gc.reshape(b, n, chunk, GDN_HEADS).transpose(0, 1, 3, 2)
    if is_prompt:
        assert chunk == PK_C and s0 is None
        og, new_s = _pk_gdn(qg, kg, vg, gcb, grow.reshape(b, n, 1, PK_CAT))
        sink_cols = jnp.repeat(p['sinks'].reshape(SWA_Q_HEADS // 2, 2), WINDOW, axis=1)
        osg = _swa_prompt(qr, kr, vs, sgb, sink_cols.reshape(SWA_Q_HEADS // 2, 1, 2 * WINDOW))
        new_k, new_v = kr[:, l - WINDOW:], vs[:, l - WINDOW:]
    else:
        assert n == 1
        og, new_s = _gdn_sample(qg, kg, vg, gcb, jnp.repeat(gc, l, axis=-1),
                                grow.reshape(b, 1, GDN_HEADS * l), s0, _token_tile(b, 8))
        sink_rows = jnp.repeat(p['sinks'], l, axis=1).reshape(SWA_KV_HEADS, SWA_GROUP * l, 1)
        nrows = k_buf.shape[1]
        to_minor = lambda c: c.transpose(0, 2, 3, 1).reshape(b, SWA_KV_DIM, nrows)
        from_minor = lambda c: c.reshape(b, SWA_KV_HEADS, SWA_HEAD_DIM, nrows).transpose(0, 3, 1, 2)
        osg, new_k, new_v = _swa_sample(qr, kr, vs, mixed[6], mixed[7], to_minor(k_buf), to_minor(v_buf),
                                        sgb, sink_rows, nrows // l)
        new_k, new_v = from_minor(new_k), from_minor(new_v)
    y = _out(x1, og.reshape(t, GDN_V_DIM), zga, osg.reshape(t, SWA_Q_DIM), p['gdn_norm'], w_out_b,
             p['norm_ffn2'], w_ffn2_in_b, w_ffn2_out_b, p['norm_final'], tm)
    kv_shape = (b, new_k.shape[1], SWA_KV_HEADS, SWA_HEAD_DIM)
    return (y.reshape(b, l, D_MODEL), new_conv, new_s, new_k.reshape(kv_shape), new_v.reshape(kv_shape),
            late_weights)


def _layer_params(l, norm_ffn1, w_ffn1_in, w_ffn1_out, norm_mix, w_in, conv_w, gdn_a_log, gdn_dt_bias,
                  gdn_norm, swa_sinks, w_out, norm_ffn2, w_ffn2_in, w_ffn2_out, norm_final):
    bounds = [0]
    for s in IN_SPLITS:
        bounds.append(bounds[-1] + s)
    w_in_t = jnp.swapaxes(w_in[l], 0, 1)
    col = lambda i: w_in_t[bounds[i]:bounds[i + 1]].astype(BF16)
    w_ab = jnp.pad(jnp.concatenate([col(2), col(3)], axis=0), ((0, V7X_LANES - 2 * GDN_HEADS), (0, 0)))
    row = lambda v: v.astype(F32)[None, :]
    return {
        'norm_ffn1': row(norm_ffn1[l]), 'w_ffn1_in': w_ffn1_in[l].astype(BF16),
        'w_ffn1_out': w_ffn1_out[l].astype(BF16),
        'norm_mix': row(norm_mix[l]),
        'w_qkv': col(0),
        'w_mix': [w_ab, col(1), col(7), col(4), col(5), col(6), col(8)],
        'conv_w': conv_w[l].astype(F32),
        'a_log_pad': _pad_lanes(gdn_a_log[l], V7X_LANES), 'dt_bias_pad': _pad_lanes(gdn_dt_bias[l], V7X_LANES),
        'gdn_norm': row(gdn_norm[l]),
        'sinks': swa_sinks[l].astype(F32).reshape(SWA_KV_HEADS, SWA_GROUP),
        'late_f32': (w_out[l], w_ffn2_in[l], w_ffn2_out[l]),
        'norm_ffn2': row(norm_ffn2[l]),
        'norm_final': row(norm_final),
    }


def kernel(x_prompt, x_sample, state_conv, state_gdn, cache_swa_k, cache_swa_v, norm_ffn1, w_ffn1_in, w_ffn1_out, norm_mix, w_in, conv_w, gdn_a_log, gdn_dt_bias, gdn_norm, swa_sinks, w_out, norm_ffn2, w_ffn2_in, w_ffn2_out, norm_final):
    depth = w_in.shape[0]
    assert depth == 1, "the final norm is fused into the layer's last kernel"
    l_p = x_prompt.shape[1]
    l_s = x_sample.shape[1]
    pos_p = jnp.arange(l_p, dtype=jnp.int32)
    pos_s = PAST_LEN + jnp.arange(l_s, dtype=jnp.int32)
    p = _layer_params(0, norm_ffn1, w_ffn1_in, w_ffn1_out, norm_mix, w_in, conv_w, gdn_a_log, gdn_dt_bias,
                      gdn_norm, swa_sinks, w_out, norm_ffn2, w_ffn2_in, w_ffn2_out, norm_final)
    yp, c1, s1, k1, v1, late = _branch(x_prompt, pos_p, None, None, None, None, p)
    ys, c2, s2, k2, v2, _ = _branch(x_sample, pos_s, state_conv[0], state_gdn[0], cache_swa_k[0],
                                    cache_swa_v[0], p, late)
    return (yp, ys, c1[None], c2[None], s1[None], s2[None], k1[None], k2[None], v1[None], v2[None])
```

```python
import functools

import jax
import jax.numpy as jnp
from jax import lax
from jax.experimental import pallas as pl
from jax.experimental.pallas import tpu as pltpu

D_MODEL = 1024
PAST_LEN = 16384
GDN_HEADS = 8
GDN_DK = 128
GDN_DV = 128
GDN_QK_DIM = GDN_HEADS * GDN_DK
GDN_V_DIM = GDN_HEADS * GDN_DV
CONV_W = 4
GDN_CHUNK = 64
CONV_DIM = 2 * GDN_QK_DIM + GDN_V_DIM
SWA_Q_HEADS = 16
SWA_KV_HEADS = 4
SWA_GROUP = SWA_Q_HEADS // SWA_KV_HEADS
SWA_HEAD_DIM = 64
SWA_Q_DIM = SWA_Q_HEADS * SWA_HEAD_DIM
SWA_KV_DIM = SWA_KV_HEADS * SWA_HEAD_DIM
WINDOW = 128
ROPE_DIM = SWA_HEAD_DIM // 4
ROPE_HALF = ROPE_DIM // 2
ROPE_THETA = 500000.0
D_FF = 2816
EPS = 1e-6
LOG2_E = 1.4426950408889634
IN_SPLITS = (CONV_DIM, GDN_V_DIM, GDN_HEADS, GDN_HEADS, SWA_Q_DIM, SWA_KV_DIM, SWA_KV_DIM, D_MODEL, D_MODEL)

V7X_LANES = 128
V7X_SUBLANES = 8
V7X_MXU_DIM = 256
V7X_VMEM_BYTES = 64 * 1024 * 1024

TOKEN_TILE = 2 * V7X_MXU_DIM
MIX_TILE = 2 * V7X_MXU_DIM
FFN_CHUNK = V7X_MXU_DIM
FFN_OUT_TILE = V7X_MXU_DIM
CONV_PIECE = V7X_MXU_DIM
SOLVE_BASE = V7X_SUBLANES
SWA_WINDOWS_PER_STEP = 8

F32 = jnp.float32
BF16 = jnp.bfloat16


def _dot(a, b):
    return jnp.dot(a.astype(BF16), b.astype(BF16), preferred_element_type=F32)


def _dot_nt(a, b):
    return lax.dot_general(a.astype(BF16), b.astype(BF16), (((1,), (1,)), ((), ())),
                           preferred_element_type=F32)


def _dot_tn(a, b):
    return lax.dot_general(a.astype(BF16), b.astype(BF16), (((0,), (0,)), ((), ())),
                           preferred_element_type=F32)


def _proj(xn, w_t):
    return lax.dot_general(xn, w_t, (((1,), (1,)), ((), ())), preferred_element_type=F32)


def _rms(x, w):
    return x * lax.rsqrt(jnp.mean(x * x, axis=-1, keepdims=True) + EPS) * w


def _sigmoid(x):
    return 0.5 * jnp.tanh(0.5 * x) + 0.5


def _silu(x):
    h = 0.5 * x
    return h * jnp.tanh(h) + h


def _vmem_limit(nbytes):
    return int(min(V7X_VMEM_BYTES - (4 << 20), max(nbytes, 16 << 20)))


def _const_spec(shape):
    nd = len(shape)
    return pl.BlockSpec(shape, lambda *_: (0,) * nd, pipeline_mode=pl.Buffered(1))


def _token_tile(t, want):
    tm = min(want, t)
    while t % tm:
        tm //= 2
    return tm


FFN_NCHUNK = D_FF // FFN_CHUNK


def _swiglu_act(xn, win_ref, c):
    lo = c * FFN_CHUNK
    g = jnp.dot(xn, win_ref[:, lo:lo + FFN_CHUNK], preferred_element_type=F32)
    u = jnp.dot(xn, win_ref[:, D_FF + lo:D_FF + lo + FFN_CHUNK], preferred_element_type=F32)
    return (_silu(g) * u).astype(BF16)


def _swiglu_half_step(x, norm_w, win_ref, wout_ref):
    xn = _rms(x, norm_w).astype(BF16)
    act = jnp.concatenate([_swiglu_act(xn, win_ref, c) for c in range(FFN_NCHUNK)], axis=1)
    return x + 0.5 * jnp.dot(act, wout_ref[...], preferred_element_type=F32)


def _ffn_conv_kernel(*refs, tiles_per_seq, group):
    carried = tiles_per_seq > 0
    if carried:
        (x_ref, nw1_ref, win_ref, wout_ref, nw_ref, wqkv_ref, cw_ref,
         x1_ref, qg_ref, kg_ref, vg_ref, tail_ref, xn_ref, carry_ref) = refs
    else:
        (x_ref, nw1_ref, win_ref, wout_ref, nw_ref, wqkv_ref, cw_ref, halo_ref,
         x1_ref, qg_ref, kg_ref, vg_ref, tail_ref, xn_ref) = refs
    rows = x_ref.shape[0]
    step = pl.program_id(0)
    n_tiles = pl.num_programs(0) - 1
    if carried:
        seq_start = ((step - 1) % tiles_per_seq) == 0
    else:
        pos = lax.broadcasted_iota(jnp.int32, (rows, 1), 0) % group

    def conv_piece(c):
        cols = slice(c * CONV_PIECE, (c + 1) * CONV_PIECE)
        raw = _proj(xn_ref[...], wqkv_ref[cols, :])
        cw = cw_ref[:, cols]
        acc = raw * cw[CONV_W - 1:CONV_W, :]
        if carried:
            halo = jnp.where(seq_start, 0.0, carry_ref[:, cols])
            ext = jnp.concatenate([halo, raw], axis=0)
            for s in range(1, CONV_W):
                acc = acc + pltpu.roll(ext, s, 0)[V7X_SUBLANES:] * cw[CONV_W - 1 - s:CONV_W - s, :]
            carry_ref[:, cols] = raw[rows - V7X_SUBLANES:]
            tail_ref[0, :, cols] = raw[rows - V7X_SUBLANES:]
        else:
            halo = halo_ref[:, cols]
            for s in range(1, CONV_W):
                shifted = jnp.where(pos >= s, pltpu.roll(raw, s, 0),
                                    pltpu.roll(halo, (rows + s - group) % rows, 0))
                acc = acc + shifted * cw[CONV_W - 1 - s:CONV_W - s, :]
            tail_ref[:, cols] = raw
        conv = _silu(acc)
        part, lo = divmod(c * CONV_PIECE, GDN_QK_DIM)
        out_ref = (qg_ref, kg_ref, vg_ref)[part]
        if part == 2:
            out_ref[:, lo:lo + CONV_PIECE] = conv.astype(BF16)
        else:
            for h in range(CONV_PIECE // GDN_DK):
                ch = conv[:, h * GDN_DK:(h + 1) * GDN_DK]
                out_ref[:, lo + h * GDN_DK:lo + (h + 1) * GDN_DK] = (
                    ch * lax.rsqrt(jnp.sum(ch * ch, axis=-1, keepdims=True) + EPS)).astype(BF16)

    def body(do_ffn, do_conv):
        n_pieces = CONV_DIM // CONV_PIECE
        n_out = D_MODEL // FFN_OUT_TILE
        slots = FFN_NCHUNK + n_out
        due = [((k + 1) * n_pieces) // slots for k in range(slots)] if do_ffn else [n_pieces]
        done = 0
        if do_ffn:
            x = x_ref[...]
            xn = _rms(x, nw1_ref[...]).astype(BF16)
            acts, outs = [], []
        for k, upto in enumerate(due):
            if do_ffn and k < FFN_NCHUNK:
                acts.append(_swiglu_act(xn, win_ref, k))
            elif do_ffn:
                if k == FFN_NCHUNK:
                    act = jnp.concatenate(acts, axis=1)
                lo = (k - FFN_NCHUNK) * FFN_OUT_TILE
                outs.append(jnp.dot(act, wout_ref[:, lo:lo + FFN_OUT_TILE], preferred_element_type=F32))
            if do_conv:
                for c in range(done, upto):
                    conv_piece(c)
                done = upto
        if do_ffn:
            x1 = x + 0.5 * jnp.concatenate(outs, axis=1)
            x1_ref[...] = x1
            xn_ref[...] = _rms(x1, nw_ref[...]).astype(BF16)

    @pl.when(step == 0)
    def _():
        if carried:
            carry_ref[...] = jnp.zeros_like(carry_ref)
        body(True, False)

    @pl.when((step > 0) & (step < n_tiles))
    def _():
        body(True, True)

    @pl.when(step == n_tiles)
    def _():
        body(False, True)


def _ffn_conv(x, norm1, win, wout, norm_w, wqkv, cw, halo, *, tm, seq_len, group):
    t = x.shape[0]
    carried = halo is None
    tiles_per_seq = seq_len // tm if carried else 0
    n_tiles = t // tm
    cur = lambda n: pl.BlockSpec((tm, n), lambda i: (jnp.minimum(i, n_tiles - 1), 0))
    prev = lambda n: pl.BlockSpec((tm, n), lambda i: (jnp.maximum(i - 1, 0), 0))
    weights = [win, wout, wqkv]
    in_specs = [cur(D_MODEL), _const_spec((1, D_MODEL)), _const_spec(win.shape), _const_spec(wout.shape),
                _const_spec((1, D_MODEL)), _const_spec(wqkv.shape), _const_spec(cw.shape)]
    args = [x, norm1, win, wout, norm_w, wqkv, cw]
    out_specs = [cur(D_MODEL), prev(GDN_QK_DIM), prev(GDN_QK_DIM), prev(GDN_V_DIM)]
    out_shape = [jax.ShapeDtypeStruct((t, D_MODEL), F32)] + [
        jax.ShapeDtypeStruct((t, n), BF16) for n in (GDN_QK_DIM, GDN_QK_DIM, GDN_V_DIM)]
    scratch = [pltpu.VMEM((tm, D_MODEL), BF16)]
    if carried:
        nseq = t // seq_len
        out_specs.append(pl.BlockSpec((1, V7X_SUBLANES, CONV_DIM),
                                      lambda i: (jnp.maximum(i - 1, 0) // tiles_per_seq, 0, 0)))
        out_shape.append(jax.ShapeDtypeStruct((nseq, V7X_SUBLANES, CONV_DIM), F32))
        scratch.append(pltpu.VMEM((V7X_SUBLANES, CONV_DIM), F32))
    else:
        in_specs.append(prev(CONV_DIM))
        args.append(halo)
        out_specs.append(prev(CONV_DIM))
        out_shape.append(jax.ShapeDtypeStruct((t, CONV_DIM), F32))
    wbytes = sum(w.size for w in weights) * 2
    kern = functools.partial(_ffn_conv_kernel, tiles_per_seq=tiles_per_seq, group=group)
    return pl.pallas_call(
        kern,
        grid=(n_tiles + 1,),
        in_specs=in_specs,
        out_specs=out_specs,
        out_shape=out_shape,
        scratch_shapes=scratch,
        compiler_params=pltpu.CompilerParams(
            dimension_semantics=("arbitrary",),
            vmem_limit_bytes=_vmem_limit(wbytes + 32 * tm * D_MODEL * 4 + (4 << 20))),
        name="ffn_conv",
    )(*args)


def _rope(x, cos_f, sin_a, sin_b):
    width = x.shape[1]
    reps = width // V7X_LANES
    tile = lambda t: jnp.concatenate([t] * reps, axis=1) if reps > 1 else t
    return (x * tile(cos_f) + pltpu.roll(x, width - ROPE_HALF, 1) * tile(sin_a)
            + pltpu.roll(x, ROPE_HALF, 1) * tile(sin_b))


def _mix_kernel(*refs, chunk, kv_t, n_cast):
    n_in = 14 + (3 if kv_t else 0) + n_cast
    ins, outs = refs[:n_in], refs[n_in:]
    (x_ref, nw_ref, wab_ref, wz_ref, wga_ref, wqs_ref, wks_ref, wvs_ref, wgb_ref,
     alog_ref, dtb_ref, cos_ref, sina_ref, sinb_ref) = ins[:14]
    gcb_ref, zga_ref, qr_ref, kr_ref, vs_ref, sgb_ref = outs[:6]
    if kv_t:
        cost_ref, sinat_ref, sinbt_ref = ins[14:17]
        krt_ref, vst_ref = outs[6:8]
    for src_ref, dst_ref in zip(ins[n_in - n_cast:], outs[len(outs) - n_cast:] if n_cast else ()):
        dst_ref[...] = src_ref[...].astype(BF16)
    rows = x_ref.shape[0]
    xn = _rms(x_ref[...], nw_ref[...]).astype(BF16)

    ab = _proj(xn, wab_ref[...])
    sp_in = ab + dtb_ref[...]
    softplus = jnp.maximum(sp_in, 0.0) + jnp.log1p(jnp.exp(-jnp.abs(sp_in)))
    gc = -jnp.exp(alog_ref[...]) * softplus
    cpos = lax.broadcasted_iota(jnp.int32, (rows, 1), 0) % chunk
    s = 1
    while s < chunk:
        gc = gc + jnp.where(cpos >= s, pltpu.roll(gc, s, 0), 0.0)
        s *= 2
    lane = lax.broadcasted_iota(jnp.int32, (1, V7X_LANES), 1)
    gcb_ref[...] = jnp.where(lane < GDN_HEADS, gc, jax.nn.sigmoid(ab))

    zga_ref[...] = _silu(_proj(xn, wz_ref[...])) * _sigmoid(_proj(xn, wga_ref[...]))
    cos_f, sin_a, sin_b = cos_ref[...], sina_ref[...], sinb_ref[...]
    qr_ref[...] = _rope(_proj(xn, wqs_ref[...]), cos_f, sin_a, sin_b)
    kr_ref[...] = _rope(_proj(xn, wks_ref[...]), cos_f, sin_a, sin_b)
    vs_ref[...] = _proj(xn, wvs_ref[...])
    sgb_ref[...] = _sigmoid(_proj(xn, wgb_ref[...]))
    if kv_t:
        ks_t = lax.dot_general(wks_ref[...], xn, (((1,), (1,)), ((), ())), preferred_element_type=F32)
        krt_ref[...] = (ks_t * cost_ref[...] + pltpu.roll(ks_t, SWA_KV_DIM - ROPE_HALF, 0) * sinat_ref[...]
                        + pltpu.roll(ks_t, ROPE_HALF, 0) * sinbt_ref[...])
        vst_ref[...] = lax.dot_general(wvs_ref[...], xn, (((1,), (1,)), ((), ())),
                                       preferred_element_type=F32)


def _mix(x, norm_w, weights, alog, dtb, tables, tables_t, to_cast=(), *, tm, table_tiles, chunk):
    t = x.shape[0]
    kv_t = tables_t is not None
    row = lambda n: pl.BlockSpec((tm, n), lambda i: (i, 0))
    tab = pl.BlockSpec((tm, V7X_LANES), lambda i: (i % table_tiles, 0))
    widths = (V7X_LANES, GDN_V_DIM, SWA_Q_DIM, SWA_KV_DIM, SWA_KV_DIM, D_MODEL)
    in_specs = ([row(D_MODEL), _const_spec((1, D_MODEL))] + [_const_spec(w.shape) for w in weights]
                + [_const_spec(alog.shape), _const_spec(dtb.shape), tab, tab, tab])
    args = [x, norm_w, *weights, alog, dtb, *tables]
    out_specs = [row(n) for n in widths]
    out_shape = [jax.ShapeDtypeStruct((t, n), F32) for n in widths]
    if kv_t:
        in_specs += [_const_spec((SWA_KV_DIM, tm))] * 3
        args += list(tables_t)
        out_specs += [pl.BlockSpec((SWA_KV_DIM, tm), lambda i: (0, i))] * 2
        out_shape += [jax.ShapeDtypeStruct((SWA_KV_DIM, t), F32)] * 2
    steps = t // tm
    for w in to_cast:
        parts = steps
        while w.shape[0] % parts or (w.shape[0] // parts) % (2 * V7X_SUBLANES):
            parts //= 2
        blk = pl.BlockSpec((w.shape[0] // parts, w.shape[1]),
                           functools.partial(lambda i, parts: (i * parts // steps, 0), parts=parts))
        in_specs.append(blk)
        args.append(w)
        out_specs.append(blk)
        out_shape.append(jax.ShapeDtypeStruct(w.shape, BF16))
    wbytes = sum(w.size for w in weights) * 2
    return pl.pallas_call(
        functools.partial(_mix_kernel, chunk=chunk, kv_t=kv_t, n_cast=len(to_cast)),
        grid=(t // tm,),
        in_specs=in_specs,
        out_specs=out_specs,
        out_shape=out_shape,
        compiler_params=pltpu.CompilerParams(
            dimension_semantics=("arbitrary",),
            vmem_limit_bytes=_vmem_limit(wbytes + 36 * tm * D_MODEL * 4 + (8 << 20))),
        name="mix",
    )(*args)


PK_C = GDN_CHUNK
PK_CAT = GDN_HEADS * PK_C
PK_PAIR = 2 * GDN_DK
PK_NPAIR = GDN_HEADS // 2
PK_GROUP = V7X_MXU_DIM // PK_C
PK_CHUNKS_PER_STEP = 4


def _pk_masks():
    c = PK_C
    i = lax.broadcasted_iota(jnp.int32, (c, PK_CAT), 0)
    j = lax.broadcasted_iota(jnp.int32, (c, PK_CAT), 1) % c
    lo = (lax.broadcasted_iota(jnp.int32, (c, V7X_LANES), 1) < c).astype(BF16)
    return i, j, (lo, 1 - lo)


def _pk_prod(x, y, bd_mask):
    lo, hi = bd_mask
    zero = jnp.zeros((PK_C, V7X_LANES), BF16)
    outs = []
    for g in range(GDN_HEADS // PK_GROUP):
        sl = slice(g * V7X_MXU_DIM, (g + 1) * V7X_MXU_DIM)
        yb = y[:, sl].astype(BF16)
        t0, t1 = yb[:, :V7X_LANES], yb[:, V7X_LANES:]
        bd = jnp.concatenate([jnp.concatenate([t0 * lo, zero], axis=1),
                              jnp.concatenate([t0 * hi, zero], axis=1),
                              jnp.concatenate([zero, t1 * lo], axis=1),
                              jnp.concatenate([zero, t1 * hi], axis=1)], axis=0)
        outs.append(jnp.dot(x[:, sl].astype(BF16), bd, preferred_element_type=F32))
    return jnp.concatenate(outs, axis=1)


def _pk_solve(a_list, i, j, bd_mask):
    prod_all = lambda xs, ys: [_pk_prod(x, y, bd_mask) for x, y in zip(xs, ys)]
    eye = (i == j).astype(F32)
    b = SOLVE_BASE
    same = (i // b) == (j // b)
    p = [jnp.where(same, a, 0.0) for a in a_list]
    d = [eye - x for x in p]
    span = 2
    while span < b:
        p = prod_all(p, p)
        d = [x + y for x, y in zip(d, prod_all(d, p))]
        span *= 2
    while b < PK_C:
        same2 = (i // (2 * b)) == (j // (2 * b))
        sel = same2 & jnp.logical_not(same)
        off = [jnp.where(sel, a, 0.0) for a in a_list]
        d = [x - y for x, y in zip(d, prod_all(d, prod_all(off, d)))]
        same = same2
        b *= 2
    return d


def _pk_pre_body(q_ref, k_ref, v_ref, gcb_ref, grow_ref, u_ref, w_ref, qg_ref, kg_ref, qk_ref, gl_ref):
    c = PK_C
    i, j, bd_mask = _pk_masks()
    lane_lo = lax.broadcasted_iota(jnp.int32, (c, V7X_LANES), 1) < c
    zeros_k = jnp.zeros((c, GDN_DK), BF16)
    chunks = [(b, ch) for b in range(q_ref.shape[0]) for ch in range(q_ref.shape[1] // c)]
    a_list, vb_list, kbg_list = [], [], []
    for b, ch in chunks:
        rows = slice(ch * c, (ch + 1) * c)
        gcb = gcb_ref[b, rows, :]
        g_end = gcb[c - 1:c, :]
        eg_s = jnp.exp(gcb)
        ek_s = jnp.exp(g_end - gcb)
        col = lambda arr, n: jnp.broadcast_to(arr[:, n:n + 1], (c, V7X_LANES))
        full = lambda arr, base: jnp.concatenate([col(arr, base + h) for h in range(GDN_HEADS)], axis=1)
        gc_cols = [col(gcb, h) for h in range(GDN_HEADS)]
        beta_f = full(gcb, GDN_HEADS)
        eg_f = full(eg_s, 0)
        ek_f = full(ek_s, 0)
        gc_cat = jnp.concatenate([jnp.where(lane_lo, gc_cols[2 * m], gc_cols[2 * m + 1])
                                  for m in range(PK_NPAIR)], axis=1)
        decay = jnp.exp(jnp.where(i >= j, gc_cat - grow_ref[b, ch], -jnp.inf))
        q = q_ref[b, rows, :].astype(F32) * (GDN_DK ** -0.5)
        k = k_ref[b, rows, :].astype(F32)
        kb = k * beta_f
        vb_list.append((v_ref[b, rows, :].astype(F32) * beta_f).astype(BF16))
        kbg_list.append((kb * eg_f).astype(BF16))
        qg_ref[b, rows, :] = (q * eg_f).astype(BF16)
        kg_ref[b, rows, :] = (k * ek_f).astype(BF16)
        gl_ref[b, ch] = eg_f[c - 1:c, :]
        kk = []
        for m in range(PK_NPAIR):
            sl = slice(m * PK_PAIR, (m + 1) * PK_PAIR)
            lhs = jnp.concatenate([kb[:, sl], q[:, sl]], axis=0).astype(BF16)
            k2 = k[:, sl].astype(BF16)
            rhs_nt = jnp.concatenate(
                [jnp.concatenate([k2[:, :GDN_DK], zeros_k], axis=1),
                 jnp.concatenate([zeros_k, k2[:, GDN_DK:]], axis=1)], axis=0)
            kk.append(lax.dot_general(lhs, rhs_nt, (((1,), (1,)), ((), ())), preferred_element_type=F32))
        a_raw = jnp.concatenate([x[:c] for x in kk], axis=1)
        qk_raw = jnp.concatenate([x[c:] for x in kk], axis=1)
        a_list.append(jnp.where(i > j, a_raw * decay, 0.0))
        qk_ref[b, rows, :] = (qk_raw * decay).astype(BF16)
    t_list = _pk_solve(a_list, i, j, bd_mask)
    for n, (b, ch) in enumerate(chunks):
        rows = slice(ch * c, (ch + 1) * c)
        t = t_list[n]
        for m in range(PK_NPAIR):
            tp = t[:, m * V7X_LANES:(m + 1) * V7X_LANES]
            lhs = jnp.concatenate([jnp.where(lane_lo, tp, 0.0), jnp.where(lane_lo, 0.0, tp)],
                                  axis=0).astype(BF16)
            h0, h1 = 2 * m, 2 * m + 1
            hs = lambda h: slice(h * GDN_DK, (h + 1) * GDN_DK)
            rhs = jnp.concatenate(
                [jnp.concatenate([vb_list[n][:, hs(h0)], kbg_list[n][:, hs(h0)]], axis=1),
                 jnp.concatenate([vb_list[n][:, hs(h1)], kbg_list[n][:, hs(h1)]], axis=1)], axis=0)
            uw = jnp.dot(lhs, rhs, preferred_element_type=F32)
            u_ref[b, rows, hs(h0)] = uw[:c, :GDN_DV]
            w_ref[b, rows, hs(h0)] = uw[:c, GDN_DV:].astype(BF16)
            u_ref[b, rows, hs(h1)] = uw[c:, :GDN_DV]
            w_ref[b, rows, hs(h1)] = uw[c:, GDN_DV:].astype(BF16)


def _pk_kernel(q_ref, k_ref, v_ref, gcb_ref, grow_ref, o_ref, sout_ref,
               u_ref, w_ref, qg_ref, kg_ref, qk_ref, gl_ref, s_ref):
    c = PK_C
    nb = u_ref.shape[0]
    step = pl.program_id(0)

    @pl.when(step == 0)
    def _():
        s_ref[...] = jnp.zeros_like(s_ref)

    _pk_pre_body(q_ref, k_ref, v_ref, gcb_ref, grow_ref, u_ref, w_ref, qg_ref, kg_ref, qk_ref, gl_ref)

    zeros_s = jnp.zeros((GDN_DK, GDN_DV), BF16)
    zeros_v = jnp.zeros((c, GDN_DV), BF16)
    chains = [(b, m) for b in range(nb) for m in range(PK_NPAIR)]
    pair = lambda m: slice(m * PK_PAIR, (m + 1) * PK_PAIR)
    state = {bm: s_ref[bm] for bm in chains}

    for ch in range(u_ref.shape[1] // c):
        rows = slice(ch * c, (ch + 1) * c)
        wq, bd_v = {}, {}

        def state_products(b, m):
            sb = state[b, m].astype(BF16)
            bd_s = jnp.concatenate([jnp.concatenate([sb[:, :GDN_DV], zeros_s], axis=1),
                                    jnp.concatenate([zeros_s, sb[:, GDN_DV:]], axis=1)], axis=0)
            lhs = jnp.concatenate([w_ref[b, rows, pair(m)], qg_ref[b, rows, pair(m)]], axis=0)
            wq[b, m] = jnp.dot(lhs, bd_s, preferred_element_type=F32)

        def new_values(b, m):
            v_new = (u_ref[b, rows, pair(m)] - wq[b, m][:c]).astype(BF16)
            bd_v[b, m] = jnp.concatenate([jnp.concatenate([v_new[:, :GDN_DV], zeros_v], axis=1),
                                          jnp.concatenate([zeros_v, v_new[:, GDN_DV:]], axis=1)], axis=0)

        def outputs_and_update(b, m):
            qk_p = qk_ref[b, rows, m * V7X_LANES:(m + 1) * V7X_LANES]
            o_ref[b, rows, pair(m)] = wq[b, m][c:] + jnp.dot(qk_p, bd_v[b, m], preferred_element_type=F32)
            kg_p = kg_ref[b, rows, pair(m)]
            kg_st = jnp.concatenate([kg_p[:, :GDN_DK], kg_p[:, GDN_DK:]], axis=0)
            upd = lax.dot_general(kg_st, bd_v[b, m], (((0,), (0,)), ((), ())),
                                  preferred_element_type=F32)
            state[b, m] = state[b, m] * gl_ref[b, ch, :, pair(m)] + upd

        for stage in (state_products, new_values, outputs_and_update):
            for b, m in chains:
                stage(b, m)

    for bm in chains:
        s_ref[bm] = state[bm]

    @pl.when(step == pl.num_programs(0) - 1)
    def _():
        for b, m in chains:
            s_c = s_ref[b, m]
            sout_ref[b, 2 * m] = s_c[:, :GDN_DV]
            sout_ref[b, 2 * m + 1] = s_c[:, GDN_DV:]


def _pk_gdn(q, k, v, gcb, grow_flat):
    b, l, _ = q.shape
    n = l // PK_C
    per = _token_tile(n, PK_CHUNKS_PER_STEP)
    rows = per * PK_C
    tok = lambda wd: pl.BlockSpec((b, rows, wd), lambda s: (0, s, 0))
    return pl.pallas_call(
        _pk_kernel,
        grid=(n // per,),
        in_specs=[tok(GDN_QK_DIM), tok(GDN_QK_DIM), tok(GDN_V_DIM), tok(V7X_LANES),
                  pl.BlockSpec((b, per, 1, PK_CAT), lambda s: (0, s, 0, 0))],
        out_specs=[tok(GDN_V_DIM),
                   pl.BlockSpec((b, GDN_HEADS, GDN_DK, GDN_DV), lambda s: (0, 0, 0, 0))],
        out_shape=[jax.ShapeDtypeStruct((b, l, GDN_V_DIM), F32),
                   jax.ShapeDtypeStruct((b, GDN_HEADS, GDN_DK, GDN_DV), F32)],
        scratch_shapes=[pltpu.VMEM((b, rows, GDN_V_DIM), F32),
                        pltpu.VMEM((b, rows, GDN_QK_DIM), BF16),
                        pltpu.VMEM((b, rows, GDN_QK_DIM), BF16),
                        pltpu.VMEM((b, rows, GDN_QK_DIM), BF16),
                        pltpu.VMEM((b, rows, PK_CAT), BF16),
                        pltpu.VMEM((b, per, 1, GDN_V_DIM), F32),
                        pltpu.VMEM((b, PK_NPAIR, GDN_DK, 2 * GDN_DV), F32)],
        compiler_params=pltpu.CompilerParams(dimension_semantics=("arbitrary",),
                                             vmem_limit_bytes=_vmem_limit(V7X_VMEM_BYTES)),
        name="gdn_prompt",
    )(q, k, v, gcb, grow_flat)


def _head_block_rows(x, t):
    width = x.shape[1] // GDN_HEADS
    z = jnp.zeros((t, width), x.dtype)
    rows = []
    for h in range(GDN_HEADS):
        rows.append(jnp.concatenate([z] * h + [x[:, h * width:(h + 1) * width]]
                                    + [z] * (GDN_HEADS - 1 - h), axis=1))
    return jnp.concatenate(rows, axis=0)


def _gdn_sample_kernel(q_ref, k_ref, v_ref, gcb_ref, grep_ref, grow_ref, s0_ref, o_ref, sout_ref):
    bt, t, _ = q_ref.shape
    cat = GDN_HEADS * t
    i = lax.broadcasted_iota(jnp.int32, (t, cat), 0)
    j = lax.broadcasted_iota(jnp.int32, (t, cat), 1) % t
    r = lax.broadcasted_iota(jnp.int32, (cat, cat), 0) // t
    l = lax.broadcasted_iota(jnp.int32, (cat, cat), 1) // t
    same_head = r == l
    eye = (i == j).astype(F32)
    seqs = range(bt)
    pair = lambda m: slice(m * PK_PAIR, (m + 1) * PK_PAIR)

    vb, kbg, qg, kg, gl, a, qk = {}, {}, {}, {}, {}, {}, {}
    for b in seqs:
        gcb = gcb_ref[b]
        g_end = gcb[t - 1:t, :]
        eg_s = jnp.exp(gcb)
        ek_s = jnp.exp(g_end - gcb)
        col = lambda arr, n: jnp.broadcast_to(arr[:, n:n + 1], (t, V7X_LANES))
        full = lambda arr, base: jnp.concatenate([col(arr, base + h) for h in range(GDN_HEADS)], axis=1)
        beta_f, eg_f, ek_f = full(gcb, GDN_HEADS), full(eg_s, 0), full(ek_s, 0)
        decay = jnp.exp(jnp.where(i >= j, grep_ref[b] - grow_ref[b], -jnp.inf))
        q = q_ref[b].astype(F32) * (GDN_DK ** -0.5)
        k = k_ref[b].astype(F32)
        kb = k * beta_f
        vb[b] = v_ref[b].astype(F32) * beta_f
        kbg[b] = kb * eg_f
        qg[b] = q * eg_f
        kg[b] = k * ek_f
        gl[b] = eg_f[t - 1:t, :]
        kk = _dot_nt(jnp.concatenate([kb, q], axis=0), _head_block_rows(k, t))
        a[b] = jnp.where(i > j, kk[:t] * decay, 0.0)
        qk[b] = kk[t:] * decay

    def prod_all(xs, ys):
        return {b: _dot(xs[b], jnp.where(same_head, jnp.concatenate([ys[b]] * GDN_HEADS, axis=0), 0.0))
                for b in seqs}

    d = {b: eye - a[b] for b in seqs}
    p = a
    span = 2
    while span < t:
        p = prod_all(p, p)
        dp = prod_all(d, p)
        d = {b: d[b] + dp[b] for b in seqs}
        span *= 2

    u, w = {}, {}
    for b in seqs:
        rhs = jnp.concatenate([_head_block_rows(vb[b], t), _head_block_rows(kbg[b], t)], axis=1)
        uw = _dot(d[b], rhs)
        u[b], w[b] = uw[:, :GDN_V_DIM], uw[:, GDN_V_DIM:]

    zeros_s = jnp.zeros((GDN_DK, GDN_DV), BF16)
    wq = {}
    for b in seqs:
        for m in range(PK_NPAIR):
            s0 = s0_ref[b, 2 * m].astype(BF16)
            s1 = s0_ref[b, 2 * m + 1].astype(BF16)
            bd_s = jnp.concatenate([jnp.concatenate([s0, zeros_s], axis=1),
                                    jnp.concatenate([zeros_s, s1], axis=1)], axis=0)
            lhs = jnp.concatenate([w[b][:, pair(m)], qg[b][:, pair(m)]], axis=0).astype(BF16)
            wq[b, m] = jnp.dot(lhs, bd_s, preferred_element_type=F32)
    v_new = {b: u[b] - jnp.concatenate([wq[b, m][:t] for m in range(PK_NPAIR)], axis=1) for b in seqs}
    zeros_v = jnp.zeros((t, GDN_DV), F32)
    for b in seqs:
        o_ref[b] = (jnp.concatenate([wq[b, m][t:] for m in range(PK_NPAIR)], axis=1)
                    + _dot(qk[b], _head_block_rows(v_new[b], t)))
        for m in range(PK_NPAIR):
            vp = v_new[b][:, pair(m)]
            bd_v = jnp.concatenate([jnp.concatenate([vp[:, :GDN_DV], zeros_v], axis=1),
                                    jnp.concatenate([zeros_v, vp[:, GDN_DV:]], axis=1)], axis=0)
            kp = kg[b][:, pair(m)]
            kg_st = jnp.concatenate([kp[:, :GDN_DK], kp[:, GDN_DK:]], axis=0)
            upd = _dot_tn(kg_st, bd_v)
            glp = gl[b][:, pair(m)]
            sout_ref[b, 2 * m] = s0_ref[b, 2 * m] * glp[:, :GDN_DV] + upd[:, :GDN_DV]
            sout_ref[b, 2 * m + 1] = s0_ref[b, 2 * m + 1] * glp[:, GDN_DV:] + upd[:, GDN_DV:]


def _gdn_sample(q, k, v, gcb, grep, grow_flat, s0, bt):
    b, t, _ = q.shape
    assert t == V7X_SUBLANES, "one sublane tile of steps per sequence"
    cat = GDN_HEADS * t
    tok = lambda wd: pl.BlockSpec((bt, t, wd), lambda i: (i, 0, 0))
    state = pl.BlockSpec((bt, GDN_HEADS, GDN_DK, GDN_DV), lambda i: (i, 0, 0, 0))
    return pl.pallas_call(
        _gdn_sample_kernel,
        grid=(b // bt,),
        in_specs=[tok(GDN_QK_DIM), tok(GDN_QK_DIM), tok(GDN_V_DIM), tok(V7X_LANES), tok(cat),
                  pl.BlockSpec((bt, 1, cat), lambda i: (i, 0, 0)), state],
        out_specs=[tok(GDN_V_DIM), state],
        out_shape=[jax.ShapeDtypeStruct((b, t, GDN_V_DIM), F32),
                   jax.ShapeDtypeStruct((b, GDN_HEADS, GDN_DK, GDN_DV), F32)],
        compiler_params=pltpu.CompilerParams(dimension_semantics=("parallel",),
                                             vmem_limit_bytes=_vmem_limit(40 << 20)),
        name="gdn_sample",
    )(q, k, v, gcb, grep, grow_flat, s0)


def _swa_prompt_kernel(q_ref, kc_ref, kp_ref, vc_ref, vp_ref, sgb_ref, sink_ref, o_ref):
    w = WINDOW
    hd = SWA_HEAD_DIM
    nkeys = 2 * w
    ncols = 2 * w
    nwin = q_ref.shape[1] // w
    has_prev = pl.program_id(1) > 0
    kj = lax.broadcasted_iota(jnp.int32, (nkeys, ncols), 0)
    qi = lax.broadcasted_iota(jnp.int32, (nkeys, ncols), 1) % w
    mask_inner = ((kj >= w) & ((kj - w) <= qi)) | ((kj < w) & (kj > qi))
    masks = [mask_inner & ((kj >= w) | has_prev)] + [mask_inner] * (nwin - 1)
    lo_keys = lax.broadcasted_iota(jnp.int32, (nkeys, V7X_LANES), 1) < hd
    lo_q = lax.broadcasted_iota(jnp.int32, (w, V7X_LANES), 1) < hd
    scale = hd ** -0.5 * LOG2_E
    problems = [(t, h, gp) for t in range(nwin) for h in range(SWA_KV_HEADS)
                for gp in range(SWA_GROUP // 2)]

    kk, vbd = {}, {}
    for hp in range(SWA_KV_HEADS // 2):
        sl = slice(hp * V7X_LANES, (hp + 1) * V7X_LANES)
        k_all = jnp.concatenate([kp_ref[0, :, sl], kc_ref[0, :, sl]], axis=0)
        v_all = jnp.concatenate([vp_ref[0, :, sl], vc_ref[0, :, sl]], axis=0)
        kr_all = pltpu.roll(k_all, hd, 1)
        vr_all = pltpu.roll(v_all, hd, 1)
        for t in range(nwin):
            keys = slice(t * w, (t + 2) * w)
            kt, kr, vt, vr = k_all[keys], kr_all[keys], v_all[keys], vr_all[keys]
            for par, h in enumerate((2 * hp, 2 * hp + 1)):
                kk[t, h] = (jnp.where(lo_keys, kt, kr) if par == 0
                            else jnp.where(lo_keys, kr, kt)).astype(BF16)
                v_lo, v_hi = (vt, vr) if par == 0 else (vr, vt)
                vbd[t, h] = jnp.concatenate([jnp.where(lo_keys, v_lo, 0.0), jnp.where(lo_keys, 0.0, v_hi)],
                                            axis=0).astype(BF16)
    s, pn = {}, {}

    def scores(t, h, gp):
        lo = (h * SWA_GROUP + 2 * gp) * hd
        qp = q_ref[0, t * w:(t + 1) * w, lo:lo + V7X_LANES] * scale
        rq = jnp.concatenate([jnp.where(lo_q, qp, 0.0), jnp.where(lo_q, 0.0, qp)], axis=0).astype(BF16)
        sc = lax.dot_general(kk[t, h], rq, (((1,), (1,)), ((), ())), preferred_element_type=F32)
        s[t, h, gp] = jnp.where(masks[t], sc, -jnp.inf)

    def softmax(t, h, gp):
        sink = sink_ref[h * (SWA_GROUP // 2) + gp] * LOG2_E
        m = jnp.maximum(jnp.max(s[t, h, gp], axis=0, keepdims=True), sink)
        p = jnp.exp2(s[t, h, gp] - m)
        denom = jnp.sum(p, axis=0, keepdims=True) + jnp.exp2(sink - m)
        p = (p * (1.0 / denom)).astype(BF16)
        pn[t, h, gp] = jnp.concatenate([p[:, :w], p[:, w:]], axis=0)

    def values(t, h, gp):
        lo = (h * SWA_GROUP + 2 * gp) * hd
        rows = slice(t * w, (t + 1) * w)
        o = lax.dot_general(pn[t, h, gp], vbd[t, h], (((0,), (0,)), ((), ())), preferred_element_type=F32)
        o_ref[0, rows, lo:lo + V7X_LANES] = o * sgb_ref[0, rows, lo:lo + V7X_LANES]

    for stage in (scores, softmax, values):
        for prob in problems:
            stage(*prob)


def _swa_prompt(q, k, v, sgb, sink_cols):
    b, l, _ = q.shape
    nwin = _token_tile(l // WINDOW, SWA_WINDOWS_PER_STEP)
    nb = l // (nwin * WINDOW)
    cur = lambda wd: pl.BlockSpec((1, nwin * WINDOW, wd), lambda i, j: (i, j, 0))
    prev = lambda wd: pl.BlockSpec((1, WINDOW, wd), lambda i, j: (i, jnp.maximum(j * nwin - 1, 0), 0))
    return pl.pallas_call(
        _swa_prompt_kernel,
        grid=(b, nb),
        in_specs=[cur(SWA_Q_DIM), cur(SWA_KV_DIM), prev(SWA_KV_DIM), cur(SWA_KV_DIM), prev(SWA_KV_DIM),
                  cur(SWA_Q_DIM), pl.BlockSpec(sink_cols.shape, lambda i, j: (0, 0, 0))],
        out_specs=cur(SWA_Q_DIM),
        out_shape=jax.ShapeDtypeStruct((b, l, SWA_Q_DIM), F32),
        compiler_params=pltpu.CompilerParams(dimension_semantics=("parallel", "parallel")),
        name="swa_prompt",
    )(q, k, k, v, v, sgb, sink_cols)


def _swa_sample_kernel(q_ref, kn_ref, vn_ref, knt_ref, vnt_ref, kb_ref, vb_ref, sgb_ref, sink_ref,
                       o_ref, ko_ref, vo_ref, *, steps):
    t = steps
    bt = q_ref.shape[0]
    nrows = kb_ref.shape[2]
    rows = SWA_GROUP * t
    qt_b = lax.broadcasted_iota(jnp.int32, (rows, nrows), 0) % t
    ki_b = lax.broadcasted_iota(jnp.int32, (rows, nrows), 1)
    mask_buf = ((nrows - ki_b + qt_b) < WINDOW)[None]
    qt_n = lax.broadcasted_iota(jnp.int32, (rows, t), 0) % t
    ki_n = lax.broadcasted_iota(jnp.int32, (rows, t), 1)
    mask_new = (ki_n <= qt_n)[None]
    scale = SWA_HEAD_DIM ** -0.5
    bdot = lambda a, b_, dims: lax.dot_general(a.astype(BF16), b_.astype(BF16), dims,
                                               preferred_element_type=F32)
    nt_dims = (((2,), (2,)), ((0,), (0,)))
    nn_dims = (((2,), (1,)), ((0,), (0,)))
    s_b, s_n, p_b, p_n, denom = {}, {}, {}, {}, {}

    def scores(h):
        klo = h * SWA_HEAD_DIM
        q_h = jnp.concatenate(
            [q_ref[:, :, (h * SWA_GROUP + g) * SWA_HEAD_DIM:(h * SWA_GROUP + g + 1) * SWA_HEAD_DIM]
             for g in range(SWA_GROUP)], axis=1)
        k_b = kb_ref[:, klo:klo + SWA_HEAD_DIM, :]
        k_n = kn_ref[:, :, klo:klo + SWA_HEAD_DIM]
        s_b[h] = jnp.where(mask_buf, bdot(q_h, k_b, nn_dims) * scale, -jnp.inf)
        s_n[h] = jnp.where(mask_new, bdot(q_h, k_n, nt_dims) * scale, -jnp.inf)

    def softmax(h):
        sink = sink_ref[h][None]
        m = jnp.maximum(jnp.maximum(jnp.max(s_b[h], axis=-1, keepdims=True),
                                    jnp.max(s_n[h], axis=-1, keepdims=True)), sink)
        p_b[h] = jnp.exp(s_b[h] - m)
        p_n[h] = jnp.exp(s_n[h] - m)
        denom[h] = (jnp.sum(p_b[h], axis=-1, keepdims=True) + jnp.sum(p_n[h], axis=-1, keepdims=True)
                    + jnp.exp(sink - m))

    def values(h):
        klo = h * SWA_HEAD_DIM
        v_b = vb_ref[:, klo:klo + SWA_HEAD_DIM, :]
        v_n = vn_ref[:, :, klo:klo + SWA_HEAD_DIM]
        o = (bdot(p_b[h], v_b, nt_dims) + bdot(p_n[h], v_n, nn_dims)) / denom[h]
        for g in range(SWA_GROUP):
            lo = (h * SWA_GROUP + g) * SWA_HEAD_DIM
            o_ref[:, :, lo:lo + SWA_HEAD_DIM] = (o[:, g * t:(g + 1) * t]
                                                 * sgb_ref[:, :, lo:lo + SWA_HEAD_DIM])

    for stage in (scores, softmax, values):
        for h in range(SWA_KV_HEADS):
            stage(h)
    is_new = lax.broadcasted_iota(jnp.int32, (SWA_KV_DIM, nrows), 1) >= nrows - t
    for new_ref, buf_ref, out_ref in ((knt_ref, kb_ref, ko_ref), (vnt_ref, vb_ref, vo_ref)):
        fresh = new_ref[...]
        for i in range(bt):
            kept = pltpu.roll(buf_ref[i], nrows - t, 1)
            out_ref[i] = jnp.where(is_new, pltpu.roll(fresh, (nrows - t - i * t) % nrows, 1), kept)


def _swa_sample(q, k, v, k_t, v_t, k_buf, v_buf, sgb, sink_rows, bt):
    b, t, _ = q.shape
    nrows = k_buf.shape[2]
    assert bt * t == nrows == V7X_LANES, "one lane tile of new columns per step"
    new = lambda wd: pl.BlockSpec((bt, t, wd), lambda i: (i, 0, 0))
    new_t = pl.BlockSpec((SWA_KV_DIM, bt * t), lambda i: (0, i))
    buf = pl.BlockSpec((bt, SWA_KV_DIM, nrows), lambda i: (i, 0, 0))
    kern = functools.partial(_swa_sample_kernel, steps=t)
    return pl.pallas_call(
        kern,
        grid=(b // bt,),
        in_specs=[new(SWA_Q_DIM), new(SWA_KV_DIM), new(SWA_KV_DIM), new_t, new_t, buf, buf, new(SWA_Q_DIM),
                  pl.BlockSpec(sink_rows.shape, lambda i: (0, 0, 0))],
        out_specs=[new(SWA_Q_DIM), buf, buf],
        out_shape=[jax.ShapeDtypeStruct((b, t, SWA_Q_DIM), F32),
                   jax.ShapeDtypeStruct((b, SWA_KV_DIM, nrows), F32),
                   jax.ShapeDtypeStruct((b, SWA_KV_DIM, nrows), F32)],
        compiler_params=pltpu.CompilerParams(dimension_semantics=("parallel",)),
        name="swa_sample",
    )(q, k, v, k_t, v_t, k_buf, v_buf, sgb, sink_rows)


def _out_kernel(x_ref, og_ref, zga_ref, osg_ref, gn_ref, wo_ref, nw2_ref, win_ref, wout_ref, nf_ref, y_ref):
    gn = gn_ref[...]
    heads = []
    for h in range(GDN_HEADS):
        lo = h * GDN_DV
        heads.append(_rms(og_ref[:, lo:lo + GDN_DV], gn))
    mixed = jnp.concatenate(heads, axis=1) * zga_ref[...] + osg_ref[...]
    x2 = x_ref[...] + jnp.dot(mixed.astype(BF16), wo_ref[...], preferred_element_type=F32)
    x3 = _swiglu_half_step(x2, nw2_ref[...], win_ref, wout_ref)
    y_ref[...] = _rms(x3, nf_ref[...])


def _out(x1, og, zga, osg, gnorm, wo, nw2, win, wout, nf, tm):
    t = x1.shape[0]
    row = pl.BlockSpec((tm, D_MODEL), lambda i: (i, 0))
    wbytes = (wo.size + win.size + wout.size) * 2
    return pl.pallas_call(
        _out_kernel,
        grid=(t // tm,),
        in_specs=[row] * 4 + [_const_spec(gnorm.shape), _const_spec(wo.shape), _const_spec(nw2.shape),
                              _const_spec(win.shape), _const_spec(wout.shape), _const_spec(nf.shape)],
        out_specs=row,
        out_shape=jax.ShapeDtypeStruct((t, D_MODEL), F32),
        compiler_params=pltpu.CompilerParams(
            dimension_semantics=("parallel",),
            vmem_limit_bytes=_vmem_limit(wbytes + 18 * tm * D_MODEL * 4 + (8 << 20))),
        name="out",
    )(x1, og, zga, osg, gnorm, wo, nw2, win, wout, nf)


def _rope_tables(pos):
    inv = ROPE_THETA ** (-jnp.arange(0, ROPE_DIM, 2, dtype=F32) / ROPE_DIM)
    ang = pos.astype(F32)[:, None] * inv[None, :]
    cos, sin = jnp.cos(ang), jnp.sin(ang)
    n = pos.shape[0]
    pad = SWA_HEAD_DIM - ROPE_DIM
    one_head = lambda first, second, fill: jnp.concatenate(
        [first, second, jnp.full((n, pad), fill, F32)], axis=1)
    zeros = jnp.zeros_like(sin)
    cos_f = one_head(cos, cos, 1.0)
    sin_a = one_head(-sin, zeros, 0.0)
    sin_b = one_head(zeros, sin, 0.0)
    reps = V7X_LANES // SWA_HEAD_DIM
    return tuple(jnp.tile(t, (1, reps)) for t in (cos_f, sin_a, sin_b))


def _pad_lanes(v, width):
    return jnp.pad(v.astype(F32), (0, width - v.shape[0]))[None, :]


def _branch(x, pos, conv_buf, s0, k_buf, v_buf, p, late_weights=None):
    b, l, _ = x.shape
    t = b * l
    is_prompt = conv_buf is None
    chunk = min(GDN_CHUNK, l)
    n = l // chunk
    tm = _token_tile(l if is_prompt else t, TOKEN_TILE)
    tm_mix = _token_tile(l if is_prompt else t, MIX_TILE)
    tables = _rope_tables(pos)
    if is_prompt:
        halo = None
        table_tiles = l // tm_mix
    else:
        halo = jnp.pad(conv_buf, ((0, 0), (l - (CONV_W - 1), 0), (0, 0))).reshape(t, CONV_DIM)
        heads_t = lambda tb: jnp.tile(tb.T, (SWA_KV_DIM // V7X_LANES, tm_mix // l))
        tables_t = tuple(heads_t(tb) for tb in tables)
        tables = tuple(jnp.tile(tb, (tm_mix // l, 1)) for tb in tables)
        table_tiles = 1
    tm_conv = tm if is_prompt else _token_tile(t, TOKEN_TILE // 2)
    x1, qg, kg, vg, tail = _ffn_conv(x.reshape(t, D_MODEL), p['norm_ffn1'], p['w_ffn1_in'], p['w_ffn1_out'],
                                     p['norm_mix'], p['w_qkv'], p['conv_w'], halo,
                                     tm=tm_conv, seq_len=l, group=l)
    to_cast = p['late_f32'] if late_weights is None else ()
    mixed = _mix(x1, p['norm_mix'], p['w_mix'], p['a_log_pad'], p['dt_bias_pad'], tables,
                 None if is_prompt else tables_t, to_cast,
                 tm=tm_mix, table_tiles=table_tiles, chunk=chunk)
    if late_weights is None:
        late_weights = tuple(mixed[len(mixed) - len(to_cast):])
    w_out_b, w_ffn2_in_b, w_ffn2_out_b = late_weights
    gcb, zga, qr, kr, vs, sgb = mixed[:6]
    seq = lambda a: a.reshape(b, l, a.shape[-1])
    qg, kg, vg, gcb, qr, kr, vs, sgb = (seq(a) for a in (qg, kg, vg, gcb, qr, kr, vs, sgb))
    new_conv = tail.reshape(b, -1, CONV_DIM)[:, -(CONV_W - 1):]

    gc = gcb[:, :, :GDN_HEADS]
    grow = gc.reshape(b, n, chunk, GDN_HEADS).transpose(0, 1, 3, 2)
    if is_prompt:
        assert chunk == PK_C and s0 is None
        og, new_s = _pk_gdn(qg, kg, vg, gcb, grow.reshape(b, n, 1, PK_CAT))
        sink_cols = jnp.repeat(p['sinks'].reshape(SWA_Q_HEADS // 2, 2), WINDOW, axis=1)
        osg = _swa_prompt(qr, kr, vs, sgb, sink_cols.reshape(SWA_Q_HEADS // 2, 1, 2 * WINDOW))
        new_k, new_v = kr[:, l - WINDOW:], vs[:, l - WINDOW:]
    else:
        assert n == 1
        og, new_s = _gdn_sample(qg, kg, vg, gcb, jnp.repeat(gc, l, axis=-1),
                                grow.reshape(b, 1, GDN_HEADS * l), s0, _token_tile(b, 8))
        sink_rows = jnp.repeat(p['sinks'], l, axis=1).reshape(SWA_KV_HEADS, SWA_GROUP * l, 1)
        nrows = k_buf.shape[1]
        to_minor = lambda c: c.transpose(0, 2, 3, 1).reshape(b, SWA_KV_DIM, nrows)
        from_minor = lambda c: c.reshape(b, SWA_KV_HEADS, SWA_HEAD_DIM, nrows).transpose(0, 3, 1, 2)
        osg, new_k, new_v = _swa_sample(qr, kr, vs, mixed[6], mixed[7], to_minor(k_buf), to_minor(v_buf),
                                        sgb, sink_rows, nrows // l)
        new_k, new_v = from_minor(new_k), from_minor(new_v)
    y = _out(x1, og.reshape(t, GDN_V_DIM), zga, osg.reshape(t, SWA_Q_DIM), p['gdn_norm'], w_out_b,
             p['norm_ffn2'], w_ffn2_in_b, w_ffn2_out_b, p['norm_final'], tm)
    kv_shape = (b, new_k.shape[1], SWA_KV_HEADS, SWA_HEAD_DIM)
    return (y.reshape(b, l, D_MODEL), new_conv, new_s, new_k.reshape(kv_shape), new_v.reshape(kv_shape),
            late_weights)


def _layer_params(l, norm_ffn1, w_ffn1_in, w_ffn1_out, norm_mix, w_in, conv_w, gdn_a_log, gdn_dt_bias,
                  gdn_norm, swa_sinks, w_out, norm_ffn2, w_ffn2_in, w_ffn2_out, norm_final):
    bounds = [0]
    for s in IN_SPLITS:
        bounds.append(bounds[-1] + s)
    w_in_t = jnp.swapaxes(w_in[l], 0, 1)
    col = lambda i: w_in_t[bounds[i]:bounds[i + 1]].astype(BF16)
    w_ab = jnp.pad(jnp.concatenate([col(2), col(3)], axis=0), ((0, V7X_LANES - 2 * GDN_HEADS), (0, 0)))
    row = lambda v: v.astype(F32)[None, :]
    return {
        'norm_ffn1': row(norm_ffn1[l]), 'w_ffn1_in': w_ffn1_in[l].astype(BF16),
        'w_ffn1_out': w_ffn1_out[l].astype(BF16),
        'norm_mix': row(norm_mix[l]),
        'w_qkv': col(0),
        'w_mix': [w_ab, col(1), col(7), col(4), col(5), col(6), col(8)],
        'conv_w': conv_w[l].astype(F32),
        'a_log_pad': _pad_lanes(gdn_a_log[l], V7X_LANES), 'dt_bias_pad': _pad_lanes(gdn_dt_bias[l], V7X_LANES),
        'gdn_norm': row(gdn_norm[l]),
        'sinks': swa_sinks[l].astype(F32).reshape(SWA_KV_HEADS, SWA_GROUP),
        'late_f32': (w_out[l], w_ffn2_in[l], w_ffn2_out[l]),
        'norm_ffn2': row(norm_ffn2[l]),
        'norm_final': row(norm_final),
    }


def kernel(x_prompt, x_sample, state_conv, state_gdn, cache_swa_k, cache_swa_v, norm_ffn1, w_ffn1_in, w_ffn1_out, norm_mix, w_in, conv_w, gdn_a_log, gdn_dt_bias, gdn_norm, swa_sinks, w_out, norm_ffn2, w_ffn2_in, w_ffn2_out, norm_final):
    depth = w_in.shape[0]
    assert depth == 1, "the final norm is fused into the layer's last kernel"
    l_p = x_prompt.shape[1]
    l_s = x_sample.shape[1]
    pos_p = jnp.arange(l_p, dtype=jnp.int32)
    pos_s = PAST_LEN + jnp.arange(l_s, dtype=jnp.int32)
    p = _layer_params(0, norm_ffn1, w_ffn1_in, w_ffn1_out, norm_mix, w_in, conv_w, gdn_a_log, gdn_dt_bias,
                      gdn_norm, swa_sinks, w_out, norm_ffn2, w_ffn2_in, w_ffn2_out, norm_final)
    yp, c1, s1, k1, v1, late = _branch(x_prompt, pos_p, None, None, None, None, p)
    ys, c2, s2, k2, v2, _ = _branch(x_sample, pos_s, state_conv[0], state_gdn[0], cache_swa_k[0],
                                    cache_swa_v[0], p, late)
    return (yp, ys, c1[None], c2[None], s1[None], s2[None], k1[None], k2[None], v1[None], v2[None])
```
